```python
import math
import jax
import jax.numpy as jnp
from jax import lax
import numpy as np

D_MODEL = 1024
BATCH = 2
SEQ = 8192
DEPTH = 2

GRID_W = 64
CTX_LEN = 256

HY_W = 512
SHORT_K = 3
FILT_BANDS = 16
FILT_EMB = 1 + 2 * FILT_BANDS
FILT_W = 64
DECAY_TARGET = 1e-2
FAST_DECAY_PCT = 0.3
SLOW_DECAY_PCT = 1.5

MLA_HEADS = 8
MLA_NOPE = 64
MLA_ROPE = 32
MLA_V = 64
MLA_QK = MLA_NOPE + MLA_ROPE
MLA_Q_RANK = 256
MLA_KV_RANK = 128

SWA_HEADS = 8
SWA_KV_HEADS = 2
SWA_HD = 64
WINDOW = 128

N_BRANCH = 3
QBLOCK = 128
ROPE_BASE = 10000.0
EPS = 1e-6
NEG = -1e30
ALPHA = (2 * DEPTH) ** 0.25
BETA = (8 * DEPTH) ** -0.25

COLS = (3 * HY_W, HY_W,
        MLA_Q_RANK, MLA_KV_RANK, MLA_ROPE, MLA_HEADS * MLA_V,
        SWA_HEADS * SWA_HD, SWA_KV_HEADS * SWA_HD, SWA_KV_HEADS * SWA_HD, SWA_HEADS * SWA_HD,
        N_BRANCH * D_MODEL)
SPLITS = tuple(int(s) for s in np.cumsum(COLS)[:-1])
P_TOTAL = int(sum(COLS))

kernel_name = 'hybrid_hyena_mla_swa_prefix_trunk'


def _layer_norm(x, g=None, b=None):
    xf = x.astype(jnp.float32)
    mu = jnp.mean(xf, -1, keepdims=True)
    var = jnp.mean(jnp.square(xf - mu), -1, keepdims=True)
    y = (xf - mu) * lax.rsqrt(var + EPS)
    if g is not None:
        y = y * g.astype(jnp.float32) + b.astype(jnp.float32)
    return y.astype(x.dtype)


def _rms_norm(x, g):
    xf = x.astype(jnp.float32)
    y = xf * lax.rsqrt(jnp.mean(jnp.square(xf), -1, keepdims=True) + EPS) * g.astype(jnp.float32)
    return y.astype(x.dtype)


def _split_heads(t, h):
    return t.reshape(t.shape[:-1] + (h, t.shape[-1] // h))


def _rope_1d(x, pos):
    half = x.shape[-1] // 2
    inv = ROPE_BASE ** (-jnp.arange(half, dtype=jnp.float32) / half)
    ang = pos.astype(jnp.float32)[:, None] * inv[None, :]
    cos = jnp.cos(ang)[:, None, :]
    sin = jnp.sin(ang)[:, None, :]
    xf = x.astype(jnp.float32)
    x1, x2 = xf[..., :half], xf[..., half:]
    return jnp.concatenate([x1 * cos - x2 * sin, x1 * sin + x2 * cos], -1).astype(x.dtype)


def _rope_2d(x, rows, cols):
    h = x.shape[-1] // 2
    return jnp.concatenate([_rope_1d(x[..., :h], rows), _rope_1d(x[..., h:], cols)], -1)


def _short_conv(x, w, b):
    L = x.shape[1]
    pad = SHORT_K // 2
    xp = jnp.pad(x, ((0, 0), (pad, pad), (0, 0)))
    y = b
    for k in range(SHORT_K):
        y = y + xp[:, k:k + L] * w[k]
    return y


def _hyena_filters(L, p):
    t = jnp.linspace(0.0, 1.0, L, dtype=jnp.float32)[:, None]
    w = 2.0 * math.pi * jnp.arange(L, dtype=jnp.float32) / L
    f = jnp.linspace(1e-4, FILT_BANDS - 1, FILT_BANDS, dtype=jnp.float32)
    ang = w[:, None] * f[None, :]
    z = jnp.concatenate([t, jnp.cos(ang), -jnp.sin(ang)], -1)
    fr = p['filt_freq'].astype(jnp.float32)
    h = jnp.sin(fr * (z @ p['filt_w1'].astype(jnp.float32) + p['filt_b1'].astype(jnp.float32)))
    h = jnp.sin(fr * (h @ p['filt_w2'].astype(jnp.float32) + p['filt_b2'].astype(jnp.float32)))
    h = jnp.sin(fr * (h @ p['filt_w3'].astype(jnp.float32) + p['filt_b3'].astype(jnp.float32)))
    h = h @ p['filt_w_out'].astype(jnp.float32)
    min_decay = math.log(DECAY_TARGET) / SLOW_DECAY_PCT
    max_decay = math.log(DECAY_TARGET) / FAST_DECAY_PCT
    deltas = jnp.linspace(min_decay, max_decay, HY_W, dtype=jnp.float32)
    decay = jnp.exp(-t * jnp.abs(deltas)[None, :])
    h = h.reshape(L, 2, HY_W) * decay[:, None, :]
    return h[:, 0], h[:, 1]


def _bidir_long_conv(u, h_fwd, h_bwd, skip):
    L = u.shape[1]
    n = 2 * L
    k_full = jnp.concatenate([h_fwd, jnp.zeros((1, HY_W), jnp.float32), h_bwd[1:][::-1]], 0)
    uf = u.astype(jnp.float32)
    y = jnp.fft.irfft(jnp.fft.rfft(uf, n=n, axis=1) * jnp.fft.rfft(k_full, n=n, axis=0)[None], n=n, axis=1)[:, :L]
    return (y + uf * skip.astype(jnp.float32)).astype(u.dtype)


def _hyena_mixer(xv, p):
    L = xv.shape[1]
    z = _short_conv(xv, p['hy_conv_w'], p['hy_conv_b'])
    x0, x1, v = jnp.split(z, 3, axis=-1)
    h_fwd, h_bwd = _hyena_filters(L, p)
    return _bidir_long_conv(v * x1, h_fwd, h_bwd, p['hy_skip']) * x0


def _mla_q(cq, p, rows, cols):
    q = _split_heads(_rms_norm(cq, p['mla_q_norm']) @ p['mla_w_uq'], MLA_HEADS)
    q_nope, q_rope = q[..., :MLA_NOPE], q[..., MLA_NOPE:]
    if rows is not None:
        q_rope = _rope_2d(q_rope, rows, cols)
    return jnp.concatenate([q_nope, q_rope], -1)


def _mla_kv(ckv, kr, p, rows, cols):
    kv = _split_heads(_rms_norm(ckv, p['mla_kv_norm']) @ p['mla_w_ukv'], MLA_HEADS)
    k_nope, v = kv[..., :MLA_NOPE], kv[..., MLA_NOPE:]
    kr = kr[..., None, :]
    if rows is not None:
        kr = _rope_2d(kr, rows, cols)
    k = jnp.concatenate([k_nope, jnp.broadcast_to(kr, k_nope.shape[:-1] + (MLA_ROPE,))], -1)
    return k, v


def _dense_attention(q, k, v, scale):
    s = jnp.einsum('bqhd,bkhd->bhqk', q, k).astype(jnp.float32) * scale
    pr = jax.nn.softmax(s, -1).astype(v.dtype)
    return jnp.einsum('bhqk,bkhd->bqhd', pr, v)


def _block_dense_attention(q, k, v, scale):
    B, S, H, Dq = q.shape
    nb = S // QBLOCK
    qb = jnp.swapaxes(q.reshape(B, nb, QBLOCK, H, Dq), 0, 1)
    o = lax.map(lambda qi: _dense_attention(qi, k, v, scale), qb)
    return jnp.swapaxes(o, 0, 1).reshape(B, S, H * v.shape[-1])


def _swa_latent(q, k, v, kc, vc, sink):
    B, S, H, Dh = q.shape
    G = k.shape[2]
    R = H // G
    W = WINDOW
    nb = S // W
    Lc = kc.shape[1]
    scale = Dh ** -0.5
    qb = jnp.swapaxes(q.reshape(B, nb, W, G, R, Dh), 0, 1)
    pad = ((0, 0), (W, W), (0, 0), (0, 0))
    kp = jnp.pad(k, pad)
    vp = jnp.pad(v, pad)
    band = jnp.abs(W + jnp.arange(W)[:, None] - jnp.arange(3 * W)[None, :]) <= W
    sink_l = jnp.broadcast_to(sink.astype(jnp.float32).reshape(G, R)[None, :, :, None, None], (B, G, R, W, 1))

    def block(args):
        i, qi = args
        kw = lax.dynamic_slice_in_dim(kp, i * W, 3 * W, axis=1)
        vw = lax.dynamic_slice_in_dim(vp, i * W, 3 * W, axis=1)
        kpos = (i - 1) * W + jnp.arange(3 * W)
        valid = band & ((kpos >= 0) & (kpos < S))[None, :]
        s_w = jnp.einsum('bqgrd,bkgd->bgrqk', qi, kw).astype(jnp.float32) * scale
        s_w = jnp.where(valid, s_w, NEG)
        s_c = jnp.einsum('bqgrd,bkgd->bgrqk', qi, kc).astype(jnp.float32) * scale
        pr = jax.nn.softmax(jnp.concatenate([sink_l, s_c, s_w], -1), -1).astype(v.dtype)
        return (jnp.einsum('bgrqk,bkgd->bqgrd', pr[..., 1:1 + Lc], vc)
                + jnp.einsum('bgrqk,bkgd->bqgrd', pr[..., 1 + Lc:], vw))

    o = lax.map(block, (jnp.arange(nb), qb))
    return jnp.swapaxes(o, 0, 1).reshape(B, S, H * Dh)


def _swa_context(q, kc, vc, sink):
    B, Lc, H, Dh = q.shape
    G = kc.shape[2]
    R = H // G
    qg = q.reshape(B, Lc, G, R, Dh)
    s = jnp.einsum('bqgrd,bkgd->bgrqk', qg, kc).astype(jnp.float32) * (Dh ** -0.5)
    s_sink = jnp.broadcast_to(sink.astype(jnp.float32).reshape(G, R)[None, :, :, None, None], (B, G, R, Lc, 1))
    pr = jax.nn.softmax(jnp.concatenate([s_sink, s], -1), -1).astype(vc.dtype)
    return jnp.einsum('bgrqk,bkgd->bqgrd', pr[..., 1:], vc).reshape(B, Lc, H * Dh)


def _merge(y_hy, y_mla, y_swa, g, p):
    g_hy, g_mla, g_swa = jnp.split(g, N_BRANCH, axis=-1)
    m = (jax.nn.sigmoid(g_hy) * (y_hy @ p['w_proj_hy'])
         + jax.nn.sigmoid(g_mla) * (y_mla @ p['w_proj_mla'])
         + jax.nn.sigmoid(g_swa) * (y_swa @ p['w_proj_swa']))
    return m @ p['w_out']


def _modulate(x, mod):
    shift, scale, gate = jnp.split(mod, 3, axis=-1)
    return _layer_norm(x) * (1.0 + scale) + shift, gate


def _trunk_layer(xl, xc, c, c_ctx, rows, cols, p, ctx_out):
    B, Lc = xc.shape[0], xc.shape[1]
    mod_l = (jax.nn.silu(c) @ p['w_ada'] + p['b_ada'])[:, None, :]
    mod_c = jax.nn.silu(c_ctx) @ p['w_ada'] + p['b_ada']
    ul, gate_l = _modulate(xl, mod_l)
    uc, gate_c = _modulate(xc, mod_c)
    (hy_l, hyg_l, cq_l, ckv_l, kr_l, mlag_l, sq_l, sk_l, sv_l, swag_l, mg_l) = jnp.split(ul @ p['w_in'], SPLITS, axis=-1)
    (hy_c, hyg_c, cq_c, ckv_c, kr_c, mlag_c, sq_c, sk_c, sv_c, swag_c, mg_c) = jnp.split(uc @ p['w_in'], SPLITS, axis=-1)

    k_mc, v_mc = _mla_kv(ckv_c, kr_c, p, None, None)
    k_sc = _split_heads(sk_c, SWA_KV_HEADS)
    v_sc = _split_heads(sv_c, SWA_KV_HEADS)

    y_hy = _hyena_mixer(hy_l, p) * jax.nn.silu(hyg_l)
    q_ml = _mla_q(cq_l, p, rows, cols)
    k_ml, v_ml = _mla_kv(ckv_l, kr_l, p, rows, cols)
    y_mla = _block_dense_attention(q_ml, jnp.concatenate([k_mc, k_ml], 1), jnp.concatenate([v_mc, v_ml], 1),
                                   MLA_QK ** -0.5) * jax.nn.silu(mlag_l)
    q_sl = _rope_2d(_split_heads(sq_l, SWA_HEADS), rows, cols)
    k_sl = _rope_2d(_split_heads(sk_l, SWA_KV_HEADS), rows, cols)
    v_sl = _split_heads(sv_l, SWA_KV_HEADS)
    y_swa = _swa_latent(q_sl, k_sl, v_sl, k_sc, v_sc, p['swa_sink']) * jax.nn.silu(swag_l)
    xl_new = _layer_norm(ALPHA * xl + gate_l * _merge(y_hy, y_mla, y_swa, mg_l, p), p['ln_g'], p['ln_b'])
    if not ctx_out:
        return xl_new, xc

    yc_hy = _hyena_mixer(hy_c, p) * jax.nn.silu(hyg_c)
    q_mc = _mla_q(cq_c, p, None, None)
    yc_mla = _dense_attention(q_mc, k_mc, v_mc, MLA_QK ** -0.5).reshape(B, Lc, MLA_HEADS * MLA_V) * jax.nn.silu(mlag_c)
    yc_swa = _swa_context(_split_heads(sq_c, SWA_HEADS), k_sc, v_sc, p['swa_sink']) * jax.nn.silu(swag_c)
    xc_new = _layer_norm(ALPHA * xc + gate_c * _merge(yc_hy, yc_mla, yc_swa, mg_c, p), p['ln_g'], p['ln_b'])
    return xl_new, xc_new


def setup_inputs(seed: int = 0) -> dict:
    key = jax.random.key(seed)
    ks = jax.random.split(key, 32)

    def nrm(k, shape, s):
        return jax.random.normal(k, shape, jnp.float32) * s

    L = DEPTH
    return {
        'x': nrm(ks[0], (BATCH, SEQ, D_MODEL), 1.0),
        'c': nrm(ks[1], (BATCH, D_MODEL), 1.0),
        'ctx': nrm(ks[2], (BATCH, CTX_LEN, D_MODEL), 1.0),
        'c_ctx': nrm(ks[3], (D_MODEL,), 1.0),
        'w_ada': nrm(ks[4], (L, D_MODEL, 3 * D_MODEL), 0.5 * D_MODEL ** -0.5),
        'b_ada': nrm(ks[5], (L, 3 * D_MODEL), 0.01),
        'w_in': nrm(ks[6], (L, D_MODEL, P_TOTAL), D_MODEL ** -0.5),
        'hy_conv_w': nrm(ks[7], (L, SHORT_K, 3 * HY_W), SHORT_K ** -0.5),
        'hy_conv_b': nrm(ks[8], (L, 3 * HY_W), 0.01),
        'filt_w1': nrm(ks[9], (L, FILT_EMB, FILT_W), FILT_EMB ** -0.5),
        'filt_b1': nrm(ks[10], (L, FILT_W), 0.1),
        'filt_w2': nrm(ks[11], (L, FILT_W, FILT_W), FILT_W ** -0.5),
        'filt_b2': nrm(ks[12], (L, FILT_W), 0.1),
        'filt_w3': nrm(ks[13], (L, FILT_W, FILT_W), FILT_W ** -0.5),
        'filt_b3': nrm(ks[14], (L, FILT_W), 0.1),
        'filt_freq': 1.0 + nrm(ks[15], (L, FILT_W), 0.01),
        'filt_w_out': nrm(ks[16], (L, FILT_W, 2 * HY_W), 0.05 * FILT_W ** -0.5),
        'hy_skip': nrm(ks[17], (L, HY_W), 0.5),
        'mla_q_norm': 1.0 + nrm(ks[18], (L, MLA_Q_RANK), 0.01),
        'mla_w_uq': nrm(ks[19], (L, MLA_Q_RANK, MLA_HEADS * MLA_QK), MLA_Q_RANK ** -0.5),
        'mla_kv_norm': 1.0 + nrm(ks[20], (L, MLA_KV_RANK), 0.01),
        'mla_w_ukv': nrm(ks[21], (L, MLA_KV_RANK, MLA_HEADS * (MLA_NOPE + MLA_V)), MLA_KV_RANK ** -0.5),
        'swa_sink': nrm(ks[22], (L, SWA_HEADS), 0.5),
        'w_proj_hy': nrm(ks[23], (L, HY_W, D_MODEL), BETA * HY_W ** -0.5),
        'w_proj_mla': nrm(ks[24], (L, MLA_HEADS * MLA_V, D_MODEL), BETA * (MLA_HEADS * MLA_V) ** -0.5),
        'w_proj_swa': nrm(ks[25], (L, SWA_HEADS * SWA_HD, D_MODEL), BETA * (SWA_HEADS * SWA_HD) ** -0.5),
        'w_out': nrm(ks[26], (L, D_MODEL, D_MODEL), BETA * D_MODEL ** -0.5),
        'ln_g': 1.0 + nrm(ks[27], (L, D_MODEL), 0.01),
        'ln_b': nrm(ks[28], (L, D_MODEL), 0.01),
    }


def reference(x, c, ctx, c_ctx, w_ada, b_ada, w_in, hy_conv_w, hy_conv_b, filt_w1, filt_b1, filt_w2, filt_b2,
              filt_w3, filt_b3, filt_freq, filt_w_out, hy_skip, mla_q_norm, mla_w_uq, mla_kv_norm, mla_w_ukv,
              swa_sink, w_proj_hy, w_proj_mla, w_proj_swa, w_out, ln_g, ln_b):
    S = x.shape[1]
    ROWS = S // GRID_W
    rows = jnp.repeat(jnp.arange(ROWS, dtype=jnp.int32), GRID_W)
    cols = jnp.tile(jnp.arange(GRID_W, dtype=jnp.int32), ROWS)
    xl, xc = x, ctx
    for l in range(DEPTH):
        p = {
            'w_ada': w_ada[l], 'b_ada': b_ada[l], 'w_in': w_in[l],
            'hy_conv_w': hy_conv_w[l], 'hy_conv_b': hy_conv_b[l],
            'filt_w1': filt_w1[l], 'filt_b1': filt_b1[l], 'filt_w2': filt_w2[l], 'filt_b2': filt_b2[l],
            'filt_w3': filt_w3[l], 'filt_b3': filt_b3[l], 'filt_freq': filt_freq[l], 'filt_w_out': filt_w_out[l],
            'hy_skip': hy_skip[l],
            'mla_q_norm': mla_q_norm[l], 'mla_w_uq': mla_w_uq[l],
            'mla_kv_norm': mla_kv_norm[l], 'mla_w_ukv': mla_w_ukv[l],
            'swa_sink': swa_sink[l],
            'w_proj_hy': w_proj_hy[l], 'w_proj_mla': w_proj_mla[l], 'w_proj_swa': w_proj_swa[l], 'w_out': w_out[l],
            'ln_g': ln_g[l], 'ln_b': ln_b[l],
        }
        xl, xc = _trunk_layer(xl, xc, c, c_ctx, rows, cols, p, l < DEPTH - 1)
    return xl
```

```python
import functools
import math

import jax
import jax.numpy as jnp
from jax import lax
from jax.experimental import pallas as pl
from jax.experimental.pallas import tpu as pltpu

F32 = jnp.float32
BF16 = jnp.bfloat16
HIGHEST = lax.Precision.HIGHEST

GRID_W = 64
HY_W = 512
SHORT_K = 3
FILT_BANDS = 16
FILT_W = 64
DECAY_TARGET = 1e-2
FAST_DECAY_PCT = 0.3
SLOW_DECAY_PCT = 1.5
MLA_HEADS = 8
MLA_NOPE = 64
MLA_ROPE = 32
MLA_V = 64
MLA_QK = MLA_NOPE + MLA_ROPE
MLA_Q_RANK = 256
MLA_KV_RANK = 128
SWA_HEADS = 8
SWA_KV_HEADS = 2
SWA_HD = 64
WINDOW = 128
N_BRANCH = 3
ROPE_BASE = 10000.0
EPS = 1e-6
NEG = -1e30

LANES = 128
DFT_N2 = 128
VMEM_LIMIT = 56 * 1024 * 1024


def _silu(x):
    return x * jax.nn.sigmoid(x)


def _dot(a, b):
    return jnp.dot(a, b, preferred_element_type=F32)


def _dot_nt(a, b):
    return lax.dot_general(a, b, (((1,), (1,)), ((), ())), preferred_element_type=F32)


def _params(sem):
    return pltpu.CompilerParams(dimension_semantics=sem, vmem_limit_bytes=VMEM_LIMIT)


def _const_spec(shape):
    nd = len(shape)
    return pl.BlockSpec(shape, lambda *_: (0,) * nd)


def _ada_kernel(c_ref, w_ref, b_ref, o_ref):
    a = _silu(c_ref[...])
    o_ref[...] = jnp.dot(a, w_ref[...], precision=HIGHEST, preferred_element_type=F32) + b_ref[...]


def _ada_mod(cvec, w_ada, b_ada):
    R, D = cvec.shape
    N = w_ada.shape[1]
    tn = 768
    return pl.pallas_call(
        _ada_kernel, name="ada_mod", grid=(N // tn,),
        in_specs=[_const_spec((R, D)), pl.BlockSpec((D, tn), lambda j: (0, j)),
                  pl.BlockSpec((1, tn), lambda j: (0, j))],
        out_specs=pl.BlockSpec((R, tn), lambda j: (0, j)),
        out_shape=jax.ShapeDtypeStruct((R, N), F32),
        compiler_params=_params(("arbitrary",)),
    )(cvec, w_ada, b_ada.reshape(1, N))


def _in_proj_kernel(x_ref, mod_ref, w_ref, *o_refs, widths, D):
    x = x_ref[0]
    mu = jnp.mean(x, -1, keepdims=True)
    xc = x - mu
    var = jnp.mean(xc * xc, -1, keepdims=True)
    ln = xc * lax.rsqrt(var + EPS)
    shift = mod_ref[0, :, 0:D]
    scale = mod_ref[0, :, D:2 * D]
    u = (ln * (1.0 + scale) + shift).astype(BF16)
    off = 0
    for o_ref, wd in zip(o_refs, widths):
        o_ref[0] = _dot(u, w_ref[:, off:off + wd]).astype(o_ref.dtype)
        off += wd


def _in_proj(x, mod, w, widths, tm):
    Bx, S, D = x.shape
    P = w.shape[1]
    kern = functools.partial(_in_proj_kernel, widths=tuple(widths), D=D)
    return pl.pallas_call(
        kern, name="in_proj", grid=(Bx, S // tm),
        in_specs=[pl.BlockSpec((1, tm, D), lambda b, i: (b, i, 0)),
                  pl.BlockSpec((1, 1, 3 * D), lambda b, i: (b, 0, 0)),
                  pl.BlockSpec((D, P), lambda b, i: (0, 0), pipeline_mode=pl.Buffered(1))],
        out_specs=[pl.BlockSpec((1, tm, wd), lambda b, i: (b, i, 0)) for wd in widths],
        out_shape=[jax.ShapeDtypeStruct((Bx, S, wd), F32) for wd in widths],
        compiler_params=_params(("parallel", "arbitrary")),
    )(x, mod, w)


def _filter_kernel(z_ref, t_ref, w1, b1, w2, b2, w3, b3, fr, wo, dl_ref, o_ref, *, tl, C):
    f = fr[...]
    h = jnp.sin(f * (jnp.dot(z_ref[...], w1[...], precision=HIGHEST, preferred_element_type=F32) + b1[...]))
    h = jnp.sin(f * (jnp.dot(h, w2[...], precision=HIGHEST, preferred_element_type=F32) + b2[...]))
    h = jnp.sin(f * (jnp.dot(h, w3[...], precision=HIGHEST, preferred_element_type=F32) + b3[...]))
    h = jnp.dot(h, wo[...], precision=HIGHEST, preferred_element_type=F32)
    decay = jnp.exp(-t_ref[...] * jnp.abs(dl_ref[...]))
    row = pl.program_id(0) * tl + lax.broadcasted_iota(jnp.int32, (tl, 1), 0)
    o_ref[:, 0:C] = h[:, 0:C] * decay
    o_ref[:, C:2 * C] = jnp.where(row == 0, 0.0, h[:, C:2 * C] * decay)


def _hy_filters(L, p):
    C = HY_W
    FP = LANES
    t = jnp.linspace(0.0, 1.0, L, dtype=F32)[:, None]
    w = 2.0 * math.pi * jnp.arange(L, dtype=F32) / L
    f = jnp.linspace(1e-4, FILT_BANDS - 1, FILT_BANDS, dtype=F32)
    ang = w[:, None] * f[None, :]
    z = jnp.concatenate([t, jnp.cos(ang), -jnp.sin(ang)], -1)
    z = jnp.pad(z, ((0, 0), (0, FP - z.shape[1])))
    pad_w = lambda a: jnp.pad(a, ((0, FP - a.shape[0]), (0, FP - a.shape[1])))
    pad_v = lambda a: jnp.pad(a, (0, FP - a.shape[0])).reshape(1, FP)
    w1, w2, w3 = pad_w(p['filt_w1']), pad_w(p['filt_w2']), pad_w(p['filt_w3'])
    wo = jnp.pad(p['filt_w_out'], ((0, FP - FILT_W), (0, 0)))
    min_decay = math.log(DECAY_TARGET) / SLOW_DECAY_PCT
    max_decay = math.log(DECAY_TARGET) / FAST_DECAY_PCT
    deltas = jnp.linspace(min_decay, max_decay, C, dtype=F32).reshape(1, C)
    tl = min(L, 512)
    kern = functools.partial(_filter_kernel, tl=tl, C=C)
    cs = _const_spec
    return pl.pallas_call(
        kern, name="hy_filter", grid=(L // tl,),
        in_specs=[pl.BlockSpec((tl, FP), lambda i: (i, 0)), pl.BlockSpec((tl, 1), lambda i: (i, 0)),
                  cs((FP, FP)), cs((1, FP)), cs((FP, FP)), cs((1, FP)), cs((FP, FP)), cs((1, FP)),
                  cs((1, FP)), cs((FP, 2 * C)), cs((1, C))],
        out_specs=pl.BlockSpec((tl, 2 * C), lambda i: (i, 0)),
        out_shape=jax.ShapeDtypeStruct((L, 2 * C), F32),
        compiler_params=_params(("arbitrary",)),
    )(z, t, w1, pad_v(p['filt_b1']), w2, pad_v(p['filt_b2']), w3, pad_v(p['filt_b3']),
      pad_v(p['filt_freq']), wo, deltas)


def _hy_pre_kernel(x_ref, xp_ref, xn_ref, g_ref, w_ref, b_ref, u_ref, e_ref, *, ts, C):
    i = pl.program_id(1)
    nt = pl.num_programs(1)
    x = x_ref[0]
    prev_row = xp_ref[0, 7:8, :] * jnp.where(i > 0, 1.0, 0.0)
    next_row = xn_ref[0, 0:1, :] * jnp.where(i < nt - 1, 1.0, 0.0)
    rid = lax.broadcasted_iota(jnp.int32, (ts, 1), 0)
    xm = jnp.where(rid == 0, prev_row, pltpu.roll(x, 1, 0))
    xq = jnp.where(rid == ts - 1, next_row, pltpu.roll(x, ts - 1, 0))
    z = b_ref[...] + xm * w_ref[0:1, :] + x * w_ref[1:2, :] + xq * w_ref[2:3, :]
    u_ref[0] = z[:, 2 * C:3 * C] * z[:, C:2 * C]
    e_ref[0] = z[:, 0:C] * _silu(g_ref[0])


def _hy_pre(hy, hyg, conv_w, conv_b, ts):
    Bx, S, C3 = hy.shape
    C = C3 // 3
    nb8 = S // 8
    r = ts // 8
    w8 = jnp.pad(conv_w, ((0, 8 - SHORT_K), (0, 0)))
    kern = functools.partial(_hy_pre_kernel, ts=ts, C=C)
    return pl.pallas_call(
        kern, name="hy_pre", grid=(Bx, S // ts),
        in_specs=[pl.BlockSpec((1, ts, C3), lambda b, i: (b, i, 0)),
                  pl.BlockSpec((1, 8, C3), lambda b, i: (b, jnp.maximum(i * r - 1, 0), 0)),
                  pl.BlockSpec((1, 8, C3), lambda b, i: (b, jnp.minimum((i + 1) * r, nb8 - 1), 0)),
                  pl.BlockSpec((1, ts, C), lambda b, i: (b, i, 0)),
                  _const_spec((8, C3)), _const_spec((1, C3))],
        out_specs=[pl.BlockSpec((1, ts, C), lambda b, i: (b, i, 0))] * 2,
        out_shape=[jax.ShapeDtypeStruct((Bx, S, C), F32)] * 2,
        compiler_params=_params(("parallel", "arbitrary")),
    )(hy, hy, hy, hyg, w8, conv_b.reshape(1, C3))


def _dft_tables(L):
    n = 2 * L
    N2 = DFT_N2
    N1 = n // N2
    H1 = N1 // 2
    k1 = jnp.arange(N1, dtype=jnp.int32)
    t1 = jnp.arange(H1, dtype=jnp.int32)
    ang = (2.0 * math.pi / N1) * ((k1[:, None] * t1[None, :]) % N1).astype(F32)
    fa = jnp.concatenate([jnp.cos(ang), -jnp.sin(ang)], 0)
    fai = fa.T * (1.0 / n)
    k2 = jnp.arange(N2, dtype=jnp.int32)
    t2 = jnp.arange(N2, dtype=jnp.int32)
    m = (t2[None, None, :] * (k2[None, :, None] * N1 + k1[:, None, None])) % n
    th = (2.0 * math.pi / n) * m.astype(F32)
    cr, ci = jnp.cos(th), -jnp.sin(th)
    mf = jnp.concatenate([jnp.concatenate([cr, -ci], 2), jnp.concatenate([ci, cr], 2)], 1)
    mi = jnp.swapaxes(mf, 1, 2)
    return fa.astype(BF16), fai.astype(BF16), mf.astype(BF16), mi.astype(BF16)


def _dft_a_fwd_kernel(u_ref, f_ref, o_ref):
    o_ref[0] = _dot(f_ref[...], u_ref[0].astype(BF16)).astype(o_ref.dtype)


def _dft_a_fwd(u2, fa, tc):
    Bx, H1, W = u2.shape
    R = fa.shape[0]
    return pl.pallas_call(
        _dft_a_fwd_kernel, name="dft_a_fwd", grid=(Bx, W // tc),
        in_specs=[pl.BlockSpec((1, H1, tc), lambda b, j: (b, 0, j)), _const_spec((R, H1))],
        out_specs=pl.BlockSpec((1, R, tc), lambda b, j: (b, 0, j)),
        out_shape=jax.ShapeDtypeStruct((Bx, R, W), BF16),
        compiler_params=_params(("parallel", "arbitrary")),
    )(u2, fa)


def _dft_b_filter_kernel(a_ref, m_ref, k_ref, *, kb, C, N2):
    for j in range(kb):
        a = a_ref[0, :, j].reshape(2 * N2, 2 * C)
        h = _dot(m_ref[j], a)
        k_ref[j, 0:N2, :] = h[0:N2, 0:C] + h[0:N2, C:2 * C]
        k_ref[j, N2:2 * N2, :] = h[N2:2 * N2, 0:C] - h[N2:2 * N2, C:2 * C]


def _dft_b_filter(ah, mf, kb):
    _, _, N1, N2, C2 = ah.shape
    C = C2 // 2
    kern = functools.partial(_dft_b_filter_kernel, kb=kb, C=C, N2=N2)
    return pl.pallas_call(
        kern, name="dft_b_filter", grid=(N1 // kb,),
        in_specs=[pl.BlockSpec((1, 2, kb, N2, C2), lambda i: (0, 0, i, 0, 0)),
                  pl.BlockSpec((kb, 2 * N2, 2 * N2), lambda i: (i, 0, 0))],
        out_specs=pl.BlockSpec((kb, 2 * N2, C), lambda i: (i, 0, 0)),
        out_shape=jax.ShapeDtypeStruct((N1, 2 * N2, C), F32),
        compiler_params=_params(("arbitrary",)),
    )(ah, mf)


def _dft_mid_kernel(a_ref, mf_ref, k_ref, mi_ref, g_ref, *, kb, nb, C, N2):
    for j in range(kb):
        kre = k_ref[j, 0:N2, :]
        kim = k_ref[j, N2:2 * N2, :]
        for b in range(nb):
            a = a_ref[b, :, j].reshape(2 * N2, C)
            y = _dot(mf_ref[j], a)
            yre, yim = y[0:N2], y[N2:2 * N2]
            z = jnp.concatenate([yre * kre - yim * kim, yre * kim + yim * kre], 0).astype(BF16)
            g = _dot(mi_ref[j], z)
            g_ref[b, :, j] = g.reshape(2, N2, C).astype(g_ref.dtype)


def _dft_mid(a5, mf, kf, mi, kb):
    Bx, _, N1, N2, C = a5.shape
    kern = functools.partial(_dft_mid_kernel, kb=kb, nb=Bx, C=C, N2=N2)
    return pl.pallas_call(
        kern, name="dft_mid", grid=(N1 // kb,),
        in_specs=[pl.BlockSpec((Bx, 2, kb, N2, C), lambda i: (0, 0, i, 0, 0)),
                  pl.BlockSpec((kb, 2 * N2, 2 * N2), lambda i: (i, 0, 0)),
                  pl.BlockSpec((kb, 2 * N2, C), lambda i: (i, 0, 0)),
                  pl.BlockSpec((kb, 2 * N2, 2 * N2), lambda i: (i, 0, 0))],
        out_specs=pl.BlockSpec((Bx, 2, kb, N2, C), lambda i: (0, 0, i, 0, 0)),
        out_shape=jax.ShapeDtypeStruct(a5.shape, BF16),
        compiler_params=_params(("arbitrary",)),
    )(a5, mf, kf, mi)


def _dft_a_inv_kernel(g_ref, f_ref, u_ref, e_ref, s_ref, o_ref):
    y = _dot(f_ref[...], g_ref[0])
    o_ref[0] = (y + u_ref[0] * s_ref[...]) * e_ref[0]


def _dft_a_inv(g2, fai, u2, e2, skip_t, tc):
    Bx, R, W = g2.shape
    H1 = fai.shape[0]
    return pl.pallas_call(
        _dft_a_inv_kernel, name="dft_a_inv", grid=(Bx, W // tc),
        in_specs=[pl.BlockSpec((1, R, tc), lambda b, j: (b, 0, j)), _const_spec((H1, R)),
                  pl.BlockSpec((1, H1, tc), lambda b, j: (b, 0, j)),
                  pl.BlockSpec((1, H1, tc), lambda b, j: (b, 0, j)),
                  _const_spec((1, tc))],
        out_specs=pl.BlockSpec((1, H1, tc), lambda b, j: (b, 0, j)),
        out_shape=jax.ShapeDtypeStruct((Bx, H1, W), F32),
        compiler_params=_params(("parallel", "arbitrary")),
    )(g2, fai, u2, e2, skip_t)


def _hy_long_conv(u, e, h2, skip, tabs):
    B, L, C = u.shape
    fa, fai, mf, mi = tabs
    N2 = DFT_N2
    N1 = 2 * L // N2
    H1 = N1 // 2
    tc = 4096
    kb = 4
    ah = _dft_a_fwd(h2.reshape(1, H1, N2 * 2 * C), fa, tc)
    kf = _dft_b_filter(ah.reshape(1, 2, N1, N2, 2 * C), mf, kb)
    u2 = u.reshape(B, H1, N2 * C)
    a = _dft_a_fwd(u2, fa, tc)
    g = _dft_mid(a.reshape(B, 2, N1, N2, C), mf, kf, mi, kb)
    skip_t = jnp.tile(skip.reshape(1, C), (1, tc // C))
    y = _dft_a_inv(g.reshape(B, 2 * N1, N2 * C), fai, u2, e.reshape(B, H1, N2 * C), skip_t, tc)
    return y.reshape(B, L, C)


def _ctx_dft_tables(Lc):
    n = 2 * Lc
    k = jnp.arange(n, dtype=jnp.int32)
    t = jnp.arange(Lc, dtype=jnp.int32)
    ang = (2.0 * math.pi / n) * ((k[:, None] * t[None, :]) % n).astype(F32)
    fc = jnp.concatenate([jnp.cos(ang), -jnp.sin(ang)], 0)
    fi = fc.T * (1.0 / n)
    return fc.astype(BF16), fi.astype(BF16)


def _hy_ctx_conv_kernel(u_ref, e_ref, h_ref, fc_ref, fi_ref, s_ref, o_ref, *, n, C):
    u = u_ref[0]
    fc = fc_ref[...]
    uf = _dot(fc, u.astype(BF16))
    hf = _dot(fc, h_ref[...].astype(BF16))
    kre = hf[0:n, 0:C] + hf[0:n, C:2 * C]
    kim = hf[n:2 * n, 0:C] - hf[n:2 * n, C:2 * C]
    ure, uim = uf[0:n], uf[n:2 * n]
    z = jnp.concatenate([ure * kre - uim * kim, ure * kim + uim * kre], 0).astype(BF16)
    y = _dot(fi_ref[...], z)
    o_ref[0] = (y + u * s_ref[...]) * e_ref[0]


def _hy_ctx_conv(u, e, h2, skip, tabs):
    B, Lc, C = u.shape
    fc, fi = tabs
    n = 2 * Lc
    kern = functools.partial(_hy_ctx_conv_kernel, n=n, C=C)
    blk = pl.BlockSpec((1, Lc, C), lambda b: (b, 0, 0))
    return pl.pallas_call(
        kern, name="hy_ctx_conv", grid=(B,),
        in_specs=[blk, blk, _const_spec((Lc, 2 * C)), _const_spec((2 * n, Lc)), _const_spec((Lc, 2 * n)),
                  _const_spec((1, C))],
        out_specs=blk,
        out_shape=jax.ShapeDtypeStruct((B, Lc, C), F32),
        compiler_params=_params(("arbitrary",)),
    )(u, e, h2, fc, fi, skip.reshape(1, C))


def _rope_tables(S, lane_dim, lane_on, head_rot, identity=False):
    if identity:
        return (jnp.ones((S, LANES), F32), jnp.zeros((S, LANES), F32), jnp.zeros((S, LANES), F32))
    seg_w = head_rot // 2
    half = seg_w // 2
    seg = lane_dim // seg_w
    w = lane_dim % seg_w
    first = w < half
    inv = ROPE_BASE ** (-(w % half).astype(F32) / half)
    nrow = S // GRID_W
    ang_r = jnp.arange(nrow, dtype=jnp.int32).astype(F32)[:, None] * inv[None, :]
    ang_c = jnp.arange(GRID_W, dtype=jnp.int32).astype(F32)[:, None] * inv[None, :]

    def expand(fn):
        tab = jnp.where((seg == 0)[None, None, :], fn(ang_r)[:, None, :], fn(ang_c)[None, :, :])
        return tab.reshape(S, LANES)

    cos, sin = expand(jnp.cos), expand(jnp.sin)
    on = lane_on[None, :]
    c = jnp.where(on, cos, 1.0)
    sm = jnp.where(on & first[None, :], -sin, 0.0)
    sp = jnp.where(on & (~first)[None, :], sin, 0.0)
    return c, sm, sp


def _rope(x, c, sm, sp, hh):
    return x * c + pltpu.roll(x, LANES - hh, 1) * sm + pltpu.roll(x, hh, 1) * sp


def _mla_q_kernel(cq_ref, g_ref, w_ref, c_ref, sm_ref, sp_ref, o_ref, *, nh, scale, hh):
    x = cq_ref[0]
    xn = x * lax.rsqrt(jnp.mean(x * x, -1, keepdims=True) + EPS) * g_ref[...]
    q = _dot(xn.astype(BF16), w_ref[...])
    c, sm, sp = c_ref[...], sm_ref[...], sp_ref[...]
    for h in range(nh):
        qh = _rope(q[:, h * LANES:(h + 1) * LANES], c, sm, sp, hh)
        o_ref[0, :, h * LANES:(h + 1) * LANES] = (qh * scale).astype(o_ref.dtype)


def _mla_q(cq, qnorm, w_uq_p, tabs, tm):
    Bx, S, R = cq.shape
    N = w_uq_p.shape[1]
    kern = functools.partial(_mla_q_kernel, nh=MLA_HEADS, scale=MLA_QK ** -0.5, hh=MLA_ROPE // 4)
    tab = pl.BlockSpec((tm, LANES), lambda b, i: (i, 0))
    return pl.pallas_call(
        kern, name="mla_q", grid=(Bx, S // tm),
        in_specs=[pl.BlockSpec((1, tm, R), lambda b, i: (b, i, 0)), _const_spec((1, R)), _const_spec((R, N)),
                  tab, tab, tab],
        out_specs=pl.BlockSpec((1, tm, N), lambda b, i: (b, i, 0)),
        out_shape=jax.ShapeDtypeStruct((Bx, S, N), BF16),
        compiler_params=_params(("parallel", "arbitrary")),
    )(cq, qnorm.reshape(1, R), w_uq_p, *tabs)


def _mla_kv_kernel(ckv_ref, kr_ref, g_ref, wk_ref, wv_ref, c_ref, sm_ref, sp_ref, k_ref, v_ref, *, nh, hh):
    x = ckv_ref[0]
    xn = (x * lax.rsqrt(jnp.mean(x * x, -1, keepdims=True) + EPS) * g_ref[...]).astype(BF16)
    kn = _dot(xn, wk_ref[...])
    krr = _rope(kr_ref[0], c_ref[...], sm_ref[...], sp_ref[...], hh)
    for h in range(nh):
        k_ref[0, :, h * LANES:(h + 1) * LANES] = (kn[:, h * LANES:(h + 1) * LANES] + krr).astype(k_ref.dtype)
    v_ref[0] = _dot(xn, wv_ref[...]).astype(v_ref.dtype)


def _mla_kv(ckv, kr, kvnorm, wk_p, wv, tabs, tm):
    Bx, S, R = ckv.shape
    NK, NV = wk_p.shape[1], wv.shape[1]
    kern = functools.partial(_mla_kv_kernel, nh=MLA_HEADS, hh=MLA_ROPE // 4)
    tab = pl.BlockSpec((tm, LANES), lambda b, i: (i, 0))
    return pl.pallas_call(
        kern, name="mla_kv", grid=(Bx, S // tm),
        in_specs=[pl.BlockSpec((1, tm, R), lambda b, i: (b, i, 0)),
                  pl.BlockSpec((1, tm, LANES), lambda b, i: (b, i, 0)),
                  _const_spec((1, R)), _const_spec((R, NK)), _const_spec((R, NV)), tab, tab, tab],
        out_specs=[pl.BlockSpec((1, tm, NK), lambda b, i: (b, i, 0)),
                   pl.BlockSpec((1, tm, NV), lambda b, i: (b, i, 0))],
        out_shape=[jax.ShapeDtypeStruct((Bx, S, NK), BF16), jax.ShapeDtypeStruct((Bx, S, NV), BF16)],
        compiler_params=_params(("parallel", "arbitrary")),
    )(ckv, kr, kvnorm.reshape(1, R), wk_p, wv, *tabs)


def _mla_attn_kernel(*refs, tq, tk, n_chunks):
    if n_chunks:
        q_ref, kc_ref, vc_ref, k_ref, v_ref, gate_ref, o_ref = refs
    else:
        q_ref, kc_ref, vc_ref, gate_ref, o_ref = refs
    outs = []
    for j in range(2):
        q = q_ref[0, :, j * LANES:(j + 1) * LANES]
        s = _dot_nt(q, kc_ref[0, :, j * LANES:(j + 1) * LANES])
        m = jnp.max(s, -1, keepdims=True)
        p = jnp.exp(s - m)
        l = jnp.sum(p, -1, keepdims=True)
        acc = _dot(p.astype(BF16), vc_ref[0])
        if n_chunks:
            def body(c, carry, q=q, j=j):
                m, l, acc = carry
                st = pl.multiple_of(c * tk, tk)
                s = _dot_nt(q, k_ref[0, pl.ds(st, tk), j * LANES:(j + 1) * LANES])
                m_new = jnp.maximum(m, jnp.max(s, -1, keepdims=True))
                alpha = jnp.exp(m - m_new)
                p = jnp.exp(s - m_new)
                l = alpha * l + jnp.sum(p, -1, keepdims=True)
                acc = alpha * acc + _dot(p.astype(BF16), v_ref[0, pl.ds(st, tk), :])
                return m_new, l, acc
            m, l, acc = lax.fori_loop(0, n_chunks, body, (m, l, acc))
        outs.append(acc / l)
    lane = lax.broadcasted_iota(jnp.int32, (tq, LANES), 1)
    y = jnp.where(lane < MLA_V, outs[0], outs[1])
    o_ref[0] = y * _silu(gate_ref[0])


def _mla_attn(q, kc, vc, k, v, gate, tq, tk):
    B, S, _ = q.shape
    Lc = kc.shape[1]
    npair = MLA_HEADS // 2
    n_chunks = 0 if k is None else k.shape[1] // tk
    kern = functools.partial(_mla_attn_kernel, tq=tq, tk=tk, n_chunks=n_chunks)
    in_specs = [pl.BlockSpec((1, tq, 2 * LANES), lambda b, p, i: (b, i, p)),
                pl.BlockSpec((1, Lc, 2 * LANES), lambda b, p, i: (b, 0, p)),
                pl.BlockSpec((1, Lc, LANES), lambda b, p, i: (b, 0, p))]
    args = [q, kc, vc]
    if n_chunks:
        Sk = k.shape[1]
        in_specs += [pl.BlockSpec((1, Sk, 2 * LANES), lambda b, p, i: (b, 0, p)),
                     pl.BlockSpec((1, Sk, LANES), lambda b, p, i: (b, 0, p))]
        args += [k, v]
    in_specs.append(pl.BlockSpec((1, tq, LANES), lambda b, p, i: (b, i, p)))
    args.append(gate)
    return pl.pallas_call(
        kern, name="mla_attn", grid=(B, npair, S // tq),
        in_specs=in_specs,
        out_specs=pl.BlockSpec((1, tq, LANES), lambda b, p, i: (b, i, p)),
        out_shape=jax.ShapeDtypeStruct((B, S, npair * LANES), F32),
        compiler_params=_params(("parallel", "arbitrary", "arbitrary")),
    )(*args)


def _swa_prep_kernel(q_ref, k_ref, v_ref, c_ref, sm_ref, sp_ref, qo_ref, ko_ref, vo_ref, *, scale, hh, nq):
    c, sm, sp = c_ref[...], sm_ref[...], sp_ref[...]
    for r in range(nq):
        qr = _rope(q_ref[0, :, r * LANES:(r + 1) * LANES], c, sm, sp, hh)
        qo_ref[0, :, r * LANES:(r + 1) * LANES] = (qr * scale).astype(qo_ref.dtype)
    ko_ref[0] = _rope(k_ref[0], c, sm, sp, hh).astype(ko_ref.dtype)
    vo_ref[0] = v_ref[0].astype(vo_ref.dtype)


def _swa_prep(sq, sk, sv, tabs, tm):
    Bx, S, NQ = sq.shape
    kern = functools.partial(_swa_prep_kernel, scale=SWA_HD ** -0.5, hh=SWA_HD // 4, nq=NQ // LANES)
    tab = pl.BlockSpec((tm, LANES), lambda b, i: (i, 0))
    qs = pl.BlockSpec((1, tm, NQ), lambda b, i: (b, i, 0))
    ks = pl.BlockSpec((1, tm, LANES), lambda b, i: (b, i, 0))
    return pl.pallas_call(
        kern, name="swa_prep", grid=(Bx, S // tm),
        in_specs=[qs, ks, ks, tab, tab, tab],
        out_specs=[qs, ks, ks],
        out_shape=[jax.ShapeDtypeStruct((Bx, S, NQ), BF16), jax.ShapeDtypeStruct((Bx, S, LANES), BF16),
                   jax.ShapeDtypeStruct((Bx, S, LANES), BF16)],
        compiler_params=_params(("parallel", "arbitrary")),
    )(sq, sk, sv, *tabs)


def _sink_column(sink_ref, g, R, W):
    rid = lax.broadcasted_iota(jnp.int32, (R * W, 1), 0)
    col = jnp.full((R * W, 1), sink_ref[g * R + R - 1], F32)
    for r in range(R - 2, -1, -1):
        col = jnp.where(rid < (r + 1) * W, sink_ref[g * R + r], col)
    return col


def _swa_attn_kernel(sink_ref, q_ref, kp_ref, km_ref, kn_ref, vp_ref, vm_ref, vn_ref, kc_ref, vc_ref, gate_ref,
                     o_ref, *, bpt, W, R, G):
    i = pl.program_id(1)
    nblk = pl.num_programs(1) * bpt
    kcat = jnp.concatenate([kp_ref[0], km_ref[0], kn_ref[0]], 0)
    vcat = jnp.concatenate([vp_ref[0], vm_ref[0], vn_ref[0]], 0)
    kc, vc = kc_ref[0], vc_ref[0]
    lane = lax.broadcasted_iota(jnp.int32, (W, LANES), 1)
    qi = lax.broadcasted_iota(jnp.int32, (R * W, 3 * W), 0) % W
    kj = lax.broadcasted_iota(jnp.int32, (R * W, 3 * W), 1)
    band = jnp.abs(W + qi - kj) <= W
    hd = LANES // G
    for jb in range(bpt):
        gblk = i * bpt + jb
        kw = kcat[jb * W:(jb + 3) * W]
        vw = vcat[jb * W:(jb + 3) * W]
        valid = band & ((kj >= W) | (gblk > 0)) & ((kj < 2 * W) | (gblk < nblk - 1))
        tiles = [q_ref[0, jb * W:(jb + 1) * W, r * LANES:(r + 1) * LANES] for r in range(R)]
        outg = []
        for g in range(G):
            sel = (lane >= g * hd) & (lane < (g + 1) * hd)
            qg = jnp.concatenate([jnp.where(sel, t, jnp.zeros_like(t)) for t in tiles], 0)
            s_w = jnp.where(valid, _dot_nt(qg, kw), NEG)
            s_c = _dot_nt(qg, kc)
            sk = _sink_column(sink_ref, g, R, W)
            m = jnp.maximum(jnp.maximum(jnp.max(s_w, -1, keepdims=True), jnp.max(s_c, -1, keepdims=True)), sk)
            p_w = jnp.exp(s_w - m)
            p_c = jnp.exp(s_c - m)
            den = jnp.exp(sk - m) + jnp.sum(p_c, -1, keepdims=True) + jnp.sum(p_w, -1, keepdims=True)
            o = _dot(p_c.astype(BF16), vc) + _dot(p_w.astype(BF16), vw)
            outg.append(o / den)
        for r in range(R):
            y = outg[G - 1][r * W:(r + 1) * W]
            for g in range(G - 2, -1, -1):
                y = jnp.where(lane < (g + 1) * hd, outg[g][r * W:(r + 1) * W], y)
            gt = gate_ref[0, jb * W:(jb + 1) * W, r * LANES:(r + 1) * LANES]
            o_ref[0, jb * W:(jb + 1) * W, r * LANES:(r + 1) * LANES] = y * _silu(gt)


def _swa_attn(q, k, v, kc, vc, sink, gate, bpt):
    B, S, NQ = q.shape
    Lc = kc.shape[1]
    W = WINDOW
    G = SWA_KV_HEADS
    R = SWA_HEADS // G
    nb = S // W
    T = bpt * W
    kern = functools.partial(_swa_attn_kernel, bpt=bpt, W=W, R=R, G=G)
    main = pl.BlockSpec((1, T, LANES), lambda b, i: (b, i, 0))
    prev = pl.BlockSpec((1, W, LANES), lambda b, i: (b, jnp.maximum(i * bpt - 1, 0), 0))
    nxt = pl.BlockSpec((1, W, LANES), lambda b, i: (b, jnp.minimum((i + 1) * bpt, nb - 1), 0))
    ctx = pl.BlockSpec((1, Lc, LANES), lambda b, i: (b, 0, 0))
    qs = pl.BlockSpec((1, T, NQ), lambda b, i: (b, i, 0))
    return pl.pallas_call(
        kern, name="swa_attn", grid=(B, S // T),
        in_specs=[pl.BlockSpec(memory_space=pltpu.SMEM), qs, prev, main, nxt, prev, main, nxt, ctx, ctx, qs],
        out_specs=qs,
        out_shape=jax.ShapeDtypeStruct((B, S, NQ), F32),
        compiler_params=_params(("parallel", "arbitrary")),
    )(sink, q, k, k, k, v, v, v, kc, vc, gate)


def _swa_ctx_kernel(sink_ref, q_ref, kc_ref, vc_ref, gate_ref, o_ref, *, Lc, R, G, scale):
    kc, vc = kc_ref[0], vc_ref[0]
    lane = lax.broadcasted_iota(jnp.int32, (Lc, LANES), 1)
    hd = LANES // G
    tiles = [(q_ref[0, :, r * LANES:(r + 1) * LANES] * scale).astype(BF16) for r in range(R)]
    outg = []
    for g in range(G):
        sel = (lane >= g * hd) & (lane < (g + 1) * hd)
        qg = jnp.concatenate([jnp.where(sel, t, jnp.zeros_like(t)) for t in tiles], 0)
        s_c = _dot_nt(qg, kc)
        sk = _sink_column(sink_ref, g, R, Lc)
        m = jnp.maximum(jnp.max(s_c, -1, keepdims=True), sk)
        p_c = jnp.exp(s_c - m)
        den = jnp.exp(sk - m) + jnp.sum(p_c, -1, keepdims=True)
        outg.append(_dot(p_c.astype(BF16), vc) / den)
    for r in range(R):
        y = outg[G - 1][r * Lc:(r + 1) * Lc]
        for g in range(G - 2, -1, -1):
            y = jnp.where(lane < (g + 1) * hd, outg[g][r * Lc:(r + 1) * Lc], y)
        o_ref[0, :, r * LANES:(r + 1) * LANES] = y * _silu(gate_ref[0, :, r * LANES:(r + 1) * LANES])


def _swa_ctx(q, kc, vc, sink, gate):
    B, Lc, NQ = q.shape
    G = SWA_KV_HEADS
    R = SWA_HEADS // G
    kern = functools.partial(_swa_ctx_kernel, Lc=Lc, R=R, G=G, scale=SWA_HD ** -0.5)
    qs = pl.BlockSpec((1, Lc, NQ), lambda b: (b, 0, 0))
    ctx = pl.BlockSpec((1, Lc, LANES), lambda b: (b, 0, 0))
    return pl.pallas_call(
        kern, name="swa_ctx", grid=(B,),
        in_specs=[pl.BlockSpec(memory_space=pltpu.SMEM), qs, ctx, ctx, qs],
        out_specs=qs,
        out_shape=jax.ShapeDtypeStruct((B, Lc, NQ), F32),
        compiler_params=_params(("arbitrary",)),
    )(sink, q, kc, vc, gate)


def _merge_kernel(yh_ref, ym_ref, ys_ref, mg_ref, x_ref, mod_ref, wh, wm, ws, wo, lg, lb, o_ref, *, D, alpha):
    m = (jax.nn.sigmoid(mg_ref[0, :, 0:D]) * _dot(yh_ref[0].astype(BF16), wh[...])
         + jax.nn.sigmoid(mg_ref[0, :, D:2 * D]) * _dot(ym_ref[0].astype(BF16), wm[...])
         + jax.nn.sigmoid(mg_ref[0, :, 2 * D:3 * D]) * _dot(ys_ref[0].astype(BF16), ws[...]))
    out = _dot(m.astype(BF16), wo[...])
    r = alpha * x_ref[0] + mod_ref[0, :, 2 * D:3 * D] * out
    mu = jnp.mean(r, -1, keepdims=True)
    rc = r - mu
    var = jnp.mean(rc * rc, -1, keepdims=True)
    o_ref[0] = rc * lax.rsqrt(var + EPS) * lg[...] + lb[...]


def _merge(yh, ym, ys, mg, x, mod, wh, wm, ws, wo, lg, lb, alpha, tm):
    Bx, S, D = x.shape
    kern = functools.partial(_merge_kernel, D=D, alpha=alpha)
    tok = lambda w: pl.BlockSpec((1, tm, w), lambda b, i: (b, i, 0))
    cs = _const_spec
    return pl.pallas_call(
        kern, name="merge", grid=(Bx, S // tm),
        in_specs=[tok(yh.shape[2]), tok(ym.shape[2]), tok(ys.shape[2]), tok(3 * D), tok(D),
                  pl.BlockSpec((1, 1, 3 * D), lambda b, i: (b, 0, 0)),
                  cs(wh.shape), cs(wm.shape), cs(ws.shape), cs(wo.shape), cs((1, D)), cs((1, D))],
        out_specs=tok(D),
        out_shape=jax.ShapeDtypeStruct((Bx, S, D), F32),
        compiler_params=_params(("parallel", "arbitrary")),
    )(yh, ym, ys, mg, x, mod, wh, wm, ws, wo, lg.reshape(1, D), lb.reshape(1, D))


_PAIR_ORDER = (0, 4, 1, 5, 2, 6, 3, 7)

_IN_COLS = (3 * HY_W, HY_W, MLA_Q_RANK, MLA_KV_RANK, MLA_ROPE, MLA_HEADS * MLA_V,
            SWA_HEADS * SWA_HD, SWA_KV_HEADS * SWA_HD, SWA_KV_HEADS * SWA_HD, SWA_HEADS * SWA_HD)
_SLAB_WIDTHS = (None, 3 * HY_W, HY_W, MLA_Q_RANK, MLA_KV_RANK, LANES, MLA_HEADS * MLA_V,
                SWA_HEADS * SWA_HD, LANES, LANES, SWA_HEADS * SWA_HD)


def _pair_cols(w):
    D = w.shape[0]
    return w.reshape(D, SWA_HEADS, SWA_HD)[:, _PAIR_ORDER, :].reshape(D, SWA_HEADS * SWA_HD)


def _layout_w_in(w_in, D):
    offs = [0]
    for cw in _IN_COLS:
        offs.append(offs[-1] + cw)
    sl = [w_in[:, offs[i]:offs[i + 1]] for i in range(len(_IN_COLS))]
    hy, hyg, cq, ckv, kr, mlag, sq, sk, sv, swag = sl
    mg = w_in[:, offs[-1]:]
    kr_p = jnp.pad(kr, ((0, 0), (MLA_NOPE, LANES - MLA_NOPE - MLA_ROPE)))
    return jnp.concatenate([mg, hy, hyg, cq, ckv, kr_p, mlag, _pair_cols(sq), sk, sv, _pair_cols(swag)],
                           1).astype(BF16)


def _layer_weights(p, D):
    w = {}
    w['w_in'] = _layout_w_in(p['w_in'], D)
    uq = p['mla_w_uq'].reshape(MLA_Q_RANK, MLA_HEADS, MLA_QK)
    w['w_uq'] = jnp.pad(uq, ((0, 0), (0, 0), (0, LANES - MLA_QK))).reshape(MLA_Q_RANK, MLA_HEADS * LANES).astype(BF16)
    ukv = p['mla_w_ukv'].reshape(MLA_KV_RANK, MLA_HEADS, MLA_NOPE + MLA_V)
    w['w_uk'] = jnp.pad(ukv[:, :, :MLA_NOPE], ((0, 0), (0, 0), (0, LANES - MLA_NOPE))).reshape(
        MLA_KV_RANK, MLA_HEADS * LANES).astype(BF16)
    w['w_uv'] = ukv[:, :, MLA_NOPE:].reshape(MLA_KV_RANK, MLA_HEADS * MLA_V).astype(BF16)
    w['w_proj_hy'] = p['w_proj_hy'].astype(BF16)
    w['w_proj_mla'] = p['w_proj_mla'].astype(BF16)
    w['w_proj_swa'] = p['w_proj_swa'].reshape(SWA_HEADS, SWA_HD, D)[_PAIR_ORDER, :, :].reshape(
        SWA_HEADS * SWA_HD, D).astype(BF16)
    w['w_out'] = p['w_out'].astype(BF16)
    return w


def _trunk_layer(xl, xc, cvec, p, tabs, alpha, ctx_out):
    B, S, D = xl.shape
    Lc = xc.shape[1]
    w = _layer_weights(p, D)
    widths = (N_BRANCH * D,) + _SLAB_WIDTHS[1:]

    mod = _ada_mod(cvec, p['w_ada'], p['b_ada'])
    mod_l = mod[:B].reshape(B, 1, 3 * D)
    mod_c = jnp.broadcast_to(mod[B].reshape(1, 1, 3 * D), (B, 1, 3 * D))

    tm_c = min(Lc, 256)
    (mg_l, hy_l, hyg_l, cq_l, ckv_l, kr_l, mlag_l, sq_l, sk_l, sv_l, swag_l) = _in_proj(xl, mod_l, w['w_in'], widths, 256)
    (mg_c, hy_c, hyg_c, cq_c, ckv_c, kr_c, mlag_c, sq_c, sk_c, sv_c, swag_c) = _in_proj(xc, mod_c, w['w_in'], widths, tm_c)

    k_mc, v_mc = _mla_kv(ckv_c, kr_c, p['mla_kv_norm'], w['w_uk'], w['w_uv'], tabs['id_c'], tm_c)
    k_sc = sk_c.astype(BF16)
    v_sc = sv_c.astype(BF16)

    u_l, e_l = _hy_pre(hy_l, hyg_l, p['hy_conv_w'], p['hy_conv_b'], 512)
    h2_l = _hy_filters(S, p)
    y_hy = _hy_long_conv(u_l, e_l, h2_l, p['hy_skip'], tabs['dft'])

    q_ml = _mla_q(cq_l, p['mla_q_norm'], w['w_uq'], tabs['mla'], 512)
    k_ml, v_ml = _mla_kv(ckv_l, kr_l, p['mla_kv_norm'], w['w_uk'], w['w_uv'], tabs['mla'], 512)
    y_mla = _mla_attn(q_ml, k_mc, v_mc, k_ml, v_ml, mlag_l, 256, 512)

    q_sl, k_sl, v_sl = _swa_prep(sq_l, sk_l, sv_l, tabs['swa'], 512)
    y_swa = _swa_attn(q_sl, k_sl, v_sl, k_sc, v_sc, p['swa_sink'], swag_l, 4)

    xl_new = _merge(y_hy, y_mla, y_swa, mg_l, xl, mod_l, w['w_proj_hy'], w['w_proj_mla'], w['w_proj_swa'],
                    w['w_out'], p['ln_g'], p['ln_b'], alpha, 256)
    if not ctx_out:
        return xl_new, xc

    u_c, e_c = _hy_pre(hy_c, hyg_c, p['hy_conv_w'], p['hy_conv_b'], tm_c)
    h2_c = _hy_filters(Lc, p)
    yc_hy = _hy_ctx_conv(u_c, e_c, h2_c, p['hy_skip'], tabs['dft_c'])
    q_mc = _mla_q(cq_c, p['mla_q_norm'], w['w_uq'], tabs['id_c'], tm_c)
    yc_mla = _mla_attn(q_mc, k_mc, v_mc, None, None, mlag_c, tm_c, 0)
    yc_swa = _swa_ctx(sq_c, k_sc, v_sc, p['swa_sink'], swag_c)
    xc_new = _merge(yc_hy, yc_mla, yc_swa, mg_c, xc, mod_c, w['w_proj_hy'], w['w_proj_mla'], w['w_proj_swa'],
                    w['w_out'], p['ln_g'], p['ln_b'], alpha, tm_c)
    return xl_new, xc_new


def kernel(x, c, ctx, c_ctx, w_ada, b_ada, w_in, hy_conv_w, hy_conv_b, filt_w1, filt_b1, filt_w2, filt_b2,
           filt_w3, filt_b3, filt_freq, filt_w_out, hy_skip, mla_q_norm, mla_w_uq, mla_kv_norm, mla_w_ukv,
           swa_sink, w_proj_hy, w_proj_mla, w_proj_swa, w_out, ln_g, ln_b):
    B, S, D = x.shape
    Lc = ctx.shape[1]
    depth = w_in.shape[0]
    alpha = (2 * depth) ** 0.25
    stacked = dict(w_ada=w_ada, b_ada=b_ada, w_in=w_in, hy_conv_w=hy_conv_w, hy_conv_b=hy_conv_b,
                   filt_w1=filt_w1, filt_b1=filt_b1, filt_w2=filt_w2, filt_b2=filt_b2, filt_w3=filt_w3,
                   filt_b3=filt_b3, filt_freq=filt_freq, filt_w_out=filt_w_out, hy_skip=hy_skip,
                   mla_q_norm=mla_q_norm, mla_w_uq=mla_w_uq, mla_kv_norm=mla_kv_norm, mla_w_ukv=mla_w_ukv,
                   swa_sink=swa_sink, w_proj_hy=w_proj_hy, w_proj_mla=w_proj_mla, w_proj_swa=w_proj_swa,
                   w_out=w_out, ln_g=ln_g, ln_b=ln_b)

    lane = jnp.arange(LANES, dtype=jnp.int32)
    mla_on = (lane >= MLA_NOPE) & (lane < MLA_QK)
    tabs = {
        'mla': _rope_tables(S, jnp.clip(lane - MLA_NOPE, 0, MLA_ROPE - 1), mla_on, MLA_ROPE),
        'swa': _rope_tables(S, lane % SWA_HD, jnp.ones((LANES,), bool), SWA_HD),
        'id_c': _rope_tables(Lc, None, None, None, identity=True),
        'dft': _dft_tables(S),
        'dft_c': _ctx_dft_tables(Lc),
    }
    cvec = jnp.concatenate([c, c_ctx[None, :], jnp.zeros((8 - B - 1, D), F32)], 0)

    xl, xc = x, ctx
    for l in range(depth):
        p = {k: v[l] for k, v in stacked.items()}
        xl, xc = _trunk_layer(xl, xc, cvec, p, tabs, alpha, l < depth - 1)
    return xl
```

```python
import functools
import math

import jax
import jax.numpy as jnp
from jax import lax
from jax.experimental import pallas as pl
from jax.experimental.pallas import tpu as pltpu

F32 = jnp.float32
BF16 = jnp.bfloat16
HIGHEST = lax.Precision.HIGHEST

GRID_W = 64
HY_W = 512
SHORT_K = 3
FILT_BANDS = 16
FILT_W = 64
DECAY_TARGET = 1e-2
FAST_DECAY_PCT = 0.3
SLOW_DECAY_PCT = 1.5
MLA_HEADS = 8
MLA_NOPE = 64
MLA_ROPE = 32
MLA_V = 64
MLA_QK = MLA_NOPE + MLA_ROPE
MLA_Q_RANK = 256
MLA_KV_RANK = 128
SWA_HEADS = 8
SWA_KV_HEADS = 2
SWA_HD = 64
WINDOW = 128
N_BRANCH = 3
ROPE_BASE = 10000.0
EPS = 1e-6
NEG = -1e30

LANES = 128
DFT_N2 = 128
VMEM_LIMIT = 56 * 1024 * 1024


def _silu(x):
    return x * jax.nn.sigmoid(x)


def _dot(a, b):
    return jnp.dot(a, b, preferred_element_type=F32)


def _dot_nt(a, b):
    return lax.dot_general(a, b, (((1,), (1,)), ((), ())), preferred_element_type=F32)


def _params(sem):
    return pltpu.CompilerParams(dimension_semantics=sem, vmem_limit_bytes=VMEM_LIMIT)


def _const_spec(shape):
    nd = len(shape)
    return pl.BlockSpec(shape, lambda *_: (0,) * nd)


def _ada_kernel(c_ref, w_ref, b_ref, o_ref):
    a = _silu(c_ref[...])
    o_ref[...] = jnp.dot(a, w_ref[...], precision=HIGHEST, preferred_element_type=F32) + b_ref[...]


def _ada_mod(cvec, w_ada, b_ada):
    R, D = cvec.shape
    N = w_ada.shape[1]
    tn = 768
    return pl.pallas_call(
        _ada_kernel, name="ada_mod", grid=(N // tn,),
        in_specs=[_const_spec((R, D)), pl.BlockSpec((D, tn), lambda j: (0, j)),
                  pl.BlockSpec((1, tn), lambda j: (0, j))],
        out_specs=pl.BlockSpec((R, tn), lambda j: (0, j)),
        out_shape=jax.ShapeDtypeStruct((R, N), F32),
        compiler_params=_params(("arbitrary",)),
    )(cvec, w_ada, b_ada.reshape(1, N))


def _in_proj_kernel(x_ref, mod_ref, w_ref, *o_refs, widths, D):
    x = x_ref[0]
    mu = jnp.mean(x, -1, keepdims=True)
    xc = x - mu
    var = jnp.mean(xc * xc, -1, keepdims=True)
    ln = xc * lax.rsqrt(var + EPS)
    shift = mod_ref[0, :, 0:D]
    scale = mod_ref[0, :, D:2 * D]
    u = (ln * (1.0 + scale) + shift).astype(BF16)
    off = 0
    for o_ref, wd in zip(o_refs, widths):
        o_ref[0] = _dot(u, w_ref[:, off:off + wd]).astype(o_ref.dtype)
        off += wd


def _in_proj(x, mod, w, widths, tm):
    Bx, S, D = x.shape
    P = w.shape[1]
    kern = functools.partial(_in_proj_kernel, widths=tuple(widths), D=D)
    return pl.pallas_call(
        kern, name="in_proj", grid=(Bx, S // tm),
        in_specs=[pl.BlockSpec((1, tm, D), lambda b, i: (b, i, 0)),
                  pl.BlockSpec((1, 1, 3 * D), lambda b, i: (b, 0, 0)),
                  pl.BlockSpec((D, P), lambda b, i: (0, 0), pipeline_mode=pl.Buffered(1))],
        out_specs=[pl.BlockSpec((1, tm, wd), lambda b, i: (b, i, 0)) for wd in widths],
        out_shape=[jax.ShapeDtypeStruct((Bx, S, wd), F32) for wd in widths],
        compiler_params=_params(("parallel", "arbitrary")),
    )(x, mod, w)


def _filter_kernel(z_ref, t_ref, w1, b1, w2, b2, w3, b3, fr, wo, dl_ref, o_ref, *, tl, C):
    f = fr[...]
    h = jnp.sin(f * (jnp.dot(z_ref[...], w1[...], precision=HIGHEST, preferred_element_type=F32) + b1[...]))
    h = jnp.sin(f * (jnp.dot(h, w2[...], precision=HIGHEST, preferred_element_type=F32) + b2[...]))
    h = jnp.sin(f * (jnp.dot(h, w3[...], precision=HIGHEST, preferred_element_type=F32) + b3[...]))
    h = jnp.dot(h, wo[...], precision=HIGHEST, preferred_element_type=F32)
    decay = jnp.exp(-t_ref[...] * jnp.abs(dl_ref[...]))
    row = pl.program_id(0) * tl + lax.broadcasted_iota(jnp.int32, (tl, 1), 0)
    o_ref[:, 0:C] = h[:, 0:C] * decay
    o_ref[:, C:2 * C] = jnp.where(row == 0, 0.0, h[:, C:2 * C] * decay)


def _hy_filters(L, p):
    C = HY_W
    FP = LANES
    t = jnp.linspace(0.0, 1.0, L, dtype=F32)[:, None]
    w = 2.0 * math.pi * jnp.arange(L, dtype=F32) / L
    f = jnp.linspace(1e-4, FILT_BANDS - 1, FILT_BANDS, dtype=F32)
    ang = w[:, None] * f[None, :]
    z = jnp.concatenate([t, jnp.cos(ang), -jnp.sin(ang)], -1)
    z = jnp.pad(z, ((0, 0), (0, FP - z.shape[1])))
    pad_w = lambda a: jnp.pad(a, ((0, FP - a.shape[0]), (0, FP - a.shape[1])))
    pad_v = lambda a: jnp.pad(a, (0, FP - a.shape[0])).reshape(1, FP)
    w1, w2, w3 = pad_w(p['filt_w1']), pad_w(p['filt_w2']), pad_w(p['filt_w3'])
    wo = jnp.pad(p['filt_w_out'], ((0, FP - FILT_W), (0, 0)))
    min_decay = math.log(DECAY_TARGET) / SLOW_DECAY_PCT
    max_decay = math.log(DECAY_TARGET) / FAST_DECAY_PCT
    deltas = jnp.linspace(min_decay, max_decay, C, dtype=F32).reshape(1, C)
    tl = min(L, 512)
    kern = functools.partial(_filter_kernel, tl=tl, C=C)
    cs = _const_spec
    return pl.pallas_call(
        kern, name="hy_filter", grid=(L // tl,),
        in_specs=[pl.BlockSpec((tl, FP), lambda i: (i, 0)), pl.BlockSpec((tl, 1), lambda i: (i, 0)),
                  cs((FP, FP)), cs((1, FP)), cs((FP, FP)), cs((1, FP)), cs((FP, FP)), cs((1, FP)),
                  cs((1, FP)), cs((FP, 2 * C)), cs((1, C))],
        out_specs=pl.BlockSpec((tl, 2 * C), lambda i: (i, 0)),
        out_shape=jax.ShapeDtypeStruct((L, 2 * C), F32),
        compiler_params=_params(("arbitrary",)),
    )(z, t, w1, pad_v(p['filt_b1']), w2, pad_v(p['filt_b2']), w3, pad_v(p['filt_b3']),
      pad_v(p['filt_freq']), wo, deltas)


def _hy_pre_kernel(x_ref, xp_ref, xn_ref, g_ref, w_ref, b_ref, u_ref, e_ref, *, ts, C):
    i = pl.program_id(1)
    nt = pl.num_programs(1)
    x = x_ref[0]
    prev_row = xp_ref[0, 7:8, :] * jnp.where(i > 0, 1.0, 0.0)
    next_row = xn_ref[0, 0:1, :] * jnp.where(i < nt - 1, 1.0, 0.0)
    rid = lax.broadcasted_iota(jnp.int32, (ts, 1), 0)
    xm = jnp.where(rid == 0, prev_row, pltpu.roll(x, 1, 0))
    xq = jnp.where(rid == ts - 1, next_row, pltpu.roll(x, ts - 1, 0))
    z = b_ref[...] + xm * w_ref[0:1, :] + x * w_ref[1:2, :] + xq * w_ref[2:3, :]
    u_ref[0] = z[:, 2 * C:3 * C] * z[:, C:2 * C]
    e_ref[0] = z[:, 0:C] * _silu(g_ref[0])


def _hy_pre(hy, hyg, conv_w, conv_b, ts):
    Bx, S, C3 = hy.shape
    C = C3 // 3
    nb8 = S // 8
    r = ts // 8
    w8 = jnp.pad(conv_w, ((0, 8 - SHORT_K), (0, 0)))
    kern = functools.partial(_hy_pre_kernel, ts=ts, C=C)
    return pl.pallas_call(
        kern, name="hy_pre", grid=(Bx, S // ts),
        in_specs=[pl.BlockSpec((1, ts, C3), lambda b, i: (b, i, 0)),
                  pl.BlockSpec((1, 8, C3), lambda b, i: (b, jnp.maximum(i * r - 1, 0), 0)),
                  pl.BlockSpec((1, 8, C3), lambda b, i: (b, jnp.minimum((i + 1) * r, nb8 - 1), 0)),
                  pl.BlockSpec((1, ts, C), lambda b, i: (b, i, 0)),
                  _const_spec((8, C3)), _const_spec((1, C3))],
        out_specs=[pl.BlockSpec((1, ts, C), lambda b, i: (b, i, 0))] * 2,
        out_shape=[jax.ShapeDtypeStruct((Bx, S, C), F32)] * 2,
        compiler_params=_params(("parallel", "arbitrary")),
    )(hy, hy, hy, hyg, w8, conv_b.reshape(1, C3))


def _dft_tables(L):
    n = 2 * L
    N2 = DFT_N2
    N1 = n // N2
    H1 = N1 // 2
    k1 = jnp.arange(N1, dtype=jnp.int32)
    t1 = jnp.arange(H1, dtype=jnp.int32)
    ang = (2.0 * math.pi / N1) * ((k1[:, None] * t1[None, :]) % N1).astype(F32)
    fa = jnp.concatenate([jnp.cos(ang), -jnp.sin(ang)], 0)
    fai = fa.T * (1.0 / n)
    k2 = jnp.arange(N2, dtype=jnp.int32)
    t2 = jnp.arange(N2, dtype=jnp.int32)
    m = (t2[None, None, :] * (k2[None, :, None] * N1 + k1[:, None, None])) % n
    th = (2.0 * math.pi / n) * m.astype(F32)
    cr, ci = jnp.cos(th), -jnp.sin(th)
    mf = jnp.concatenate([jnp.concatenate([cr, -ci], 2), jnp.concatenate([ci, cr], 2)], 1)
    mi = jnp.swapaxes(mf, 1, 2)
    return fa.astype(BF16), fai.astype(BF16), mf.astype(BF16), mi.astype(BF16)


def _dft_a_fwd_kernel(u_ref, f_ref, o_ref):
    o_ref[0] = _dot(f_ref[...], u_ref[0].astype(BF16)).astype(o_ref.dtype)


def _dft_a_fwd(u2, fa, tc):
    Bx, H1, W = u2.shape
    R = fa.shape[0]
    return pl.pallas_call(
        _dft_a_fwd_kernel, name="dft_a_fwd", grid=(Bx, W // tc),
        in_specs=[pl.BlockSpec((1, H1, tc), lambda b, j: (b, 0, j)), _const_spec((R, H1))],
        out_specs=pl.BlockSpec((1, R, tc), lambda b, j: (b, 0, j)),
        out_shape=jax.ShapeDtypeStruct((Bx, R, W), BF16),
        compiler_params=_params(("parallel", "arbitrary")),
    )(u2, fa)


def _dft_b_filter_kernel(a_ref, m_ref, k_ref, *, kb, C, N2):
    for j in range(kb):
        a = a_ref[0, :, j].reshape(2 * N2, 2 * C)
        h = _dot(m_ref[j], a)
        k_ref[j, 0:N2, :] = h[0:N2, 0:C] + h[0:N2, C:2 * C]
        k_ref[j, N2:2 * N2, :] = h[N2:2 * N2, 0:C] - h[N2:2 * N2, C:2 * C]


def _dft_b_filter(ah, mf, kb):
    _, _, N1, N2, C2 = ah.shape
    C = C2 // 2
    kern = functools.partial(_dft_b_filter_kernel, kb=kb, C=C, N2=N2)
    return pl.pallas_call(
        kern, name="dft_b_filter", grid=(N1 // kb,),
        in_specs=[pl.BlockSpec((1, 2, kb, N2, C2), lambda i: (0, 0, i, 0, 0)),
                  pl.BlockSpec((kb, 2 * N2, 2 * N2), lambda i: (i, 0, 0))],
        out_specs=pl.BlockSpec((kb, 2 * N2, C), lambda i: (i, 0, 0)),
        out_shape=jax.ShapeDtypeStruct((N1, 2 * N2, C), F32),
        compiler_params=_params(("arbitrary",)),
    )(ah, mf)


def _dft_mid_kernel(a_ref, mf_ref, k_ref, mi_ref, g_ref, *, kb, nb, C, N2):
    for j in range(kb):
        kre = k_ref[j, 0:N2, :]
        kim = k_ref[j, N2:2 * N2, :]
        for b in range(nb):
            a = a_ref[b, :, j].reshape(2 * N2, C)
            y = _dot(mf_ref[j], a)
            yre, yim = y[0:N2], y[N2:2 * N2]
            z = jnp.concatenate([yre * kre - yim * kim, yre * kim + yim * kre], 0).astype(BF16)
            g = _dot(mi_ref[j], z)
            g_ref[b, :, j] = g.reshape(2, N2, C).astype(g_ref.dtype)


def _dft_mid(a5, mf, kf, mi, kb):
    Bx, _, N1, N2, C = a5.shape
    kern = functools.partial(_dft_mid_kernel, kb=kb, nb=Bx, C=C, N2=N2)
    return pl.pallas_call(
        kern, name="dft_mid", grid=(N1 // kb,),
        in_specs=[pl.BlockSpec((Bx, 2, kb, N2, C), lambda i: (0, 0, i, 0, 0)),
                  pl.BlockSpec((kb, 2 * N2, 2 * N2), lambda i: (i, 0, 0)),
                  pl.BlockSpec((kb, 2 * N2, C), lambda i: (i, 0, 0)),
                  pl.BlockSpec((kb, 2 * N2, 2 * N2), lambda i: (i, 0, 0))],
        out_specs=pl.BlockSpec((Bx, 2, kb, N2, C), lambda i: (0, 0, i, 0, 0)),
        out_shape=jax.ShapeDtypeStruct(a5.shape, BF16),
        compiler_params=_params(("arbitrary",)),
    )(a5, mf, kf, mi)


def _dft_a_inv_kernel(g_ref, f_ref, u_ref, e_ref, s_ref, o_ref):
    y = _dot(f_ref[...], g_ref[0])
    o_ref[0] = (y + u_ref[0] * s_ref[...]) * e_ref[0]


def _dft_a_inv(g2, fai, u2, e2, skip_t, tc):
    Bx, R, W = g2.shape
    H1 = fai.shape[0]
    return pl.pallas_call(
        _dft_a_inv_kernel, name="dft_a_inv", grid=(Bx, W // tc),
        in_specs=[pl.BlockSpec((1, R, tc), lambda b, j: (b, 0, j)), _const_spec((H1, R)),
                  pl.BlockSpec((1, H1, tc), lambda b, j: (b, 0, j)),
                  pl.BlockSpec((1, H1, tc), lambda b, j: (b, 0, j)),
                  _const_spec((1, tc))],
        out_specs=pl.BlockSpec((1, H1, tc), lambda b, j: (b, 0, j)),
        out_shape=jax.ShapeDtypeStruct((Bx, H1, W), F32),
        compiler_params=_params(("parallel", "arbitrary")),
    )(g2, fai, u2, e2, skip_t)


def _hy_long_conv(u, e, h2, skip, tabs):
    B, L, C = u.shape
    fa, fai, mf, mi = tabs
    N2 = DFT_N2
    N1 = 2 * L // N2
    H1 = N1 // 2
    tc = 4096
    kb = 4
    ah = _dft_a_fwd(h2.reshape(1, H1, N2 * 2 * C), fa, tc)
    kf = _dft_b_filter(ah.reshape(1, 2, N1, N2, 2 * C), mf, kb)
    u2 = u.reshape(B, H1, N2 * C)
    a = _dft_a_fwd(u2, fa, tc)
    g = _dft_mid(a.reshape(B, 2, N1, N2, C), mf, kf, mi, kb)
    skip_t = jnp.tile(skip.reshape(1, C), (1, tc // C))
    y = _dft_a_inv(g.reshape(B, 2 * N1, N2 * C), fai, u2, e.reshape(B, H1, N2 * C), skip_t, tc)
    return y.reshape(B, L, C)


def _ctx_dft_tables(Lc):
    n = 2 * Lc
    k = jnp.arange(n, dtype=jnp.int32)
    t = jnp.arange(Lc, dtype=jnp.int32)
    ang = (2.0 * math.pi / n) * ((k[:, None] * t[None, :]) % n).astype(F32)
    fc = jnp.concatenate([jnp.cos(ang), -jnp.sin(ang)], 0)
    fi = fc.T * (1.0 / n)
    return fc.astype(BF16), fi.astype(BF16)


def _hy_ctx_conv_kernel(u_ref, e_ref, h_ref, fc_ref, fi_ref, s_ref, o_ref, *, n, C):
    u = u_ref[0]
    fc = fc_ref[...]
    uf = _dot(fc, u.astype(BF16))
    hf = _dot(fc, h_ref[...].astype(BF16))
    kre = hf[0:n, 0:C] + hf[0:n, C:2 * C]
    kim = hf[n:2 * n, 0:C] - hf[n:2 * n, C:2 * C]
    ure, uim = uf[0:n], uf[n:2 * n]
    z = jnp.concatenate([ure * kre - uim * kim, ure * kim + uim * kre], 0).astype(BF16)
    y = _dot(fi_ref[...], z)
    o_ref[0] = (y + u * s_ref[...]) * e_ref[0]


def _hy_ctx_conv(u, e, h2, skip, tabs):
    B, Lc, C = u.shape
    fc, fi = tabs
    n = 2 * Lc
    kern = functools.partial(_hy_ctx_conv_kernel, n=n, C=C)
    blk = pl.BlockSpec((1, Lc, C), lambda b: (b, 0, 0))
    return pl.pallas_call(
        kern, name="hy_ctx_conv", grid=(B,),
        in_specs=[blk, blk, _const_spec((Lc, 2 * C)), _const_spec((2 * n, Lc)), _const_spec((Lc, 2 * n)),
                  _const_spec((1, C))],
        out_specs=blk,
        out_shape=jax.ShapeDtypeStruct((B, Lc, C), F32),
        compiler_params=_params(("arbitrary",)),
    )(u, e, h2, fc, fi, skip.reshape(1, C))


def _rope_tables(S, lane_dim, lane_on, head_rot, identity=False):
    if identity:
        return (jnp.ones((S, LANES), F32), jnp.zeros((S, LANES), F32), jnp.zeros((S, LANES), F32))
    seg_w = head_rot // 2
    half = seg_w // 2
    seg = lane_dim // seg_w
    w = lane_dim % seg_w
    first = w < half
    inv = ROPE_BASE ** (-(w % half).astype(F32) / half)
    nrow = S // GRID_W
    ang_r = jnp.arange(nrow, dtype=jnp.int32).astype(F32)[:, None] * inv[None, :]
    ang_c = jnp.arange(GRID_W, dtype=jnp.int32).astype(F32)[:, None] * inv[None, :]

    def expand(fn):
        tab = jnp.where((seg == 0)[None, None, :], fn(ang_r)[:, None, :], fn(ang_c)[None, :, :])
        return tab.reshape(S, LANES)

    cos, sin = expand(jnp.cos), expand(jnp.sin)
    on = lane_on[None, :]
    c = jnp.where(on, cos, 1.0)
    sm = jnp.where(on & first[None, :], -sin, 0.0)
    sp = jnp.where(on & (~first)[None, :], sin, 0.0)
    return c, sm, sp


def _rope(x, c, sm, sp, hh):
    return x * c + pltpu.roll(x, LANES - hh, 1) * sm + pltpu.roll(x, hh, 1) * sp


MLA_VROWS = 80

def _mla_q_kernel(cq_ref, g_ref, w_ref, c_ref, sm_ref, sp_ref, o_ref, *, nh, scale, hh):
    x = cq_ref[0]
    xn = x * lax.rsqrt(jnp.mean(x * x, -1, keepdims=True) + EPS) * g_ref[...]
    q = _dot(xn.astype(BF16), w_ref[...])
    c, sm, sp = c_ref[...], sm_ref[...], sp_ref[...]
    for h in range(nh):
        qh = _rope(q[:, h * LANES:(h + 1) * LANES], c, sm, sp, hh) * scale
        o_ref[0, h * LANES:(h + 1) * LANES, :] = qh.T.astype(o_ref.dtype)


def _mla_q(cq, qnorm, w_uq_p, tabs, tm):
    Bx, S, R = cq.shape
    N = w_uq_p.shape[1]
    kern = functools.partial(_mla_q_kernel, nh=MLA_HEADS, scale=MLA_QK ** -0.5 * math.log2(math.e),
                             hh=MLA_ROPE // 4)
    tab = pl.BlockSpec((tm, LANES), lambda b, i: (i, 0))
    return pl.pallas_call(
        kern, name="mla_q", grid=(Bx, S // tm),
        in_specs=[pl.BlockSpec((1, tm, R), lambda b, i: (b, i, 0)), _const_spec((1, R)), _const_spec((R, N)),
                  tab, tab, tab],
        out_specs=pl.BlockSpec((1, N, tm), lambda b, i: (b, 0, i)),
        out_shape=jax.ShapeDtypeStruct((Bx, N, S), BF16),
        compiler_params=_params(("parallel", "arbitrary")),
    )(cq, qnorm.reshape(1, R), w_uq_p, *tabs)


def _mla_kv_kernel(ckv_ref, kr_ref, g_ref, wk_ref, wv_ref, c_ref, sm_ref, sp_ref, k_ref, vt_ref, *, nh, hh, tm):
    x = ckv_ref[0]
    xn = (x * lax.rsqrt(jnp.mean(x * x, -1, keepdims=True) + EPS) * g_ref[...]).astype(BF16)
    kn = _dot(xn, wk_ref[...])
    krr = _rope(kr_ref[0], c_ref[...], sm_ref[...], sp_ref[...], hh)
    for h in range(nh):
        k_ref[0, :, h * LANES:(h + 1) * LANES] = (kn[:, h * LANES:(h + 1) * LANES] + krr).astype(k_ref.dtype)
    vt = _dot(xn, wv_ref[...]).T
    pad = MLA_VROWS - MLA_V
    ones_rows = jnp.where(lax.broadcasted_iota(jnp.int32, (pad, tm), 0) == 0, 1.0, 0.0).astype(vt_ref.dtype)
    for h in range(nh):
        vt_ref[0, 0, h * MLA_VROWS:h * MLA_VROWS + MLA_V, :] = vt[h * MLA_V:(h + 1) * MLA_V].astype(vt_ref.dtype)
        vt_ref[0, 0, h * MLA_VROWS + MLA_V:(h + 1) * MLA_VROWS, :] = ones_rows


def _mla_kv(ckv, kr, kvnorm, wk_p, wv, tabs, tm):
    Bx, S, R = ckv.shape
    NK, NV = wk_p.shape[1], wv.shape[1]
    kern = functools.partial(_mla_kv_kernel, nh=MLA_HEADS, hh=MLA_ROPE // 4, tm=tm)
    tab = pl.BlockSpec((tm, LANES), lambda b, i: (i, 0))
    VR = MLA_HEADS * MLA_VROWS
    return pl.pallas_call(
        kern, name="mla_kv", grid=(Bx, S // tm),
        in_specs=[pl.BlockSpec((1, tm, R), lambda b, i: (b, i, 0)),
                  pl.BlockSpec((1, tm, LANES), lambda b, i: (b, i, 0)),
                  _const_spec((1, R)), _const_spec((R, NK)), _const_spec((R, NV)), tab, tab, tab],
        out_specs=[pl.BlockSpec((1, tm, NK), lambda b, i: (b, i, 0)),
                   pl.BlockSpec((1, 1, VR, tm), lambda b, i: (b, i, 0, 0))],
        out_shape=[jax.ShapeDtypeStruct((Bx, S, NK), BF16), jax.ShapeDtypeStruct((Bx, S // tm, VR, tm), BF16)],
        compiler_params=_params(("parallel", "arbitrary")),
    )(ckv, kr, kvnorm.reshape(1, R), wk_p, wv, *tabs)


def _mla_attn_kernel(*refs, tq, tk, n_chunks):
    if n_chunks:
        qt_ref, kc_ref, vct_ref, k_ref, vt_ref, gate_ref, o_ref = refs[:7]
    else:
        qt_ref, kc_ref, vct_ref, gate_ref, o_ref = refs
    VR = MLA_VROWS
    qs = [qt_ref[0, j * LANES:(j + 1) * LANES, :] for j in range(2)]

    def accumulate(m, acc, st, mx, vblk):
        m_new = mx if m is None else jnp.maximum(m, mx)
        pt = jnp.exp2(st - m_new).astype(BF16)
        pv = _dot(vblk, pt)
        acc = pv if m is None else jnp.exp2(m - m_new) * acc + pv
        return m_new, acc

    m, acc = [], []
    for j in range(2):
        st = _dot(kc_ref[0, :, j * LANES:(j + 1) * LANES], qs[j])
        mj, aj = accumulate(None, None, st, jnp.max(st, 0, keepdims=True), vct_ref[0, 0, j * VR:(j + 1) * VR, :])
        m.append(mj)
        acc.append(aj)

    if n_chunks:
        s_scr = refs[7:]

        def scores(j, c, slot):
            st = pl.multiple_of(c * tk, tk)
            blk = _dot(k_ref[0, pl.ds(st, tk), j * LANES:(j + 1) * LANES], qs[j])
            s_scr[2 * j + slot][...] = blk
            return jnp.max(blk, 0, keepdims=True)

        def consume(j, c, slot, mj, aj, mxj):
            return accumulate(mj, aj, s_scr[2 * j + slot][...], mxj, vt_ref[0, c, j * VR:(j + 1) * VR, :])

        mx = [scores(j, 0, 0) for j in range(2)]

        def body(c2, carry):
            m0, a0, x0, m1, a1, x1 = carry
            m, acc, mx = [m0, m1], [a0, a1], [x0, x1]
            c = 2 * c2
            for slot in range(2):
                nxt = [scores(j, c + slot + 1, 1 - slot) for j in range(2)]
                for j in range(2):
                    m[j], acc[j] = consume(j, c + slot, slot, m[j], acc[j], mx[j])
                mx = nxt
            return m[0], acc[0], mx[0], m[1], acc[1], mx[1]

        m0, a0, x0, m1, a1, x1 = lax.fori_loop(0, n_chunks // 2 - 1, body, (m[0], acc[0], mx[0], m[1], acc[1], mx[1]))
        m, acc, mx = [m0, m1], [a0, a1], [x0, x1]
        c = n_chunks - 2
        nxt = [scores(j, c + 1, 1) for j in range(2)]
        for j in range(2):
            m[j], acc[j] = consume(j, c, 0, m[j], acc[j], mx[j])
        for j in range(2):
            m[j], acc[j] = consume(j, c + 1, 1, m[j], acc[j], nxt[j])
    yt = jnp.concatenate([acc[j][0:MLA_V] / acc[j][MLA_V:MLA_V + 1] for j in range(2)], 0)
    o_ref[0] = yt.T * _silu(gate_ref[0])


def _mla_attn(qt, kc, vct, k, vt, gate, tq):
    B, _, S = qt.shape
    Lc = kc.shape[1]
    npair = MLA_HEADS // 2
    VR2 = 2 * MLA_VROWS
    n_chunks, tk = (0, 0) if k is None else (vt.shape[1], vt.shape[3])
    kern = functools.partial(_mla_attn_kernel, tq=tq, tk=tk, n_chunks=n_chunks)
    in_specs = [pl.BlockSpec((1, 2 * LANES, tq), lambda b, p, i: (b, p, i)),
                pl.BlockSpec((1, Lc, 2 * LANES), lambda b, p, i: (b, 0, p)),
                pl.BlockSpec((1, 1, VR2, Lc), lambda b, p, i: (b, 0, p, 0))]
    args = [qt, kc, vct]
    if n_chunks:
        Sk = k.shape[1]
        in_specs += [pl.BlockSpec((1, Sk, 2 * LANES), lambda b, p, i: (b, 0, p)),
                     pl.BlockSpec((1, n_chunks, VR2, tk), lambda b, p, i: (b, 0, p, 0))]
        args += [k, vt]
    in_specs.append(pl.BlockSpec((1, tq, LANES), lambda b, p, i: (b, i, p)))
    args.append(gate)
    return pl.pallas_call(
        kern, name="mla_attn", grid=(B, npair, S // tq),
        in_specs=in_specs,
        out_specs=pl.BlockSpec((1, tq, LANES), lambda b, p, i: (b, i, p)),
        out_shape=jax.ShapeDtypeStruct((B, S, npair * LANES), F32),
        scratch_shapes=[pltpu.VMEM((tk, tq), F32)] * (4 if n_chunks else 0),
        compiler_params=_params(("parallel", "arbitrary", "arbitrary")),
    )(*args)


def _swa_prep_kernel(q_ref, k_ref, v_ref, c_ref, sm_ref, sp_ref, qo_ref, ko_ref, vo_ref, *, scale, hh, nq):
    c, sm, sp = c_ref[...], sm_ref[...], sp_ref[...]
    for r in range(nq):
        qr = _rope(q_ref[0, :, r * LANES:(r + 1) * LANES], c, sm, sp, hh)
        qo_ref[0, :, r * LANES:(r + 1) * LANES] = (qr * scale).astype(qo_ref.dtype)
    ko_ref[0] = _rope(k_ref[0], c, sm, sp, hh).astype(ko_ref.dtype)
    vo_ref[0] = v_ref[0].astype(vo_ref.dtype)


def _swa_prep(sq, sk, sv, tabs, tm):
    Bx, S, NQ = sq.shape
    kern = functools.partial(_swa_prep_kernel, scale=SWA_HD ** -0.5, hh=SWA_HD // 4, nq=NQ // LANES)
    tab = pl.BlockSpec((tm, LANES), lambda b, i: (i, 0))
    qs = pl.BlockSpec((1, tm, NQ), lambda b, i: (b, i, 0))
    ks = pl.BlockSpec((1, tm, LANES), lambda b, i: (b, i, 0))
    return pl.pallas_call(
        kern, name="swa_prep", grid=(Bx, S // tm),
        in_specs=[qs, ks, ks, tab, tab, tab],
        out_specs=[qs, ks, ks],
        out_shape=[jax.ShapeDtypeStruct((Bx, S, NQ), BF16), jax.ShapeDtypeStruct((Bx, S, LANES), BF16),
                   jax.ShapeDtypeStruct((Bx, S, LANES), BF16)],
        compiler_params=_params(("parallel", "arbitrary")),
    )(sq, sk, sv, *tabs)


def _sink_column(sink_ref, g, R, W):
    rid = lax.broadcasted_iota(jnp.int32, (R * W, 1), 0)
    col = jnp.full((R * W, 1), sink_ref[g * R + R - 1], F32)
    for r in range(R - 2, -1, -1):
        col = jnp.where(rid < (r + 1) * W, sink_ref[g * R + r], col)
    return col


def _swa_attn_kernel(sink_ref, q_ref, kp_ref, km_ref, kn_ref, vp_ref, vm_ref, vn_ref, kc_ref, vc_ref, gate_ref,
                     o_ref, *, bpt, W, R, G):
    i = pl.program_id(1)
    nblk = pl.num_programs(1) * bpt
    kcat = jnp.concatenate([kp_ref[0], km_ref[0], kn_ref[0]], 0)
    vcat = jnp.concatenate([vp_ref[0], vm_ref[0], vn_ref[0]], 0)
    kc, vc = kc_ref[0], vc_ref[0]
    lane = lax.broadcasted_iota(jnp.int32, (W, LANES), 1)
    qi = lax.broadcasted_iota(jnp.int32, (R * W, 3 * W), 0) % W
    kj = lax.broadcasted_iota(jnp.int32, (R * W, 3 * W), 1)
    band = jnp.abs(W + qi - kj) <= W
    hd = LANES // G
    for jb in range(bpt):
        gblk = i * bpt + jb
        kw = kcat[jb * W:(jb + 3) * W]
        vw = vcat[jb * W:(jb + 3) * W]
        valid = band & ((kj >= W) | (gblk > 0)) & ((kj < 2 * W) | (gblk < nblk - 1))
        tiles = [q_ref[0, jb * W:(jb + 1) * W, r * LANES:(r + 1) * LANES] for r in range(R)]
        outg = []
        for g in range(G):
            sel = (lane >= g * hd) & (lane < (g + 1) * hd)
            qg = jnp.concatenate([jnp.where(sel, t, jnp.zeros_like(t)) for t in tiles], 0)
            s_w = jnp.where(valid, _dot_nt(qg, kw), NEG)
            s_c = _dot_nt(qg, kc)
            sk = _sink_column(sink_ref, g, R, W)
            m = jnp.maximum(jnp.maximum(jnp.max(s_w, -1, keepdims=True), jnp.max(s_c, -1, keepdims=True)), sk)
            p_w = jnp.exp(s_w - m)
            p_c = jnp.exp(s_c - m)
            den = jnp.exp(sk - m) + jnp.sum(p_c, -1, keepdims=True) + jnp.sum(p_w, -1, keepdims=True)
            o = _dot(p_c.astype(BF16), vc) + _dot(p_w.astype(BF16), vw)
            outg.append(o / den)
        for r in range(R):
            y = outg[G - 1][r * W:(r + 1) * W]
            for g in range(G - 2, -1, -1):
                y = jnp.where(lane < (g + 1) * hd, outg[g][r * W:(r + 1) * W], y)
            gt = gate_ref[0, jb * W:(jb + 1) * W, r * LANES:(r + 1) * LANES]
            o_ref[0, jb * W:(jb + 1) * W, r * LANES:(r + 1) * LANES] = y * _silu(gt)


def _swa_attn(q, k, v, kc, vc, sink, gate, bpt):
    B, S, NQ = q.shape
    Lc = kc.shape[1]
    W = WINDOW
    G = SWA_KV_HEADS
    R = SWA_HEADS // G
    nb = S // W
    T = bpt * W
    kern = functools.partial(_swa_attn_kernel, bpt=bpt, W=W, R=R, G=G)
    main = pl.BlockSpec((1, T, LANES), lambda b, i: (b, i, 0))
    prev = pl.BlockSpec((1, W, LANES), lambda b, i: (b, jnp.maximum(i * bpt - 1, 0), 0))
    nxt = pl.BlockSpec((1, W, LANES), lambda b, i: (b, jnp.minimum((i + 1) * bpt, nb - 1), 0))
    ctx = pl.BlockSpec((1, Lc, LANES), lambda b, i: (b, 0, 0))
    qs = pl.BlockSpec((1, T, NQ), lambda b, i: (b, i, 0))
    return pl.pallas_call(
        kern, name="swa_attn", grid=(B, S // T),
        in_specs=[pl.BlockSpec(memory_space=pltpu.SMEM), qs, prev, main, nxt, prev, main, nxt, ctx, ctx, qs],
        out_specs=qs,
        out_shape=jax.ShapeDtypeStruct((B, S, NQ), F32),
        compiler_params=_params(("parallel", "arbitrary")),
    )(sink, q, k, k, k, v, v, v, kc, vc, gate)


def _swa_ctx_kernel(sink_ref, q_ref, kc_ref, vc_ref, gate_ref, o_ref, *, Lc, R, G, scale):
    kc, vc = kc_ref[0], vc_ref[0]
    lane = lax.broadcasted_iota(jnp.int32, (Lc, LANES), 1)
    hd = LANES // G
    tiles = [(q_ref[0, :, r * LANES:(r + 1) * LANES] * scale).astype(BF16) for r in range(R)]
    outg = []
    for g in range(G):
        sel = (lane >= g * hd) & (lane < (g + 1) * hd)
        qg = jnp.concatenate([jnp.where(sel, t, jnp.zeros_like(t)) for t in tiles], 0)
        s_c = _dot_nt(qg, kc)
        sk = _sink_column(sink_ref, g, R, Lc)
        m = jnp.maximum(jnp.max(s_c, -1, keepdims=True), sk)
        p_c = jnp.exp(s_c - m)
        den = jnp.exp(sk - m) + jnp.sum(p_c, -1, keepdims=True)
        outg.append(_dot(p_c.astype(BF16), vc) / den)
    for r in range(R):
        y = outg[G - 1][r * Lc:(r + 1) * Lc]
        for g in range(G - 2, -1, -1):
            y = jnp.where(lane < (g + 1) * hd, outg[g][r * Lc:(r + 1) * Lc], y)
        o_ref[0, :, r * LANES:(r + 1) * LANES] = y * _silu(gate_ref[0, :, r * LANES:(r + 1) * LANES])


def _swa_ctx(q, kc, vc, sink, gate):
    B, Lc, NQ = q.shape
    G = SWA_KV_HEADS
    R = SWA_HEADS // G
    kern = functools.partial(_swa_ctx_kernel, Lc=Lc, R=R, G=G, scale=SWA_HD ** -0.5)
    qs = pl.BlockSpec((1, Lc, NQ), lambda b: (b, 0, 0))
    ctx = pl.BlockSpec((1, Lc, LANES), lambda b: (b, 0, 0))
    return pl.pallas_call(
        kern, name="swa_ctx", grid=(B,),
        in_specs=[pl.BlockSpec(memory_space=pltpu.SMEM), qs, ctx, ctx, qs],
        out_specs=qs,
        out_shape=jax.ShapeDtypeStruct((B, Lc, NQ), F32),
        compiler_params=_params(("arbitrary",)),
    )(sink, q, kc, vc, gate)


def _merge_kernel(yh_ref, ym_ref, ys_ref, mg_ref, x_ref, mod_ref, wh, wm, ws, wo, lg, lb, o_ref, *, D, alpha):
    m = (jax.nn.sigmoid(mg_ref[0, :, 0:D]) * _dot(yh_ref[0].astype(BF16), wh[...])
         + jax.nn.sigmoid(mg_ref[0, :, D:2 * D]) * _dot(ym_ref[0].astype(BF16), wm[...])
         + jax.nn.sigmoid(mg_ref[0, :, 2 * D:3 * D]) * _dot(ys_ref[0].astype(BF16), ws[...]))
    out = _dot(m.astype(BF16), wo[...])
    r = alpha * x_ref[0] + mod_ref[0, :, 2 * D:3 * D] * out
    mu = jnp.mean(r, -1, keepdims=True)
    rc = r - mu
    var = jnp.mean(rc * rc, -1, keepdims=True)
    o_ref[0] = rc * lax.rsqrt(var + EPS) * lg[...] + lb[...]


def _merge(yh, ym, ys, mg, x, mod, wh, wm, ws, wo, lg, lb, alpha, tm):
    Bx, S, D = x.shape
    kern = functools.partial(_merge_kernel, D=D, alpha=alpha)
    tok = lambda w: pl.BlockSpec((1, tm, w), lambda b, i: (b, i, 0))
    cs = _const_spec
    return pl.pallas_call(
        kern, name="merge", grid=(Bx, S // tm),
        in_specs=[tok(yh.shape[2]), tok(ym.shape[2]), tok(ys.shape[2]), tok(3 * D), tok(D),
                  pl.BlockSpec((1, 1, 3 * D), lambda b, i: (b, 0, 0)),
                  cs(wh.shape), cs(wm.shape), cs(ws.shape), cs(wo.shape), cs((1, D)), cs((1, D))],
        out_specs=tok(D),
        out_shape=jax.ShapeDtypeStruct((Bx, S, D), F32),
        compiler_params=_params(("parallel", "arbitrary")),
    )(yh, ym, ys, mg, x, mod, wh, wm, ws, wo, lg.reshape(1, D), lb.reshape(1, D))


_PAIR_ORDER = (0, 4, 1, 5, 2, 6, 3, 7)

_IN_COLS = (3 * HY_W, HY_W, MLA_Q_RANK, MLA_KV_RANK, MLA_ROPE, MLA_HEADS * MLA_V,
            SWA_HEADS * SWA_HD, SWA_KV_HEADS * SWA_HD, SWA_KV_HEADS * SWA_HD, SWA_HEADS * SWA_HD)
_SLAB_WIDTHS = (None, 3 * HY_W, HY_W, MLA_Q_RANK, MLA_KV_RANK, LANES, MLA_HEADS * MLA_V,
                SWA_HEADS * SWA_HD, LANES, LANES, SWA_HEADS * SWA_HD)


def _pair_cols(w):
    D = w.shape[0]
    return w.reshape(D, SWA_HEADS, SWA_HD)[:, _PAIR_ORDER, :].reshape(D, SWA_HEADS * SWA_HD)


def _layout_w_in(w_in, D):
    offs = [0]
    for cw in _IN_COLS:
        offs.append(offs[-1] + cw)
    sl = [w_in[:, offs[i]:offs[i + 1]] for i in range(len(_IN_COLS))]
    hy, hyg, cq, ckv, kr, mlag, sq, sk, sv, swag = sl
    mg = w_in[:, offs[-1]:]
    kr_p = jnp.pad(kr, ((0, 0), (MLA_NOPE, LANES - MLA_NOPE - MLA_ROPE)))
    return jnp.concatenate([mg, hy, hyg, cq, ckv, kr_p, mlag, _pair_cols(sq), sk, sv, _pair_cols(swag)],
                           1).astype(BF16)


def _layer_weights(p, D):
    w = {}
    w['w_in'] = _layout_w_in(p['w_in'], D)
    uq = p['mla_w_uq'].reshape(MLA_Q_RANK, MLA_HEADS, MLA_QK)
    w['w_uq'] = jnp.pad(uq, ((0, 0), (0, 0), (0, LANES - MLA_QK))).reshape(MLA_Q_RANK, MLA_HEADS * LANES).astype(BF16)
    ukv = p['mla_w_ukv'].reshape(MLA_KV_RANK, MLA_HEADS, MLA_NOPE + MLA_V)
    w['w_uk'] = jnp.pad(ukv[:, :, :MLA_NOPE], ((0, 0), (0, 0), (0, LANES - MLA_NOPE))).reshape(
        MLA_KV_RANK, MLA_HEADS * LANES).astype(BF16)
    w['w_uv'] = ukv[:, :, MLA_NOPE:].reshape(MLA_KV_RANK, MLA_HEADS * MLA_V).astype(BF16)
    w['w_proj_hy'] = p['w_proj_hy'].astype(BF16)
    w['w_proj_mla'] = p['w_proj_mla'].astype(BF16)
    w['w_proj_swa'] = p['w_proj_swa'].reshape(SWA_HEADS, SWA_HD, D)[_PAIR_ORDER, :, :].reshape(
        SWA_HEADS * SWA_HD, D).astype(BF16)
    w['w_out'] = p['w_out'].astype(BF16)
    return w


def _trunk_layer(xl, xc, cvec, p, tabs, alpha, ctx_out):
    B, S, D = xl.shape
    Lc = xc.shape[1]
    w = _layer_weights(p, D)
    widths = (N_BRANCH * D,) + _SLAB_WIDTHS[1:]

    mod = _ada_mod(cvec, p['w_ada'], p['b_ada'])
    mod_l = mod[:B].reshape(B, 1, 3 * D)
    mod_c = jnp.broadcast_to(mod[B].reshape(1, 1, 3 * D), (B, 1, 3 * D))

    tm_c = min(Lc, 256)
    (mg_l, hy_l, hyg_l, cq_l, ckv_l, kr_l, mlag_l, sq_l, sk_l, sv_l, swag_l) = _in_proj(xl, mod_l, w['w_in'], widths, 256)
    (mg_c, hy_c, hyg_c, cq_c, ckv_c, kr_c, mlag_c, sq_c, sk_c, sv_c, swag_c) = _in_proj(xc, mod_c, w['w_in'], widths, tm_c)

    k_mc, v_mc = _mla_kv(ckv_c, kr_c, p['mla_kv_norm'], w['w_uk'], w['w_uv'], tabs['id_c'], tm_c)
    k_sc = sk_c.astype(BF16)
    v_sc = sv_c.astype(BF16)

    u_l, e_l = _hy_pre(hy_l, hyg_l, p['hy_conv_w'], p['hy_conv_b'], 512)
    h2_l = _hy_filters(S, p)
    y_hy = _hy_long_conv(u_l, e_l, h2_l, p['hy_skip'], tabs['dft'])

    q_ml = _mla_q(cq_l, p['mla_q_norm'], w['w_uq'], tabs['mla'], 512)
    k_ml, v_ml = _mla_kv(ckv_l, kr_l, p['mla_kv_norm'], w['w_uk'], w['w_uv'], tabs['mla'], 512)
    y_mla = _mla_attn(q_ml, k_mc, v_mc, k_ml, v_ml, mlag_l, 512)

    q_sl, k_sl, v_sl = _swa_prep(sq_l, sk_l, sv_l, tabs['swa'], 512)
    y_swa = _swa_attn(q_sl, k_sl, v_sl, k_sc, v_sc, p['swa_sink'], swag_l, 4)

    xl_new = _merge(y_hy, y_mla, y_swa, mg_l, xl, mod_l, w['w_proj_hy'], w['w_proj_mla'], w['w_proj_swa'],
                    w['w_out'], p['ln_g'], p['ln_b'], alpha, 256)
    if not ctx_out:
        return xl_new, xc

    u_c, e_c = _hy_pre(hy_c, hyg_c, p['hy_conv_w'], p['hy_conv_b'], tm_c)
    h2_c = _hy_filters(Lc, p)
    yc_hy = _hy_ctx_conv(u_c, e_c, h2_c, p['hy_skip'], tabs['dft_c'])
    q_mc = _mla_q(cq_c, p['mla_q_norm'], w['w_uq'], tabs['id_c'], tm_c)
    yc_mla = _mla_attn(q_mc, k_mc, v_mc, None, None, mlag_c, tm_c)
    yc_swa = _swa_ctx(sq_c, k_sc, v_sc, p['swa_sink'], swag_c)
    xc_new = _merge(yc_hy, yc_mla, yc_swa, mg_c, xc, mod_c, w['w_proj_hy'], w['w_proj_mla'], w['w_proj_swa'],
                    w['w_out'], p['ln_g'], p['ln_b'], alpha, tm_c)
    return xl_new, xc_new


def kernel(x, c, ctx, c_ctx, w_ada, b_ada, w_in, hy_conv_w, hy_conv_b, filt_w1, filt_b1, filt_w2, filt_b2,
           filt_w3, filt_b3, filt_freq, filt_w_out, hy_skip, mla_q_norm, mla_w_uq, mla_kv_norm, mla_w_ukv,
           swa_sink, w_proj_hy, w_proj_mla, w_proj_swa, w_out, ln_g, ln_b):
    B, S, D = x.shape
    Lc = ctx.shape[1]
    depth = w_in.shape[0]
    alpha = (2 * depth) ** 0.25
    stacked = dict(w_ada=w_ada, b_ada=b_ada, w_in=w_in, hy_conv_w=hy_conv_w, hy_conv_b=hy_conv_b,
                   filt_w1=filt_w1, filt_b1=filt_b1, filt_w2=filt_w2, filt_b2=filt_b2, filt_w3=filt_w3,
                   filt_b3=filt_b3, filt_freq=filt_freq, filt_w_out=filt_w_out, hy_skip=hy_skip,
                   mla_q_norm=mla_q_norm, mla_w_uq=mla_w_uq, mla_kv_norm=mla_kv_norm, mla_w_ukv=mla_w_ukv,
                   swa_sink=swa_sink, w_proj_hy=w_proj_hy, w_proj_mla=w_proj_mla, w_proj_swa=w_proj_swa,
                   w_out=w_out, ln_g=ln_g, ln_b=ln_b)

    lane = jnp.arange(LANES, dtype=jnp.int32)
    mla_on = (lane >= MLA_NOPE) & (lane < MLA_QK)
    tabs = {
        'mla': _rope_tables(S, jnp.clip(lane - MLA_NOPE, 0, MLA_ROPE - 1), mla_on, MLA_ROPE),
        'swa': _rope_tables(S, lane % SWA_HD, jnp.ones((LANES,), bool), SWA_HD),
        'id_c': _rope_tables(Lc, None, None, None, identity=True),
        'dft': _dft_tables(S),
        'dft_c': _ctx_dft_tables(Lc),
    }
    cvec = jnp.concatenate([c, c_ctx[None, :], jnp.zeros((8 - B - 1, D), F32)], 0)

    xl, xc = x, ctx
    for l in range(depth):
        p = {k: v[l] for k, v in stacked.items()}
        xl, xc = _trunk_layer(xl, xc, cvec, p, tabs, alpha, l < depth - 1)
    return xl
```

```python
import functools
import math

import jax
import jax.numpy as jnp
from jax import lax
from jax.experimental import pallas as pl
from jax.experimental.pallas import tpu as pltpu

F32 = jnp.float32
BF16 = jnp.bfloat16
HIGHEST = lax.Precision.HIGHEST

GRID_W = 64
HY_W = 512
SHORT_K = 3
FILT_BANDS = 16
FILT_W = 64
DECAY_TARGET = 1e-2
FAST_DECAY_PCT = 0.3
SLOW_DECAY_PCT = 1.5
MLA_HEADS = 8
MLA_NOPE = 64
MLA_ROPE = 32
MLA_V = 64
MLA_QK = MLA_NOPE + MLA_ROPE
MLA_Q_RANK = 256
MLA_KV_RANK = 128
SWA_HEADS = 8
SWA_KV_HEADS = 2
SWA_HD = 64
WINDOW = 128
N_BRANCH = 3
ROPE_BASE = 10000.0
EPS = 1e-6
NEG = -1e30
LOG2E = math.log2(math.e)

LANES = 128
DFT_N2 = 128
VMEM_LIMIT = 56 * 1024 * 1024


def _silu(x):
    return x * jax.nn.sigmoid(x)


def _dot(a, b):
    return jnp.dot(a, b, preferred_element_type=F32)


def _dot_nt(a, b):
    return lax.dot_general(a, b, (((1,), (1,)), ((), ())), preferred_element_type=F32)


def _params(sem):
    return pltpu.CompilerParams(dimension_semantics=sem, vmem_limit_bytes=VMEM_LIMIT)


def _const_spec(shape):
    nd = len(shape)
    return pl.BlockSpec(shape, lambda *_: (0,) * nd)


def _ada_kernel(c_ref, w_ref, b_ref, o_ref):
    a = _silu(c_ref[...])
    o_ref[...] = jnp.dot(a, w_ref[...], precision=HIGHEST, preferred_element_type=F32) + b_ref[...]


def _ada_mod(cvec, w_ada, b_ada):
    R, D = cvec.shape
    N = w_ada.shape[1]
    tn = 768
    return pl.pallas_call(
        _ada_kernel, name="ada_mod", grid=(N // tn,),
        in_specs=[_const_spec((R, D)), pl.BlockSpec((D, tn), lambda j: (0, j)),
                  pl.BlockSpec((1, tn), lambda j: (0, j))],
        out_specs=pl.BlockSpec((R, tn), lambda j: (0, j)),
        out_shape=jax.ShapeDtypeStruct((R, N), F32),
        compiler_params=_params(("arbitrary",)),
    )(cvec, w_ada, b_ada.reshape(1, N))


def _in_proj_kernel(x_ref, mod_ref, w_ref, *o_refs, widths, D):
    x = x_ref[0]
    mu = jnp.mean(x, -1, keepdims=True)
    xc = x - mu
    var = jnp.mean(xc * xc, -1, keepdims=True)
    ln = xc * lax.rsqrt(var + EPS)
    shift = mod_ref[0, :, 0:D]
    scale = mod_ref[0, :, D:2 * D]
    u = (ln * (1.0 + scale) + shift).astype(BF16)
    off = 0
    for o_ref, wd in zip(o_refs, widths):
        o_ref[0] = _dot(u, w_ref[:, off:off + wd]).astype(o_ref.dtype)
        off += wd


def _in_proj(x, mod, w, widths, tm):
    Bx, S, D = x.shape
    P = w.shape[1]
    kern = functools.partial(_in_proj_kernel, widths=tuple(widths), D=D)
    return pl.pallas_call(
        kern, name="in_proj", grid=(Bx, S // tm),
        in_specs=[pl.BlockSpec((1, tm, D), lambda b, i: (b, i, 0)),
                  pl.BlockSpec((1, 1, 3 * D), lambda b, i: (b, 0, 0)),
                  pl.BlockSpec((D, P), lambda b, i: (0, 0), pipeline_mode=pl.Buffered(1))],
        out_specs=[pl.BlockSpec((1, tm, wd), lambda b, i: (b, i, 0)) for wd in widths],
        out_shape=[jax.ShapeDtypeStruct((Bx, S, wd), F32) for wd in widths],
        compiler_params=_params(("parallel", "arbitrary")),
    )(x, mod, w)


def _filter_kernel(z_ref, t_ref, w1, b1, w2, b2, w3, b3, fr, wo, dl_ref, o_ref, *, tl, C):
    hp = lambda a, b: jnp.dot(a, b, precision=HIGHEST, preferred_element_type=F32)
    f = fr[...]
    h = jnp.sin(f * (hp(z_ref[...], w1[...]) + b1[...]))
    h = jnp.sin(f * (hp(h, w2[...]) + b2[...]))
    h = jnp.sin(f * (hp(h, w3[...]) + b3[...]))
    dl = jnp.abs(dl_ref[...])
    half = tl // 2
    for s in range(2):
        rows = slice(s * half, (s + 1) * half)
        o = hp(h, wo[s])
        decay = jnp.exp(-t_ref[rows, :] * dl)
        o_ref[rows, 0:C] = o[:, 0:C] * decay
        bwd = o[:, C:2 * C] * decay
        if s == 0:
            row = pl.program_id(0) * tl + lax.broadcasted_iota(jnp.int32, (half, 1), 0)
            bwd = jnp.where(row == 0, 0.0, bwd)
        o_ref[rows, C:2 * C] = bwd


def _hy_filters(L, p):
    C = HY_W
    FP = LANES
    FH = FP // 2
    t = jnp.linspace(0.0, 1.0, L, dtype=F32)[:, None]
    w = 2.0 * math.pi * jnp.arange(L, dtype=F32) / L
    f = jnp.linspace(1e-4, FILT_BANDS - 1, FILT_BANDS, dtype=F32)
    ang = w[:, None] * f[None, :]
    z = jnp.concatenate([t, jnp.cos(ang), -jnp.sin(ang)], -1)
    z = jnp.pad(z, ((0, 0), (0, FH - z.shape[1])))
    tl = min(L, 512)
    z = z.reshape(L // tl, 2, tl // 2, FH).transpose(0, 2, 1, 3).reshape(L // 2, FP)
    pad_h = lambda a: jnp.pad(a, ((0, FH - a.shape[0]), (0, FH - a.shape[1])))
    bdiag = lambda a: jnp.kron(jnp.eye(2, dtype=F32), pad_h(a))
    pad_v = lambda a: jnp.tile(jnp.pad(a, (0, FH - a.shape[0])), 2).reshape(1, FP)
    w1, w2, w3 = bdiag(p['filt_w1']), bdiag(p['filt_w2']), bdiag(p['filt_w3'])
    wo_h = jnp.pad(p['filt_w_out'], ((0, FH - FILT_W), (0, 0)))
    zeros = jnp.zeros_like(wo_h)
    wo = jnp.stack([jnp.concatenate([wo_h, zeros], 0), jnp.concatenate([zeros, wo_h], 0)], 0)
    min_decay = math.log(DECAY_TARGET) / SLOW_DECAY_PCT
    max_decay = math.log(DECAY_TARGET) / FAST_DECAY_PCT
    deltas = jnp.linspace(min_decay, max_decay, C, dtype=F32).reshape(1, C)
    kern = functools.partial(_filter_kernel, tl=tl, C=C)
    cs = _const_spec
    return pl.pallas_call(
        kern, name="hy_filter", grid=(L // tl,),
        in_specs=[pl.BlockSpec((tl // 2, FP), lambda i: (i, 0)), pl.BlockSpec((tl, 1), lambda i: (i, 0)),
                  cs((FP, FP)), cs((1, FP)), cs((FP, FP)), cs((1, FP)), cs((FP, FP)), cs((1, FP)),
                  cs((1, FP)), cs((2, FP, 2 * C)), cs((1, C))],
        out_specs=pl.BlockSpec((tl, 2 * C), lambda i: (i, 0)),
        out_shape=jax.ShapeDtypeStruct((L, 2 * C), F32),
        compiler_params=_params(("arbitrary",)),
    )(z, t, w1, pad_v(p['filt_b1']), w2, pad_v(p['filt_b2']), w3, pad_v(p['filt_b3']),
      pad_v(p['filt_freq']), wo, deltas)


def _hy_pre_kernel(x_ref, xp_ref, xn_ref, g_ref, w_ref, b_ref, u_ref, e_ref, *, ts, C):
    i = pl.program_id(1)
    nt = pl.num_programs(1)
    x = x_ref[0]
    prev_row = xp_ref[0, 7:8, :] * jnp.where(i > 0, 1.0, 0.0)
    next_row = xn_ref[0, 0:1, :] * jnp.where(i < nt - 1, 1.0, 0.0)
    rid = lax.broadcasted_iota(jnp.int32, (ts, 1), 0)
    xm = jnp.where(rid == 0, prev_row, pltpu.roll(x, 1, 0))
    xq = jnp.where(rid == ts - 1, next_row, pltpu.roll(x, ts - 1, 0))
    z = b_ref[...] + xm * w_ref[0:1, :] + x * w_ref[1:2, :] + xq * w_ref[2:3, :]
    u_ref[0] = z[:, 2 * C:3 * C] * z[:, C:2 * C]
    e_ref[0] = z[:, 0:C] * _silu(g_ref[0])


def _hy_pre(hy, hyg, conv_w, conv_b, ts):
    Bx, S, C3 = hy.shape
    C = C3 // 3
    nb8 = S // 8
    r = ts // 8
    w8 = jnp.pad(conv_w, ((0, 8 - SHORT_K), (0, 0)))
    kern = functools.partial(_hy_pre_kernel, ts=ts, C=C)
    return pl.pallas_call(
        kern, name="hy_pre", grid=(Bx, S // ts),
        in_specs=[pl.BlockSpec((1, ts, C3), lambda b, i: (b, i, 0)),
                  pl.BlockSpec((1, 8, C3), lambda b, i: (b, jnp.maximum(i * r - 1, 0), 0)),
                  pl.BlockSpec((1, 8, C3), lambda b, i: (b, jnp.minimum((i + 1) * r, nb8 - 1), 0)),
                  pl.BlockSpec((1, ts, C), lambda b, i: (b, i, 0)),
                  _const_spec((8, C3)), _const_spec((1, C3))],
        out_specs=[pl.BlockSpec((1, ts, C), lambda b, i: (b, i, 0))] * 2,
        out_shape=[jax.ShapeDtypeStruct((Bx, S, C), F32)] * 2,
        compiler_params=_params(("parallel", "arbitrary")),
    )(hy, hy, hy, hyg, w8, conv_b.reshape(1, C3))


def _dft_tables(L):
    n = 2 * L
    N2 = DFT_N2
    N1 = n // N2
    H1 = N1 // 2
    k1 = jnp.arange(N1, dtype=jnp.int32)
    t1 = jnp.arange(H1, dtype=jnp.int32)
    ang = (2.0 * math.pi / N1) * ((k1[:, None] * t1[None, :]) % N1).astype(F32)
    fa = jnp.stack([jnp.cos(ang), -jnp.sin(ang)], 1).reshape(2 * N1, H1)
    fai = jnp.concatenate([jnp.cos(ang), -jnp.sin(ang)], 0).T * (1.0 / n)
    k2 = jnp.arange(N2, dtype=jnp.int32)
    t2 = jnp.arange(N2, dtype=jnp.int32)
    m = (t2[None, None, :] * (k2[None, :, None] * N1 + k1[:, None, None])) % n
    th = (2.0 * math.pi / n) * m.astype(F32)
    cr, ci = jnp.cos(th), -jnp.sin(th)
    mf = jnp.concatenate([jnp.concatenate([cr, -ci], 2), jnp.concatenate([ci, cr], 2)], 1)
    mi = jnp.swapaxes(mf, 1, 2)
    return fa.astype(BF16), fai.astype(BF16), mf.astype(BF16), mi.astype(BF16)


DFT_TT = 8
DFT_KB = 4


def _dft_a_fwd_kernel(x_ref, f_ref, o_ref):
    f = f_ref[...]
    for j in range(DFT_TT):
        o_ref[0, j] = _dot(f, x_ref[0, :, j, :].astype(BF16))


def _dft_a_fwd(x4, fa):
    Bx, H1, N2, C = x4.shape
    R = fa.shape[0]
    return pl.pallas_call(
        _dft_a_fwd_kernel, name="dft_a_fwd", grid=(Bx, N2 // DFT_TT),
        in_specs=[pl.BlockSpec((1, H1, DFT_TT, C), lambda b, j: (b, 0, j, 0)), _const_spec((R, H1))],
        out_specs=pl.BlockSpec((1, DFT_TT, R, C), lambda b, j: (b, j, 0, 0)),
        out_shape=jax.ShapeDtypeStruct((Bx, N2, R, C), F32),
        compiler_params=_params(("parallel", "arbitrary")),
    )(x4, fa)


def _stage_b_operand(a_ref, b, j):
    return jnp.concatenate([a_ref[b, :, 2 * j, :], a_ref[b, :, 2 * j + 1, :]], 0).astype(BF16)


def _dft_b_filter_kernel(a_ref, m_ref, k_ref, *, kb, C, N2):
    for j in range(kb):
        h = _dot(m_ref[j], _stage_b_operand(a_ref, 0, j))
        k_ref[j, 0:N2, :] = h[0:N2, 0:C] + h[0:N2, C:2 * C]
        k_ref[j, N2:2 * N2, :] = h[N2:2 * N2, 0:C] - h[N2:2 * N2, C:2 * C]


def _dft_b_filter(ah, mf):
    _, N2, R, C2 = ah.shape
    N1, C, kb = R // 2, C2 // 2, DFT_KB
    kern = functools.partial(_dft_b_filter_kernel, kb=kb, C=C, N2=N2)
    return pl.pallas_call(
        kern, name="dft_b_filter", grid=(N1 // kb,),
        in_specs=[pl.BlockSpec((1, N2, 2 * kb, C2), lambda i: (0, 0, i, 0)),
                  pl.BlockSpec((kb, 2 * N2, 2 * N2), lambda i: (i, 0, 0))],
        out_specs=pl.BlockSpec((kb, 2 * N2, C), lambda i: (i, 0, 0)),
        out_shape=jax.ShapeDtypeStruct((N1, 2 * N2, C), F32),
        compiler_params=_params(("arbitrary",)),
    )(ah, mf)


def _dft_mid_kernel(a_ref, mf_ref, k_ref, mi_ref, g_ref, *, kb, nb, C, N2):
    for j in range(kb):
        kre = k_ref[j, 0:N2, :]
        kim = k_ref[j, N2:2 * N2, :]
        for b in range(nb):
            y = _dot(mf_ref[j], _stage_b_operand(a_ref, b, j))
            yre, yim = y[0:N2], y[N2:2 * N2]
            z = jnp.concatenate([yre * kre - yim * kim, yre * kim + yim * kre], 0).astype(BF16)
            g_ref[b, j] = _dot(mi_ref[j], z)


def _dft_mid(a, mf, kf, mi):
    Bx, N2, R, C = a.shape
    N1, kb = R // 2, DFT_KB
    kern = functools.partial(_dft_mid_kernel, kb=kb, nb=Bx, C=C, N2=N2)
    return pl.pallas_call(
        kern, name="dft_mid", grid=(N1 // kb,),
        in_specs=[pl.BlockSpec((Bx, N2, 2 * kb, C), lambda i: (0, 0, i, 0)),
                  pl.BlockSpec((kb, 2 * N2, 2 * N2), lambda i: (i, 0, 0)),
                  pl.BlockSpec((kb, 2 * N2, C), lambda i: (i, 0, 0)),
                  pl.BlockSpec((kb, 2 * N2, 2 * N2), lambda i: (i, 0, 0))],
        out_specs=pl.BlockSpec((Bx, kb, 2 * N2, C), lambda i: (0, i, 0, 0)),
        out_shape=jax.ShapeDtypeStruct((Bx, N1, 2 * N2, C), F32),
        compiler_params=_params(("arbitrary",)),
    )(a, mf, kf, mi)


def _dft_a_inv_kernel(g_ref, f_ref, u_ref, e_ref, s_ref, o_ref):
    f = f_ref[...]
    for j in range(DFT_TT):
        g = jnp.concatenate([g_ref[0, :, 0, j, :], g_ref[0, :, 1, j, :]], 0).astype(BF16)
        y = _dot(f, g)
        o_ref[0, :, j, :] = (y + u_ref[0, :, j, :] * s_ref[...]) * e_ref[0, :, j, :]


def _dft_a_inv(g5, fai, u4, e4, skip):
    Bx, N1, _, N2, C = g5.shape
    H1 = fai.shape[0]
    tok = pl.BlockSpec((1, H1, DFT_TT, C), lambda b, j: (b, 0, j, 0))
    return pl.pallas_call(
        _dft_a_inv_kernel, name="dft_a_inv", grid=(Bx, N2 // DFT_TT),
        in_specs=[pl.BlockSpec((1, N1, 2, DFT_TT, C), lambda b, j: (b, 0, 0, j, 0)), _const_spec((H1, 2 * N1)),
                  tok, tok, _const_spec((1, C))],
        out_specs=tok,
        out_shape=jax.ShapeDtypeStruct((Bx, H1, N2, C), F32),
        compiler_params=_params(("parallel", "arbitrary")),
    )(g5, fai, u4, e4, skip)


def _hy_long_conv(u, e, h2, skip, tabs):
    B, L, C = u.shape
    fa, fai, mf, mi = tabs
    N2 = DFT_N2
    N1 = 2 * L // N2
    H1 = N1 // 2
    kf = _dft_b_filter(_dft_a_fwd(h2.reshape(1, H1, N2, 2 * C), fa), mf)
    u4 = u.reshape(B, H1, N2, C)
    g = _dft_mid(_dft_a_fwd(u4, fa), mf, kf, mi)
    y = _dft_a_inv(g.reshape(B, N1, 2, N2, C), fai, u4, e.reshape(B, H1, N2, C), skip.reshape(1, C))
    return y.reshape(B, L, C)


def _ctx_dft_tables(Lc):
    n = 2 * Lc
    k = jnp.arange(n, dtype=jnp.int32)
    t = jnp.arange(Lc, dtype=jnp.int32)
    ang = (2.0 * math.pi / n) * ((k[:, None] * t[None, :]) % n).astype(F32)
    fc = jnp.concatenate([jnp.cos(ang), -jnp.sin(ang)], 0)
    fi = fc.T * (1.0 / n)
    return fc.astype(BF16), fi.astype(BF16)


def _hy_ctx_conv_kernel(u_ref, e_ref, h_ref, fc_ref, fi_ref, s_ref, o_ref, *, n, C):
    u = u_ref[0]
    fc = fc_ref[...]
    uf = _dot(fc, u.astype(BF16))
    hf = _dot(fc, h_ref[...].astype(BF16))
    kre = hf[0:n, 0:C] + hf[0:n, C:2 * C]
    kim = hf[n:2 * n, 0:C] - hf[n:2 * n, C:2 * C]
    ure, uim = uf[0:n], uf[n:2 * n]
    z = jnp.concatenate([ure * kre - uim * kim, ure * kim + uim * kre], 0).astype(BF16)
    y = _dot(fi_ref[...], z)
    o_ref[0] = (y + u * s_ref[...]) * e_ref[0]


def _hy_ctx_conv(u, e, h2, skip, tabs):
    B, Lc, C = u.shape
    fc, fi = tabs
    n = 2 * Lc
    kern = functools.partial(_hy_ctx_conv_kernel, n=n, C=C)
    blk = pl.BlockSpec((1, Lc, C), lambda b: (b, 0, 0))
    return pl.pallas_call(
        kern, name="hy_ctx_conv", grid=(B,),
        in_specs=[blk, blk, _const_spec((Lc, 2 * C)), _const_spec((2 * n, Lc)), _const_spec((Lc, 2 * n)),
                  _const_spec((1, C))],
        out_specs=blk,
        out_shape=jax.ShapeDtypeStruct((B, Lc, C), F32),
        compiler_params=_params(("arbitrary",)),
    )(u, e, h2, fc, fi, skip.reshape(1, C))


def _rope_tables(S, lane_dim, lane_on, head_rot, identity=False):
    if identity:
        return (jnp.ones((S, LANES), F32), jnp.zeros((S, LANES), F32), jnp.zeros((S, LANES), F32))
    seg_w = head_rot // 2
    half = seg_w // 2
    seg = lane_dim // seg_w
    w = lane_dim % seg_w
    first = w < half
    inv = ROPE_BASE ** (-(w % half).astype(F32) / half)
    nrow = S // GRID_W
    ang_r = jnp.arange(nrow, dtype=jnp.int32).astype(F32)[:, None] * inv[None, :]
    ang_c = jnp.arange(GRID_W, dtype=jnp.int32).astype(F32)[:, None] * inv[None, :]

    def expand(fn):
        tab = jnp.where((seg == 0)[None, None, :], fn(ang_r)[:, None, :], fn(ang_c)[None, :, :])
        return tab.reshape(S, LANES)

    cos, sin = expand(jnp.cos), expand(jnp.sin)
    on = lane_on[None, :]
    c = jnp.where(on, cos, 1.0)
    sm = jnp.where(on & first[None, :], -sin, 0.0)
    sp = jnp.where(on & (~first)[None, :], sin, 0.0)
    return c, sm, sp


def _rope(x, c, sm, sp, hh):
    return x * c + pltpu.roll(x, LANES - hh, 1) * sm + pltpu.roll(x, hh, 1) * sp


MLA_VROWS = 80

def _mla_q_kernel(cq_ref, g_ref, w_ref, c_ref, sm_ref, sp_ref, o_ref, *, nh, scale, hh):
    x = cq_ref[0]
    xn = x * lax.rsqrt(jnp.mean(x * x, -1, keepdims=True) + EPS) * g_ref[...]
    q = _dot(xn.astype(BF16), w_ref[...])
    c, sm, sp = c_ref[...], sm_ref[...], sp_ref[...]
    for h in range(nh):
        qh = _rope(q[:, h * LANES:(h + 1) * LANES], c, sm, sp, hh) * scale
        o_ref[0, h * LANES:(h + 1) * LANES, :] = qh.T.astype(o_ref.dtype)


def _mla_q(cq, qnorm, w_uq_p, tabs, tm):
    Bx, S, R = cq.shape
    N = w_uq_p.shape[1]
    kern = functools.partial(_mla_q_kernel, nh=MLA_HEADS, scale=MLA_QK ** -0.5 * LOG2E,
                             hh=MLA_ROPE // 4)
    tab = pl.BlockSpec((tm, LANES), lambda b, i: (i, 0))
    return pl.pallas_call(
        kern, name="mla_q", grid=(Bx, S // tm),
        in_specs=[pl.BlockSpec((1, tm, R), lambda b, i: (b, i, 0)), _const_spec((1, R)), _const_spec((R, N)),
                  tab, tab, tab],
        out_specs=pl.BlockSpec((1, N, tm), lambda b, i: (b, 0, i)),
        out_shape=jax.ShapeDtypeStruct((Bx, N, S), BF16),
        compiler_params=_params(("parallel", "arbitrary")),
    )(cq, qnorm.reshape(1, R), w_uq_p, *tabs)


def _mla_kv_kernel(ckv_ref, kr_ref, g_ref, wk_ref, wv_ref, c_ref, sm_ref, sp_ref, k_ref, vt_ref, *, nh, hh, tm):
    x = ckv_ref[0]
    xn = (x * lax.rsqrt(jnp.mean(x * x, -1, keepdims=True) + EPS) * g_ref[...]).astype(BF16)
    kn = _dot(xn, wk_ref[...])
    krr = _rope(kr_ref[0], c_ref[...], sm_ref[...], sp_ref[...], hh)
    for h in range(nh):
        k_ref[0, :, h * LANES:(h + 1) * LANES] = (kn[:, h * LANES:(h + 1) * LANES] + krr).astype(k_ref.dtype)
    vt = _dot(xn, wv_ref[...]).T
    pad = MLA_VROWS - MLA_V
    ones_rows = jnp.where(lax.broadcasted_iota(jnp.int32, (pad, tm), 0) == 0, 1.0, 0.0).astype(vt_ref.dtype)
    for h in range(nh):
        vt_ref[0, 0, h * MLA_VROWS:h * MLA_VROWS + MLA_V, :] = vt[h * MLA_V:(h + 1) * MLA_V].astype(vt_ref.dtype)
        vt_ref[0, 0, h * MLA_VROWS + MLA_V:(h + 1) * MLA_VROWS, :] = ones_rows


def _mla_kv(ckv, kr, kvnorm, wk_p, wv, tabs, tm):
    Bx, S, R = ckv.shape
    NK, NV = wk_p.shape[1], wv.shape[1]
    kern = functools.partial(_mla_kv_kernel, nh=MLA_HEADS, hh=MLA_ROPE // 4, tm=tm)
    tab = pl.BlockSpec((tm, LANES), lambda b, i: (i, 0))
    VR = MLA_HEADS * MLA_VROWS
    return pl.pallas_call(
        kern, name="mla_kv", grid=(Bx, S // tm),
        in_specs=[pl.BlockSpec((1, tm, R), lambda b, i: (b, i, 0)),
                  pl.BlockSpec((1, tm, LANES), lambda b, i: (b, i, 0)),
                  _const_spec((1, R)), _const_spec((R, NK)), _const_spec((R, NV)), tab, tab, tab],
        out_specs=[pl.BlockSpec((1, tm, NK), lambda b, i: (b, i, 0)),
                   pl.BlockSpec((1, 1, VR, tm), lambda b, i: (b, i, 0, 0))],
        out_shape=[jax.ShapeDtypeStruct((Bx, S, NK), BF16), jax.ShapeDtypeStruct((Bx, S // tm, VR, tm), BF16)],
        compiler_params=_params(("parallel", "arbitrary")),
    )(ckv, kr, kvnorm.reshape(1, R), wk_p, wv, *tabs)


def _mla_attn_kernel(*refs, tq, tk, n_chunks):
    if n_chunks:
        qt_ref, kc_ref, vct_ref, k_ref, vt_ref, gate_ref, o_ref = refs[:7]
    else:
        qt_ref, kc_ref, vct_ref, gate_ref, o_ref = refs
    VR = MLA_VROWS
    qs = [qt_ref[0, j * LANES:(j + 1) * LANES, :] for j in range(2)]

    def accumulate(m, acc, st, mx, vblk):
        m_new = mx if m is None else jnp.maximum(m, mx)
        pt = jnp.exp2(st - m_new).astype(BF16)
        pv = _dot(vblk, pt)
        acc = pv if m is None else jnp.exp2(m - m_new) * acc + pv
        return m_new, acc

    m, acc = [], []
    for j in range(2):
        st = _dot(kc_ref[0, :, j * LANES:(j + 1) * LANES], qs[j])
        mj, aj = accumulate(None, None, st, jnp.max(st, 0, keepdims=True), vct_ref[0, 0, j * VR:(j + 1) * VR, :])
        m.append(mj)
        acc.append(aj)

    if n_chunks:
        s_scr = refs[7:]

        def scores(j, c, slot):
            st = pl.multiple_of(c * tk, tk)
            blk = _dot(k_ref[0, pl.ds(st, tk), j * LANES:(j + 1) * LANES], qs[j])
            s_scr[2 * j + slot][...] = blk
            return jnp.max(blk, 0, keepdims=True)

        def consume(j, c, slot, mj, aj, mxj):
            return accumulate(mj, aj, s_scr[2 * j + slot][...], mxj, vt_ref[0, c, j * VR:(j + 1) * VR, :])

        mx = [scores(j, 0, 0) for j in range(2)]

        unroll = 4 if n_chunks % 4 == 0 else 2

        def steps(c, m, acc, mx, last):
            for s in range(unroll):
                slot = s % 2
                nxt = mx
                if not (last and s == unroll - 1):
                    nxt = [scores(j, c + s + 1, 1 - slot) for j in range(2)]
                for j in range(2):
                    m[j], acc[j] = consume(j, c + s, slot, m[j], acc[j], mx[j])
                mx = nxt
            return m, acc, mx

        def body(ci, carry):
            m0, a0, x0, m1, a1, x1 = carry
            m, acc, mx = steps(unroll * ci, [m0, m1], [a0, a1], [x0, x1], False)
            return m[0], acc[0], mx[0], m[1], acc[1], mx[1]

        m0, a0, x0, m1, a1, x1 = lax.fori_loop(0, n_chunks // unroll - 1, body,
                                               (m[0], acc[0], mx[0], m[1], acc[1], mx[1]))
        m, acc, _ = steps(n_chunks - unroll, [m0, m1], [a0, a1], [x0, x1], True)
    yt = jnp.concatenate([acc[j][0:MLA_V] / acc[j][MLA_V:MLA_V + 1] for j in range(2)], 0)
    o_ref[0] = yt.T * _silu(gate_ref[0])


def _mla_attn(qt, kc, vct, k, vt, gate, tq):
    B, _, S = qt.shape
    Lc = kc.shape[1]
    npair = MLA_HEADS // 2
    VR2 = 2 * MLA_VROWS
    n_chunks, tk = (0, 0) if k is None else (vt.shape[1], vt.shape[3])
    kern = functools.partial(_mla_attn_kernel, tq=tq, tk=tk, n_chunks=n_chunks)
    in_specs = [pl.BlockSpec((1, 2 * LANES, tq), lambda b, p, i: (b, p, i)),
                pl.BlockSpec((1, Lc, 2 * LANES), lambda b, p, i: (b, 0, p)),
                pl.BlockSpec((1, 1, VR2, Lc), lambda b, p, i: (b, 0, p, 0))]
    args = [qt, kc, vct]
    if n_chunks:
        Sk = k.shape[1]
        in_specs += [pl.BlockSpec((1, Sk, 2 * LANES), lambda b, p, i: (b, 0, p)),
                     pl.BlockSpec((1, n_chunks, VR2, tk), lambda b, p, i: (b, 0, p, 0))]
        args += [k, vt]
    in_specs.append(pl.BlockSpec((1, tq, LANES), lambda b, p, i: (b, i, p)))
    args.append(gate)
    return pl.pallas_call(
        kern, name="mla_attn", grid=(B, npair, S // tq),
        in_specs=in_specs,
        out_specs=pl.BlockSpec((1, tq, LANES), lambda b, p, i: (b, i, p)),
        out_shape=jax.ShapeDtypeStruct((B, S, npair * LANES), F32),
        scratch_shapes=[pltpu.VMEM((tk, tq), F32)] * (4 if n_chunks else 0),
        compiler_params=_params(("parallel", "arbitrary", "arbitrary")),
    )(*args)


def _swa_prep_kernel(q_ref, k_ref, v_ref, c_ref, sm_ref, sp_ref, qo_ref, ko_ref, vo_ref, *, scale, hh, nq):
    c, sm, sp = c_ref[...], sm_ref[...], sp_ref[...]
    for r in range(nq):
        qr = _rope(q_ref[0, :, r * LANES:(r + 1) * LANES], c, sm, sp, hh)
        qo_ref[0, :, r * LANES:(r + 1) * LANES] = (qr * scale).astype(qo_ref.dtype)
    ko_ref[0] = _rope(k_ref[0], c, sm, sp, hh).astype(ko_ref.dtype)
    vo_ref[0] = v_ref[0].astype(vo_ref.dtype)


def _swa_prep(sq, sk, sv, tabs, tm):
    Bx, S, NQ = sq.shape
    kern = functools.partial(_swa_prep_kernel, scale=SWA_HD ** -0.5 * LOG2E, hh=SWA_HD // 4, nq=NQ // LANES)
    tab = pl.BlockSpec((tm, LANES), lambda b, i: (i, 0))
    qs = pl.BlockSpec((1, tm, NQ), lambda b, i: (b, i, 0))
    ks = pl.BlockSpec((1, tm, LANES), lambda b, i: (b, i, 0))
    return pl.pallas_call(
        kern, name="swa_prep", grid=(Bx, S // tm),
        in_specs=[qs, ks, ks, tab, tab, tab],
        out_specs=[qs, ks, ks],
        out_shape=[jax.ShapeDtypeStruct((Bx, S, NQ), BF16), jax.ShapeDtypeStruct((Bx, S, LANES), BF16),
                   jax.ShapeDtypeStruct((Bx, S, LANES), BF16)],
        compiler_params=_params(("parallel", "arbitrary")),
    )(sq, sk, sv, *tabs)


def _sink_column(sink_ref, g, R, W):
    rid = lax.broadcasted_iota(jnp.int32, (R * W, 1), 0)
    col = jnp.full((R * W, 1), sink_ref[g * R + R - 1], F32)
    for r in range(R - 2, -1, -1):
        col = jnp.where(rid < (r + 1) * W, sink_ref[g * R + r], col)
    return col


def _swa_group(qg, sk2, blocks):
    ss = []
    for kk, _, bias in blocks:
        s = _dot_nt(qg, kk)
        ss.append(s if bias is None else s + bias)
    m = sk2
    for s in ss:
        m = jnp.maximum(m, jnp.max(s, -1, keepdims=True))
    o = None
    for s, (_, vv, _) in zip(ss, blocks):
        pv = _dot(jnp.exp2(s - m).astype(BF16), vv)
        o = pv if o is None else o + pv
    den = pltpu.roll(o, LANES // 2, 1) + jnp.exp2(sk2 - m)
    return o / den


def _group_select(x, g, fill):
    lane = lax.broadcasted_iota(jnp.int32, (1, LANES), 1)
    sel = (lane < LANES // 2) if g == 0 else (lane >= LANES // 2)
    return jnp.where(sel, x, jnp.full_like(x, fill))


def _swa_attn_kernel(sink_ref, q_ref, kp_ref, km_ref, kn_ref, vp_ref, vm_ref, vn_ref, kc_ref, vc_ref, gate_ref,
                     o_ref, *, bpt, W, R, G):
    i = pl.program_id(1)
    nblk = pl.num_programs(1) * bpt
    kcat = jnp.concatenate([kp_ref[0], km_ref[0], kn_ref[0]], 0)
    vcat = jnp.concatenate([vp_ref[0], vm_ref[0], vn_ref[0]], 0)
    vcat_g = [_group_select(vcat, g, 1.0) for g in range(G)]
    vc_g = [_group_select(vc_ref[0], g, 1.0) for g in range(G)]
    kc = kc_ref[0]
    lane = lax.broadcasted_iota(jnp.int32, (W, LANES), 1)
    qi = lax.broadcasted_iota(jnp.int32, (R * W, 3 * W), 0) % W
    kj = lax.broadcasted_iota(jnp.int32, (R * W, 3 * W), 1)
    band_bias = jnp.where(jnp.abs(W + qi - kj) <= W, 0.0, NEG)
    kj_row = lax.broadcasted_iota(jnp.int32, (1, 3 * W), 1)
    sk2 = [_sink_column(sink_ref, g, R, W) * LOG2E for g in range(G)]
    for jb in range(bpt):
        gblk = i * bpt + jb
        kw = kcat[jb * W:(jb + 3) * W]
        pen_prev = jnp.where(gblk > 0, 0.0, NEG)
        pen_next = jnp.where(gblk < nblk - 1, 0.0, NEG)
        bias = band_bias + jnp.where(kj_row < W, pen_prev, jnp.where(kj_row >= 2 * W, pen_next, 0.0))
        tiles = [q_ref[0, jb * W:(jb + 1) * W, r * LANES:(r + 1) * LANES] for r in range(R)]
        outg = []
        for g in range(G):
            qg = jnp.concatenate([_group_select(t, g, 0.0) for t in tiles], 0)
            outg.append(_swa_group(qg, sk2[g], [(kc, vc_g[g], None),
                                                (kw, vcat_g[g][jb * W:(jb + 3) * W], bias)]))
        for r in range(R):
            y = jnp.where(lane < LANES // 2, outg[0][r * W:(r + 1) * W], outg[1][r * W:(r + 1) * W])
            gt = gate_ref[0, jb * W:(jb + 1) * W, r * LANES:(r + 1) * LANES]
            o_ref[0, jb * W:(jb + 1) * W, r * LANES:(r + 1) * LANES] = y * _silu(gt)


def _swa_attn(q, k, v, kc, vc, sink, gate, bpt):
    B, S, NQ = q.shape
    Lc = kc.shape[1]
    W = WINDOW
    G = SWA_KV_HEADS
    R = SWA_HEADS // G
    nb = S // W
    T = bpt * W
    kern = functools.partial(_swa_attn_kernel, bpt=bpt, W=W, R=R, G=G)
    main = pl.BlockSpec((1, T, LANES), lambda b, i: (b, i, 0))
    prev = pl.BlockSpec((1, W, LANES), lambda b, i: (b, jnp.maximum(i * bpt - 1, 0), 0))
    nxt = pl.BlockSpec((1, W, LANES), lambda b, i: (b, jnp.minimum((i + 1) * bpt, nb - 1), 0))
    ctx = pl.BlockSpec((1, Lc, LANES), lambda b, i: (b, 0, 0))
    qs = pl.BlockSpec((1, T, NQ), lambda b, i: (b, i, 0))
    return pl.pallas_call(
        kern, name="swa_attn", grid=(B, S // T),
        in_specs=[pl.BlockSpec(memory_space=pltpu.SMEM), qs, prev, main, nxt, prev, main, nxt, ctx, ctx, qs],
        out_specs=qs,
        out_shape=jax.ShapeDtypeStruct((B, S, NQ), F32),
        compiler_params=_params(("parallel", "arbitrary")),
    )(sink, q, k, k, k, v, v, v, kc, vc, gate)


def _swa_ctx_kernel(sink_ref, q_ref, kc_ref, vc_ref, gate_ref, o_ref, *, Lc, R, G, scale):
    kc = kc_ref[0]
    lane = lax.broadcasted_iota(jnp.int32, (Lc, LANES), 1)
    tiles = [(q_ref[0, :, r * LANES:(r + 1) * LANES] * scale).astype(BF16) for r in range(R)]
    outg = []
    for g in range(G):
        qg = jnp.concatenate([_group_select(t, g, 0.0) for t in tiles], 0)
        outg.append(_swa_group(qg, _sink_column(sink_ref, g, R, Lc) * LOG2E,
                               [(kc, _group_select(vc_ref[0], g, 1.0), None)]))
    for r in range(R):
        y = jnp.where(lane < LANES // 2, outg[0][r * Lc:(r + 1) * Lc], outg[1][r * Lc:(r + 1) * Lc])
        o_ref[0, :, r * LANES:(r + 1) * LANES] = y * _silu(gate_ref[0, :, r * LANES:(r + 1) * LANES])


def _swa_ctx(q, kc, vc, sink, gate):
    B, Lc, NQ = q.shape
    G = SWA_KV_HEADS
    R = SWA_HEADS // G
    kern = functools.partial(_swa_ctx_kernel, Lc=Lc, R=R, G=G, scale=SWA_HD ** -0.5 * LOG2E)
    qs = pl.BlockSpec((1, Lc, NQ), lambda b: (b, 0, 0))
    ctx = pl.BlockSpec((1, Lc, LANES), lambda b: (b, 0, 0))
    return pl.pallas_call(
        kern, name="swa_ctx", grid=(B,),
        in_specs=[pl.BlockSpec(memory_space=pltpu.SMEM), qs, ctx, ctx, qs],
        out_specs=qs,
        out_shape=jax.ShapeDtypeStruct((B, Lc, NQ), F32),
        compiler_params=_params(("arbitrary",)),
    )(sink, q, kc, vc, gate)


def _merge_kernel(yh_ref, ym_ref, ys_ref, mg_ref, x_ref, mod_ref, wh, wm, ws, wo, lg, lb, o_ref, *, D, alpha):
    m = (jax.nn.sigmoid(mg_ref[0, :, 0:D]) * _dot(yh_ref[0].astype(BF16), wh[...])
         + jax.nn.sigmoid(mg_ref[0, :, D:2 * D]) * _dot(ym_ref[0].astype(BF16), wm[...])
         + jax.nn.sigmoid(mg_ref[0, :, 2 * D:3 * D]) * _dot(ys_ref[0].astype(BF16), ws[...]))
    out = _dot(m.astype(BF16), wo[...])
    r = alpha * x_ref[0] + mod_ref[0, :, 2 * D:3 * D] * out
    mu = jnp.mean(r, -1, keepdims=True)
    rc = r - mu
    var = jnp.mean(rc * rc, -1, keepdims=True)
    o_ref[0] = rc * lax.rsqrt(var + EPS) * lg[...] + lb[...]


def _merge(yh, ym, ys, mg, x, mod, wh, wm, ws, wo, lg, lb, alpha, tm):
    Bx, S, D = x.shape
    kern = functools.partial(_merge_kernel, D=D, alpha=alpha)
    tok = lambda w: pl.BlockSpec((1, tm, w), lambda b, i: (b, i, 0))
    cs = _const_spec
    return pl.pallas_call(
        kern, name="merge", grid=(Bx, S // tm),
        in_specs=[tok(yh.shape[2]), tok(ym.shape[2]), tok(ys.shape[2]), tok(3 * D), tok(D),
                  pl.BlockSpec((1, 1, 3 * D), lambda b, i: (b, 0, 0)),
                  cs(wh.shape), cs(wm.shape), cs(ws.shape), cs(wo.shape), cs((1, D)), cs((1, D))],
        out_specs=tok(D),
        out_shape=jax.ShapeDtypeStruct((Bx, S, D), F32),
        compiler_params=_params(("parallel", "arbitrary")),
    )(yh, ym, ys, mg, x, mod, wh, wm, ws, wo, lg.reshape(1, D), lb.reshape(1, D))


_PAIR_ORDER = (0, 4, 1, 5, 2, 6, 3, 7)

_IN_COLS = (3 * HY_W, HY_W, MLA_Q_RANK, MLA_KV_RANK, MLA_ROPE, MLA_HEADS * MLA_V,
            SWA_HEADS * SWA_HD, SWA_KV_HEADS * SWA_HD, SWA_KV_HEADS * SWA_HD, SWA_HEADS * SWA_HD)
_SLAB_WIDTHS = (None, 3 * HY_W, HY_W, MLA_Q_RANK, MLA_KV_RANK, LANES, MLA_HEADS * MLA_V,
                SWA_HEADS * SWA_HD, LANES, LANES, SWA_HEADS * SWA_HD)


def _pair_cols(w):
    D = w.shape[0]
    return w.reshape(D, SWA_HEADS, SWA_HD)[:, _PAIR_ORDER, :].reshape(D, SWA_HEADS * SWA_HD)


def _layout_w_in(w_in, D):
    offs = [0]
    for cw in _IN_COLS:
        offs.append(offs[-1] + cw)
    sl = [w_in[:, offs[i]:offs[i + 1]] for i in range(len(_IN_COLS))]
    hy, hyg, cq, ckv, kr, mlag, sq, sk, sv, swag = sl
    mg = w_in[:, offs[-1]:]
    kr_p = jnp.pad(kr, ((0, 0), (MLA_NOPE, LANES - MLA_NOPE - MLA_ROPE)))
    return jnp.concatenate([mg, hy, hyg, cq, ckv, kr_p, mlag, _pair_cols(sq), sk, sv, _pair_cols(swag)],
                           1).astype(BF16)


def _layer_weights(p, D):
    w = {}
    w['w_in'] = _layout_w_in(p['w_in'], D)
    uq = p['mla_w_uq'].reshape(MLA_Q_RANK, MLA_HEADS, MLA_QK)
    w['w_uq'] = jnp.pad(uq, ((0, 0), (0, 0), (0, LANES - MLA_QK))).reshape(MLA_Q_RANK, MLA_HEADS * LANES).astype(BF16)
    ukv = p['mla_w_ukv'].reshape(MLA_KV_RANK, MLA_HEADS, MLA_NOPE + MLA_V)
    w['w_uk'] = jnp.pad(ukv[:, :, :MLA_NOPE], ((0, 0), (0, 0), (0, LANES - MLA_NOPE))).reshape(
        MLA_KV_RANK, MLA_HEADS * LANES).astype(BF16)
    w['w_uv'] = ukv[:, :, MLA_NOPE:].reshape(MLA_KV_RANK, MLA_HEADS * MLA_V).astype(BF16)
    w['w_proj_hy'] = p['w_proj_hy'].astype(BF16)
    w['w_proj_mla'] = p['w_proj_mla'].astype(BF16)
    w['w_proj_swa'] = p['w_proj_swa'].reshape(SWA_HEADS, SWA_HD, D)[_PAIR_ORDER, :, :].reshape(
        SWA_HEADS * SWA_HD, D).astype(BF16)
    w['w_out'] = p['w_out'].astype(BF16)
    return w


def _trunk_layer(xl, xc, cvec, p, tabs, alpha, ctx_out):
    B, S, D = xl.shape
    Lc = xc.shape[1]
    w = _layer_weights(p, D)
    widths = (N_BRANCH * D,) + _SLAB_WIDTHS[1:]

    mod = _ada_mod(cvec, p['w_ada'], p['b_ada'])
    mod_l = mod[:B].reshape(B, 1, 3 * D)
    mod_c = jnp.broadcast_to(mod[B].reshape(1, 1, 3 * D), (B, 1, 3 * D))

    tm_c = min(Lc, 256)
    (mg_l, hy_l, hyg_l, cq_l, ckv_l, kr_l, mlag_l, sq_l, sk_l, sv_l, swag_l) = _in_proj(xl, mod_l, w['w_in'], widths, 256)
    (mg_c, hy_c, hyg_c, cq_c, ckv_c, kr_c, mlag_c, sq_c, sk_c, sv_c, swag_c) = _in_proj(xc, mod_c, w['w_in'], widths, tm_c)

    k_mc, v_mc = _mla_kv(ckv_c, kr_c, p['mla_kv_norm'], w['w_uk'], w['w_uv'], tabs['id_c'], tm_c)
    k_sc = sk_c.astype(BF16)
    v_sc = sv_c.astype(BF16)

    u_l, e_l = _hy_pre(hy_l, hyg_l, p['hy_conv_w'], p['hy_conv_b'], 512)
    h2_l = _hy_filters(S, p)
    y_hy = _hy_long_conv(u_l, e_l, h2_l, p['hy_skip'], tabs['dft'])

    q_ml = _mla_q(cq_l, p['mla_q_norm'], w['w_uq'], tabs['mla'], 512)
    k_ml, v_ml = _mla_kv(ckv_l, kr_l, p['mla_kv_norm'], w['w_uk'], w['w_uv'], tabs['mla'], 512)
    y_mla = _mla_attn(q_ml, k_mc, v_mc, k_ml, v_ml, mlag_l, 512)

    q_sl, k_sl, v_sl = _swa_prep(sq_l, sk_l, sv_l, tabs['swa'], 512)
    y_swa = _swa_attn(q_sl, k_sl, v_sl, k_sc, v_sc, p['swa_sink'], swag_l, 4)

    xl_new = _merge(y_hy, y_mla, y_swa, mg_l, xl, mod_l, w['w_proj_hy'], w['w_proj_mla'], w['w_proj_swa'],
                    w['w_out'], p['ln_g'], p['ln_b'], alpha, 256)
    if not ctx_out:
        return xl_new, xc

    u_c, e_c = _hy_pre(hy_c, hyg_c, p['hy_conv_w'], p['hy_conv_b'], tm_c)
    h2_c = _hy_filters(Lc, p)
    yc_hy = _hy_ctx_conv(u_c, e_c, h2_c, p['hy_skip'], tabs['dft_c'])
    q_mc = _mla_q(cq_c, p['mla_q_norm'], w['w_uq'], tabs['id_c'], tm_c)
    yc_mla = _mla_attn(q_mc, k_mc, v_mc, None, None, mlag_c, tm_c)
    yc_swa = _swa_ctx(sq_c, k_sc, v_sc, p['swa_sink'], swag_c)
    xc_new = _merge(yc_hy, yc_mla, yc_swa, mg_c, xc, mod_c, w['w_proj_hy'], w['w_proj_mla'], w['w_proj_swa'],
                    w['w_out'], p['ln_g'], p['ln_b'], alpha, tm_c)
    return xl_new, xc_new


def kernel(x, c, ctx, c_ctx, w_ada, b_ada, w_in, hy_conv_w, hy_conv_b, filt_w1, filt_b1, filt_w2, filt_b2,
           filt_w3, filt_b3, filt_freq, filt_w_out, hy_skip, mla_q_norm, mla_w_uq, mla_kv_norm, mla_w_ukv,
           swa_sink, w_proj_hy, w_proj_mla, w_proj_swa, w_out, ln_g, ln_b):
    B, S, D = x.shape
    Lc = ctx.shape[1]
    depth = w_in.shape[0]
    alpha = (2 * depth) ** 0.25
    stacked = dict(w_ada=w_ada, b_ada=b_ada, w_in=w_in, hy_conv_w=hy_conv_w, hy_conv_b=hy_conv_b,
                   filt_w1=filt_w1, filt_b1=filt_b1, filt_w2=filt_w2, filt_b2=filt_b2, filt_w3=filt_w3,
                   filt_b3=filt_b3, filt_freq=filt_freq, filt_w_out=filt_w_out, hy_skip=hy_skip,
                   mla_q_norm=mla_q_norm, mla_w_uq=mla_w_uq, mla_kv_norm=mla_kv_norm, mla_w_ukv=mla_w_ukv,
                   swa_sink=swa_sink, w_proj_hy=w_proj_hy, w_proj_mla=w_proj_mla, w_proj_swa=w_proj_swa,
                   w_out=w_out, ln_g=ln_g, ln_b=ln_b)

    lane = jnp.arange(LANES, dtype=jnp.int32)
    mla_on = (lane >= MLA_NOPE) & (lane < MLA_QK)
    tabs = {
        'mla': _rope_tables(S, jnp.clip(lane - MLA_NOPE, 0, MLA_ROPE - 1), mla_on, MLA_ROPE),
        'swa': _rope_tables(S, lane % SWA_HD, jnp.ones((LANES,), bool), SWA_HD),
        'id_c': _rope_tables(Lc, None, None, None, identity=True),
        'dft': _dft_tables(S),
        'dft_c': _ctx_dft_tables(Lc),
    }
    cvec = jnp.concatenate([c, c_ctx[None, :], jnp.zeros((8 - B - 1, D), F32)], 0)

    xl, xc = x, ctx
    for l in range(depth):
        p = {k: v[l] for k, v in stacked.items()}
        xl, xc = _trunk_layer(xl, xc, cvec, p, tabs, alpha, l < depth - 1)
    return xl
```

```python
import functools
import math

import jax
import jax.numpy as jnp
from jax import lax
from jax.experimental import pallas as pl
from jax.experimental.pallas import tpu as pltpu

F32 = jnp.float32
BF16 = jnp.bfloat16
HIGHEST = lax.Precision.HIGHEST

GRID_W = 64
HY_W = 512
SHORT_K = 3
FILT_BANDS = 16
FILT_W = 64
DECAY_TARGET = 1e-2
FAST_DECAY_PCT = 0.3
SLOW_DECAY_PCT = 1.5
MLA_HEADS = 8
MLA_NOPE = 64
MLA_ROPE = 32
MLA_V = 64
MLA_QK = MLA_NOPE + MLA_ROPE
MLA_Q_RANK = 256
MLA_KV_RANK = 128
SWA_HEADS = 8
SWA_KV_HEADS = 2
SWA_HD = 64
WINDOW = 128
N_BRANCH = 3
ROPE_BASE = 10000.0
EPS = 1e-6
NEG = -1e30
LOG2E = math.log2(math.e)

LANES = 128
DFT_N2 = 128
VMEM_LIMIT = 56 * 1024 * 1024


def _silu(x):
    return x * jax.nn.sigmoid(x)


def _dot(a, b):
    return jnp.dot(a, b, preferred_element_type=F32)


def _dot_nt(a, b):
    return lax.dot_general(a, b, (((1,), (1,)), ((), ())), preferred_element_type=F32)


def _params(sem):
    return pltpu.CompilerParams(dimension_semantics=sem, vmem_limit_bytes=VMEM_LIMIT)


def _const_spec(shape):
    nd = len(shape)
    return pl.BlockSpec(shape, lambda *_: (0,) * nd)


def _ada_kernel(c_ref, w_ref, b_ref, o_ref):
    a = _silu(c_ref[...])
    o_ref[...] = jnp.dot(a, w_ref[...], precision=HIGHEST, preferred_element_type=F32) + b_ref[...]


def _ada_mod(cvec, w_ada, b_ada):
    R, D = cvec.shape
    N = w_ada.shape[1]
    tn = 768
    return pl.pallas_call(
        _ada_kernel, name="ada_mod", grid=(N // tn,),
        in_specs=[_const_spec((R, D)), pl.BlockSpec((D, tn), lambda j: (0, j)),
                  pl.BlockSpec((1, tn), lambda j: (0, j))],
        out_specs=pl.BlockSpec((R, tn), lambda j: (0, j)),
        out_shape=jax.ShapeDtypeStruct((R, N), F32),
        compiler_params=_params(("arbitrary",)),
    )(cvec, w_ada, b_ada.reshape(1, N))


def _in_proj_kernel(x_ref, mod_ref, w_ref, *o_refs, widths, D, tm, sub):
    shift = mod_ref[0, :, 0:D]
    scale1 = 1.0 + mod_ref[0, :, D:2 * D]
    us = []
    for r in range(0, tm, sub):
        x = x_ref[0, r:r + sub, :]
        mu = jnp.mean(x, -1, keepdims=True)
        xc = x - mu
        var = jnp.mean(xc * xc, -1, keepdims=True)
        us.append(((xc * lax.rsqrt(var + EPS)) * scale1 + shift).astype(BF16))
    off = 0
    for o_ref, wd in zip(o_refs, widths):
        for k, u in enumerate(us):
            o_ref[0, k * sub:(k + 1) * sub, :] = _dot(u, w_ref[:, off:off + wd]).astype(o_ref.dtype)
        off += wd


def _in_proj(x, mod, w, widths, tm):
    Bx, S, D = x.shape
    P = w.shape[1]
    kern = functools.partial(_in_proj_kernel, widths=tuple(widths), D=D, tm=tm, sub=min(tm, 256))
    return pl.pallas_call(
        kern, name="in_proj", grid=(Bx, S // tm),
        in_specs=[pl.BlockSpec((1, tm, D), lambda b, i: (b, i, 0)),
                  pl.BlockSpec((1, 1, 3 * D), lambda b, i: (b, 0, 0)),
                  pl.BlockSpec((D, P), lambda b, i: (0, 0), pipeline_mode=pl.Buffered(1))],
        out_specs=[pl.BlockSpec((1, tm, wd), lambda b, i: (b, i, 0)) for wd in widths],
        out_shape=[jax.ShapeDtypeStruct((Bx, S, wd), F32) for wd in widths],
        compiler_params=_params(("parallel", "arbitrary")),
    )(x, mod, w)


def _filter_kernel(z_ref, t_ref, w1, b1, w2, b2, w3, b3, fr, wo, dl_ref, o_ref, *, tl, C):
    hp = lambda a, b: jnp.dot(a, b, precision=HIGHEST, preferred_element_type=F32)
    f = fr[...]
    h = jnp.sin(f * (hp(z_ref[...], w1[...]) + b1[...]))
    h = jnp.sin(f * (hp(h, w2[...]) + b2[...]))
    h = jnp.sin(f * (hp(h, w3[...]) + b3[...]))
    dl = jnp.abs(dl_ref[...])
    half = tl // 2
    for s in range(2):
        rows = slice(s * half, (s + 1) * half)
        o = hp(h, wo[s])
        decay = jnp.exp(-t_ref[rows, :] * dl)
        o_ref[rows, 0:C] = o[:, 0:C] * decay
        bwd = o[:, C:2 * C] * decay
        if s == 0:
            row = pl.program_id(0) * tl + lax.broadcasted_iota(jnp.int32, (half, 1), 0)
            bwd = jnp.where(row == 0, 0.0, bwd)
        o_ref[rows, C:2 * C] = bwd


def _hy_filters(L, p):
    C = HY_W
    FP = LANES
    FH = FP // 2
    t = jnp.linspace(0.0, 1.0, L, dtype=F32)[:, None]
    w = 2.0 * math.pi * jnp.arange(L, dtype=F32) / L
    f = jnp.linspace(1e-4, FILT_BANDS - 1, FILT_BANDS, dtype=F32)
    ang = w[:, None] * f[None, :]
    z = jnp.concatenate([t, jnp.cos(ang), -jnp.sin(ang)], -1)
    z = jnp.pad(z, ((0, 0), (0, FH - z.shape[1])))
    tl = min(L, 512)
    z = z.reshape(L // tl, 2, tl // 2, FH).transpose(0, 2, 1, 3).reshape(L // 2, FP)
    pad_h = lambda a: jnp.pad(a, ((0, FH - a.shape[0]), (0, FH - a.shape[1])))
    bdiag = lambda a: jnp.kron(jnp.eye(2, dtype=F32), pad_h(a))
    pad_v = lambda a: jnp.tile(jnp.pad(a, (0, FH - a.shape[0])), 2).reshape(1, FP)
    w1, w2, w3 = bdiag(p['filt_w1']), bdiag(p['filt_w2']), bdiag(p['filt_w3'])
    wo_h = jnp.pad(p['filt_w_out'], ((0, FH - FILT_W), (0, 0)))
    zeros = jnp.zeros_like(wo_h)
    wo = jnp.stack([jnp.concatenate([wo_h, zeros], 0), jnp.concatenate([zeros, wo_h], 0)], 0)
    min_decay = math.log(DECAY_TARGET) / SLOW_DECAY_PCT
    max_decay = math.log(DECAY_TARGET) / FAST_DECAY_PCT
    deltas = jnp.linspace(min_decay, max_decay, C, dtype=F32).reshape(1, C)
    kern = functools.partial(_filter_kernel, tl=tl, C=C)
    cs = _const_spec
    return pl.pallas_call(
        kern, name="hy_filter", grid=(L // tl,),
        in_specs=[pl.BlockSpec((tl // 2, FP), lambda i: (i, 0)), pl.BlockSpec((tl, 1), lambda i: (i, 0)),
                  cs((FP, FP)), cs((1, FP)), cs((FP, FP)), cs((1, FP)), cs((FP, FP)), cs((1, FP)),
                  cs((1, FP)), cs((2, FP, 2 * C)), cs((1, C))],
        out_specs=pl.BlockSpec((tl, 2 * C), lambda i: (i, 0)),
        out_shape=jax.ShapeDtypeStruct((L, 2 * C), F32),
        compiler_params=_params(("arbitrary",)),
    )(z, t, w1, pad_v(p['filt_b1']), w2, pad_v(p['filt_b2']), w3, pad_v(p['filt_b3']),
      pad_v(p['filt_freq']), wo, deltas)


def _hy_pre_kernel(x_ref, xp_ref, xn_ref, g_ref, w_ref, b_ref, u_ref, e_ref, *, ts, C):
    i = pl.program_id(1)
    nt = pl.num_programs(1)
    x = x_ref[0]
    prev_row = xp_ref[0, 7:8, :] * jnp.where(i > 0, 1.0, 0.0)
    next_row = xn_ref[0, 0:1, :] * jnp.where(i < nt - 1, 1.0, 0.0)
    rid = lax.broadcasted_iota(jnp.int32, (ts, 1), 0)
    xm = jnp.where(rid == 0, prev_row, pltpu.roll(x, 1, 0))
    xq = jnp.where(rid == ts - 1, next_row, pltpu.roll(x, ts - 1, 0))
    z = b_ref[...] + xm * w_ref[0:1, :] + x * w_ref[1:2, :] + xq * w_ref[2:3, :]
    u_ref[0] = z[:, 2 * C:3 * C] * z[:, C:2 * C]
    e_ref[0] = z[:, 0:C] * _silu(g_ref[0])


def _hy_pre(hy, hyg, conv_w, conv_b, ts):
    Bx, S, C3 = hy.shape
    C = C3 // 3
    nb8 = S // 8
    r = ts // 8
    w8 = jnp.pad(conv_w, ((0, 8 - SHORT_K), (0, 0)))
    kern = functools.partial(_hy_pre_kernel, ts=ts, C=C)
    return pl.pallas_call(
        kern, name="hy_pre", grid=(Bx, S // ts),
        in_specs=[pl.BlockSpec((1, ts, C3), lambda b, i: (b, i, 0)),
                  pl.BlockSpec((1, 8, C3), lambda b, i: (b, jnp.maximum(i * r - 1, 0), 0)),
                  pl.BlockSpec((1, 8, C3), lambda b, i: (b, jnp.minimum((i + 1) * r, nb8 - 1), 0)),
                  pl.BlockSpec((1, ts, C), lambda b, i: (b, i, 0)),
                  _const_spec((8, C3)), _const_spec((1, C3))],
        out_specs=[pl.BlockSpec((1, ts, C), lambda b, i: (b, i, 0))] * 2,
        out_shape=[jax.ShapeDtypeStruct((Bx, S, C), F32)] * 2,
        compiler_params=_params(("parallel", "arbitrary")),
    )(hy, hy, hy, hyg, w8, conv_b.reshape(1, C3))


def _dft_tables(L):
    n = 2 * L
    N2 = DFT_N2
    N1 = n // N2
    H1 = N1 // 2
    k1 = jnp.arange(N1, dtype=jnp.int32)
    t1 = jnp.arange(H1, dtype=jnp.int32)
    ang = (2.0 * math.pi / N1) * ((k1[:, None] * t1[None, :]) % N1).astype(F32)
    fa = jnp.stack([jnp.cos(ang), -jnp.sin(ang)], 1).reshape(2 * N1, H1)
    fai = jnp.concatenate([jnp.cos(ang), -jnp.sin(ang)], 0).T * (1.0 / n)
    k2 = jnp.arange(N2, dtype=jnp.int32)
    t2 = jnp.arange(N2, dtype=jnp.int32)
    m = (t2[None, None, :] * (k2[None, :, None] * N1 + k1[:, None, None])) % n
    th = (2.0 * math.pi / n) * m.astype(F32)
    cr, ci = jnp.cos(th), -jnp.sin(th)
    mf = jnp.concatenate([jnp.concatenate([cr, -ci], 2), jnp.concatenate([ci, cr], 2)], 1)
    return fa.astype(BF16), fai.astype(BF16), mf.astype(BF16)


DFT_TT = 16
DFT_KB = 8


def _dft_a_fwd_kernel(x_ref, f_ref, o_ref):
    f = f_ref[...]
    for j in range(DFT_TT):
        o_ref[0, j] = _dot(f, x_ref[0, :, j, :].astype(BF16)).astype(o_ref.dtype)


def _dft_a_fwd(x4, fa):
    Bx, H1, N2, C = x4.shape
    R = fa.shape[0]
    return pl.pallas_call(
        _dft_a_fwd_kernel, name="dft_a_fwd", grid=(Bx, N2 // DFT_TT),
        in_specs=[pl.BlockSpec((1, H1, DFT_TT, C), lambda b, j: (b, 0, j, 0)), _const_spec((R, H1))],
        out_specs=pl.BlockSpec((1, DFT_TT, R, C), lambda b, j: (b, j, 0, 0)),
        out_shape=jax.ShapeDtypeStruct((Bx, N2, R, C), BF16),
        compiler_params=_params(("parallel", "arbitrary")),
    )(x4, fa)


def _stage_b_operand(scr, j):
    return jnp.concatenate([scr[:, 2 * j, :], scr[:, 2 * j + 1, :]], 0).astype(BF16)


def _dft_b_filter_kernel(a_ref, m_ref, k_ref, scr, *, kb, C, N2):
    scr[...] = a_ref[0].astype(F32)
    for j in range(kb):
        h = _dot(m_ref[j], _stage_b_operand(scr, j))
        k_ref[j, 0:N2, :] = h[0:N2, 0:C] + h[0:N2, C:2 * C]
        k_ref[j, N2:2 * N2, :] = h[N2:2 * N2, 0:C] - h[N2:2 * N2, C:2 * C]


def _dft_b_filter(ah, mf):
    _, N2, R, C2 = ah.shape
    N1, C, kb = R // 2, C2 // 2, DFT_KB
    kern = functools.partial(_dft_b_filter_kernel, kb=kb, C=C, N2=N2)
    return pl.pallas_call(
        kern, name="dft_b_filter", grid=(N1 // kb,),
        in_specs=[pl.BlockSpec((1, N2, 2 * kb, C2), lambda i: (0, 0, i, 0)),
                  pl.BlockSpec((kb, 2 * N2, 2 * N2), lambda i: (i, 0, 0))],
        out_specs=pl.BlockSpec((kb, 2 * N2, C), lambda i: (i, 0, 0)),
        out_shape=jax.ShapeDtypeStruct((N1, 2 * N2, C), F32),
        scratch_shapes=[pltpu.VMEM((N2, 2 * kb, C2), F32)],
        compiler_params=_params(("arbitrary",)),
    )(ah, mf)


def _dft_mid_kernel(a_ref, mf_ref, k_ref, g_ref, scr, *, kb, nb, C, N2):
    for b in range(nb):
        scr[...] = a_ref[b].astype(F32)
        for j in range(kb):
            kre = k_ref[j, 0:N2, :]
            kim = k_ref[j, N2:2 * N2, :]
            y = _dot(mf_ref[j], _stage_b_operand(scr, j))
            yre, yim = y[0:N2], y[N2:2 * N2]
            z = jnp.concatenate([yre * kre - yim * kim, yre * kim + yim * kre], 0).astype(BF16)
            g = lax.dot_general(mf_ref[j], z, (((0,), (0,)), ((), ())), preferred_element_type=F32)
            g_ref[b, j] = g.astype(g_ref.dtype)


def _dft_mid(a, mf, kf):
    Bx, N2, R, C = a.shape
    N1, kb = R // 2, DFT_KB
    kern = functools.partial(_dft_mid_kernel, kb=kb, nb=Bx, C=C, N2=N2)
    return pl.pallas_call(
        kern, name="dft_mid", grid=(N1 // kb,),
        in_specs=[pl.BlockSpec((Bx, N2, 2 * kb, C), lambda i: (0, 0, i, 0)),
                  pl.BlockSpec((kb, 2 * N2, 2 * N2), lambda i: (i, 0, 0)),
                  pl.BlockSpec((kb, 2 * N2, C), lambda i: (i, 0, 0))],
        out_specs=pl.BlockSpec((Bx, kb, 2 * N2, C), lambda i: (0, i, 0, 0)),
        out_shape=jax.ShapeDtypeStruct((Bx, N1, 2 * N2, C), BF16),
        scratch_shapes=[pltpu.VMEM((N2, 2 * kb, C), F32)],
        compiler_params=_params(("arbitrary",)),
    )(a, mf, kf)


def _dft_a_inv_kernel(g_ref, f_ref, u_ref, e_ref, s_ref, o_ref, scr):
    f = f_ref[...]
    scr[...] = g_ref[0].astype(F32)
    for j in range(DFT_TT):
        g = jnp.concatenate([scr[:, 0, j, :], scr[:, 1, j, :]], 0).astype(BF16)
        y = _dot(f, g)
        o_ref[0, :, j, :] = (y + u_ref[0, :, j, :] * s_ref[...]) * e_ref[0, :, j, :]


def _dft_a_inv(g5, fai, u4, e4, skip):
    Bx, N1, _, N2, C = g5.shape
    H1 = fai.shape[0]
    tok = pl.BlockSpec((1, H1, DFT_TT, C), lambda b, j: (b, 0, j, 0))
    return pl.pallas_call(
        _dft_a_inv_kernel, name="dft_a_inv", grid=(Bx, N2 // DFT_TT),
        in_specs=[pl.BlockSpec((1, N1, 2, DFT_TT, C), lambda b, j: (b, 0, 0, j, 0)), _const_spec((H1, 2 * N1)),
                  tok, tok, _const_spec((1, C))],
        out_specs=tok,
        out_shape=jax.ShapeDtypeStruct((Bx, H1, N2, C), F32),
        scratch_shapes=[pltpu.VMEM((N1, 2, DFT_TT, C), F32)],
        compiler_params=_params(("parallel", "arbitrary")),
    )(g5, fai, u4, e4, skip)


def _hy_long_conv(u, e, h2, skip, tabs):
    B, L, C = u.shape
    fa, fai, mf = tabs
    N2 = DFT_N2
    N1 = 2 * L // N2
    H1 = N1 // 2
    kf = _dft_b_filter(_dft_a_fwd(h2.reshape(1, H1, N2, 2 * C), fa), mf)
    u4 = u.reshape(B, H1, N2, C)
    g = _dft_mid(_dft_a_fwd(u4, fa), mf, kf)
    y = _dft_a_inv(g.reshape(B, N1, 2, N2, C), fai, u4, e.reshape(B, H1, N2, C), skip.reshape(1, C))
    return y.reshape(B, L, C)


def _ctx_dft_tables(Lc):
    n = 2 * Lc
    k = jnp.arange(n, dtype=jnp.int32)
    t = jnp.arange(Lc, dtype=jnp.int32)
    ang = (2.0 * math.pi / n) * ((k[:, None] * t[None, :]) % n).astype(F32)
    fc = jnp.concatenate([jnp.cos(ang), -jnp.sin(ang)], 0)
    fi = fc.T * (1.0 / n)
    return fc.astype(BF16), fi.astype(BF16)


def _hy_ctx_conv_kernel(u_ref, e_ref, h_ref, fc_ref, fi_ref, s_ref, o_ref, *, n, C):
    u = u_ref[0]
    fc = fc_ref[...]
    uf = _dot(fc, u.astype(BF16))
    hf = _dot(fc, h_ref[...].astype(BF16))
    kre = hf[0:n, 0:C] + hf[0:n, C:2 * C]
    kim = hf[n:2 * n, 0:C] - hf[n:2 * n, C:2 * C]
    ure, uim = uf[0:n], uf[n:2 * n]
    z = jnp.concatenate([ure * kre - uim * kim, ure * kim + uim * kre], 0).astype(BF16)
    y = _dot(fi_ref[...], z)
    o_ref[0] = (y + u * s_ref[...]) * e_ref[0]


def _hy_ctx_conv(u, e, h2, skip, tabs):
    B, Lc, C = u.shape
    fc, fi = tabs
    n = 2 * Lc
    kern = functools.partial(_hy_ctx_conv_kernel, n=n, C=C)
    blk = pl.BlockSpec((1, Lc, C), lambda b: (b, 0, 0))
    return pl.pallas_call(
        kern, name="hy_ctx_conv", grid=(B,),
        in_specs=[blk, blk, _const_spec((Lc, 2 * C)), _const_spec((2 * n, Lc)), _const_spec((Lc, 2 * n)),
                  _const_spec((1, C))],
        out_specs=blk,
        out_shape=jax.ShapeDtypeStruct((B, Lc, C), F32),
        compiler_params=_params(("arbitrary",)),
    )(u, e, h2, fc, fi, skip.reshape(1, C))


def _rope_tables(S, lane_dim, lane_on, head_rot, identity=False):
    if identity:
        return (jnp.ones((S, LANES), F32), jnp.zeros((S, LANES), F32), jnp.zeros((S, LANES), F32))
    seg_w = head_rot // 2
    half = seg_w // 2
    seg = lane_dim // seg_w
    w = lane_dim % seg_w
    first = w < half
    inv = ROPE_BASE ** (-(w % half).astype(F32) / half)
    nrow = S // GRID_W
    ang_r = jnp.arange(nrow, dtype=jnp.int32).astype(F32)[:, None] * inv[None, :]
    ang_c = jnp.arange(GRID_W, dtype=jnp.int32).astype(F32)[:, None] * inv[None, :]

    def expand(fn):
        tab = jnp.where((seg == 0)[None, None, :], fn(ang_r)[:, None, :], fn(ang_c)[None, :, :])
        return tab.reshape(S, LANES)

    cos, sin = expand(jnp.cos), expand(jnp.sin)
    on = lane_on[None, :]
    c = jnp.where(on, cos, 1.0)
    sm = jnp.where(on & first[None, :], -sin, 0.0)
    sp = jnp.where(on & (~first)[None, :], sin, 0.0)
    return c, sm, sp


def _rope(x, c, sm, sp, hh):
    return x * c + pltpu.roll(x, LANES - hh, 1) * sm + pltpu.roll(x, hh, 1) * sp


MLA_VROWS = 80

def _mla_q_kernel(cq_ref, g_ref, w_ref, c_ref, sm_ref, sp_ref, o_ref, *, nh, scale, hh):
    x = cq_ref[0]
    xn = x * lax.rsqrt(jnp.mean(x * x, -1, keepdims=True) + EPS) * g_ref[...]
    q = _dot(xn.astype(BF16), w_ref[...])
    c, sm, sp = c_ref[...], sm_ref[...], sp_ref[...]
    for h in range(nh):
        qh = _rope(q[:, h * LANES:(h + 1) * LANES], c, sm, sp, hh) * scale
        o_ref[0, h * LANES:(h + 1) * LANES, :] = qh.T.astype(o_ref.dtype)


def _mla_q(cq, qnorm, w_uq_p, tabs, tm):
    Bx, S, R = cq.shape
    N = w_uq_p.shape[1]
    kern = functools.partial(_mla_q_kernel, nh=MLA_HEADS, scale=MLA_QK ** -0.5 * LOG2E,
                             hh=MLA_ROPE // 4)
    tab = pl.BlockSpec((tm, LANES), lambda b, i: (i, 0))
    return pl.pallas_call(
        kern, name="mla_q", grid=(Bx, S // tm),
        in_specs=[pl.BlockSpec((1, tm, R), lambda b, i: (b, i, 0)), _const_spec((1, R)), _const_spec((R, N)),
                  tab, tab, tab],
        out_specs=pl.BlockSpec((1, N, tm), lambda b, i: (b, 0, i)),
        out_shape=jax.ShapeDtypeStruct((Bx, N, S), BF16),
        compiler_params=_params(("parallel", "arbitrary")),
    )(cq, qnorm.reshape(1, R), w_uq_p, *tabs)


def _mla_kv_kernel(ckv_ref, kr_ref, g_ref, wk_ref, wv_ref, c_ref, sm_ref, sp_ref, k_ref, vt_ref, *, nh, hh, tm):
    x = ckv_ref[0]
    xn = (x * lax.rsqrt(jnp.mean(x * x, -1, keepdims=True) + EPS) * g_ref[...]).astype(BF16)
    kn = _dot(xn, wk_ref[...])
    krr = _rope(kr_ref[0], c_ref[...], sm_ref[...], sp_ref[...], hh)
    for h in range(nh):
        k_ref[0, :, h * LANES:(h + 1) * LANES] = (kn[:, h * LANES:(h + 1) * LANES] + krr).astype(k_ref.dtype)
    vt = _dot(xn, wv_ref[...]).T
    pad = MLA_VROWS - MLA_V
    ones_rows = jnp.where(lax.broadcasted_iota(jnp.int32, (pad, tm), 0) == 0, 1.0, 0.0).astype(vt_ref.dtype)
    for h in range(nh):
        vt_ref[0, 0, h * MLA_VROWS:h * MLA_VROWS + MLA_V, :] = vt[h * MLA_V:(h + 1) * MLA_V].astype(vt_ref.dtype)
        vt_ref[0, 0, h * MLA_VROWS + MLA_V:(h + 1) * MLA_VROWS, :] = ones_rows


def _mla_kv(ckv, kr, kvnorm, wk_p, wv, tabs, tm):
    Bx, S, R = ckv.shape
    NK, NV = wk_p.shape[1], wv.shape[1]
    kern = functools.partial(_mla_kv_kernel, nh=MLA_HEADS, hh=MLA_ROPE // 4, tm=tm)
    tab = pl.BlockSpec((tm, LANES), lambda b, i: (i, 0))
    VR = MLA_HEADS * MLA_VROWS
    return pl.pallas_call(
        kern, name="mla_kv", grid=(Bx, S // tm),
        in_specs=[pl.BlockSpec((1, tm, R), lambda b, i: (b, i, 0)),
                  pl.BlockSpec((1, tm, LANES), lambda b, i: (b, i, 0)),
                  _const_spec((1, R)), _const_spec((R, NK)), _const_spec((R, NV)), tab, tab, tab],
        out_specs=[pl.BlockSpec((1, tm, NK), lambda b, i: (b, i, 0)),
                   pl.BlockSpec((1, 1, VR, tm), lambda b, i: (b, i, 0, 0))],
        out_shape=[jax.ShapeDtypeStruct((Bx, S, NK), BF16), jax.ShapeDtypeStruct((Bx, S // tm, VR, tm), BF16)],
        compiler_params=_params(("parallel", "arbitrary")),
    )(ckv, kr, kvnorm.reshape(1, R), wk_p, wv, *tabs)


def _mla_attn_kernel(*refs, tq, tk, n_chunks):
    if n_chunks:
        qt_ref, kc_ref, vct_ref, k_ref, vt_ref, gate_ref, o_ref = refs[:7]
    else:
        qt_ref, kc_ref, vct_ref, gate_ref, o_ref = refs
    VR = MLA_VROWS
    qs = [qt_ref[0, j * LANES:(j + 1) * LANES, :] for j in range(2)]

    def accumulate(m, acc, st, mx, vblk):
        m_new = mx if m is None else jnp.maximum(m, mx)
        pt = jnp.exp2(st - m_new).astype(BF16)
        pv = _dot(vblk, pt)
        acc = pv if m is None else jnp.exp2(m - m_new) * acc + pv
        return m_new, acc

    m, acc = [], []
    for j in range(2):
        st = _dot(kc_ref[0, :, j * LANES:(j + 1) * LANES], qs[j])
        mj, aj = accumulate(None, None, st, jnp.max(st, 0, keepdims=True), vct_ref[0, 0, j * VR:(j + 1) * VR, :])
        m.append(mj)
        acc.append(aj)

    if n_chunks:
        s_scr = refs[7:]

        def scores(j, c, slot):
            st = pl.multiple_of(c * tk, tk)
            blk = _dot(k_ref[0, pl.ds(st, tk), j * LANES:(j + 1) * LANES], qs[j])
            s_scr[2 * j + slot][...] = blk
            return jnp.max(blk, 0, keepdims=True)

        def consume(j, c, slot, mj, aj, mxj):
            return accumulate(mj, aj, s_scr[2 * j + slot][...], mxj, vt_ref[0, c, j * VR:(j + 1) * VR, :])

        mx = [scores(j, 0, 0) for j in range(2)]

        unroll = 4 if n_chunks % 4 == 0 else 2

        def steps(c, m, acc, mx, last):
            for s in range(unroll):
                slot = s % 2
                nxt = mx
                if not (last and s == unroll - 1):
                    nxt = [scores(j, c + s + 1, 1 - slot) for j in range(2)]
                for j in range(2):
                    m[j], acc[j] = consume(j, c + s, slot, m[j], acc[j], mx[j])
                mx = nxt
            return m, acc, mx

        def body(ci, carry):
            m0, a0, x0, m1, a1, x1 = carry
            m, acc, mx = steps(unroll * ci, [m0, m1], [a0, a1], [x0, x1], False)
            return m[0], acc[0], mx[0], m[1], acc[1], mx[1]

        m0, a0, x0, m1, a1, x1 = lax.fori_loop(0, n_chunks // unroll - 1, body,
                                               (m[0], acc[0], mx[0], m[1], acc[1], mx[1]))
        m, acc, _ = steps(n_chunks - unroll, [m0, m1], [a0, a1], [x0, x1], True)
    yt = jnp.concatenate([acc[j][0:MLA_V] / acc[j][MLA_V:MLA_V + 1] for j in range(2)], 0)
    o_ref[0] = yt.T * _silu(gate_ref[0])


def _mla_attn(qt, kc, vct, k, vt, gate, tq):
    B, _, S = qt.shape
    Lc = kc.shape[1]
    npair = MLA_HEADS // 2
    VR2 = 2 * MLA_VROWS
    n_chunks, tk = (0, 0) if k is None else (vt.shape[1], vt.shape[3])
    kern = functools.partial(_mla_attn_kernel, tq=tq, tk=tk, n_chunks=n_chunks)
    in_specs = [pl.BlockSpec((1, 2 * LANES, tq), lambda b, p, i: (b, p, i)),
                pl.BlockSpec((1, Lc, 2 * LANES), lambda b, p, i: (b, 0, p)),
                pl.BlockSpec((1, 1, VR2, Lc), lambda b, p, i: (b, 0, p, 0))]
    args = [qt, kc, vct]
    if n_chunks:
        Sk = k.shape[1]
        in_specs += [pl.BlockSpec((1, Sk, 2 * LANES), lambda b, p, i: (b, 0, p)),
                     pl.BlockSpec((1, n_chunks, VR2, tk), lambda b, p, i: (b, 0, p, 0))]
        args += [k, vt]
    in_specs.append(pl.BlockSpec((1, tq, LANES), lambda b, p, i: (b, i, p)))
    args.append(gate)
    return pl.pallas_call(
        kern, name="mla_attn", grid=(B, npair, S // tq),
        in_specs=in_specs,
        out_specs=pl.BlockSpec((1, tq, LANES), lambda b, p, i: (b, i, p)),
        out_shape=jax.ShapeDtypeStruct((B, S, npair * LANES), F32),
        scratch_shapes=[pltpu.VMEM((tk, tq), F32)] * (4 if n_chunks else 0),
        compiler_params=_params(("parallel", "arbitrary", "arbitrary")),
    )(*args)


SWA_VROWS = 80


def _ones_rows(n, width, dtype):
    return jnp.where(lax.broadcasted_iota(jnp.int32, (n, width), 0) == 0, 1.0, 0.0).astype(dtype)


def _swa_prep_kernel(q_ref, k_ref, v_ref, c_ref, sm_ref, sp_ref, qo_ref, ko_ref, vo_ref, *, scale, hh, nq, tm):
    c, sm, sp = c_ref[...], sm_ref[...], sp_ref[...]
    for r in range(nq):
        qr = _rope(q_ref[0, :, r * LANES:(r + 1) * LANES], c, sm, sp, hh) * scale
        qo_ref[0, r * LANES:(r + 1) * LANES, :] = qr.T.astype(qo_ref.dtype)
    ko_ref[0] = _rope(k_ref[0], c, sm, sp, hh).astype(ko_ref.dtype)
    vt = v_ref[0].T
    hd = LANES // SWA_KV_HEADS
    ones = _ones_rows(SWA_VROWS - hd, tm, vo_ref.dtype)
    for g in range(SWA_KV_HEADS):
        vo_ref[0, g * SWA_VROWS:g * SWA_VROWS + hd, :] = vt[g * hd:(g + 1) * hd].astype(vo_ref.dtype)
        vo_ref[0, g * SWA_VROWS + hd:(g + 1) * SWA_VROWS, :] = ones


def _swa_prep(sq, sk, sv, tabs, tm):
    Bx, S, NQ = sq.shape
    VR = SWA_KV_HEADS * SWA_VROWS
    kern = functools.partial(_swa_prep_kernel, scale=SWA_HD ** -0.5 * LOG2E, hh=SWA_HD // 4, nq=NQ // LANES, tm=tm)
    tab = pl.BlockSpec((tm, LANES), lambda b, i: (i, 0))
    qs = pl.BlockSpec((1, tm, NQ), lambda b, i: (b, i, 0))
    ks = pl.BlockSpec((1, tm, LANES), lambda b, i: (b, i, 0))
    return pl.pallas_call(
        kern, name="swa_prep", grid=(Bx, S // tm),
        in_specs=[qs, ks, ks, tab, tab, tab],
        out_specs=[pl.BlockSpec((1, NQ, tm), lambda b, i: (b, 0, i)), ks,
                   pl.BlockSpec((1, VR, tm), lambda b, i: (b, 0, i))],
        out_shape=[jax.ShapeDtypeStruct((Bx, NQ, S), BF16), jax.ShapeDtypeStruct((Bx, S, LANES), BF16),
                   jax.ShapeDtypeStruct((Bx, VR, S), BF16)],
        compiler_params=_params(("parallel", "arbitrary")),
    )(sq, sk, sv, *tabs)


def _sink_column(sink_ref, g, R, W):
    rid = lax.broadcasted_iota(jnp.int32, (R * W, 1), 0)
    col = jnp.full((R * W, 1), sink_ref[g * R + R - 1], F32)
    for r in range(R - 2, -1, -1):
        col = jnp.where(rid < (r + 1) * W, sink_ref[g * R + r], col)
    return col


def _swa_group(qg, sk2, blocks):
    ss = []
    for kk, _, bias in blocks:
        s = _dot_nt(qg, kk)
        ss.append(s if bias is None else s + bias)
    m = sk2
    for s in ss:
        m = jnp.maximum(m, jnp.max(s, -1, keepdims=True))
    o = None
    for s, (_, vv, _) in zip(ss, blocks):
        pv = _dot(jnp.exp2(s - m).astype(BF16), vv)
        o = pv if o is None else o + pv
    den = pltpu.roll(o, LANES // 2, 1) + jnp.exp2(sk2 - m)
    return o / den


def _group_select(x, g, fill):
    lane = lax.broadcasted_iota(jnp.int32, (1, LANES), 1)
    sel = (lane < LANES // 2) if g == 0 else (lane >= LANES // 2)
    return jnp.where(sel, x, jnp.full_like(x, fill))


def _swa_attn_kernel(sink_ref, qt_ref, kp_ref, km_ref, kn_ref, vp_ref, vm_ref, vn_ref, kc_ref, vct_ref, gate_ref,
                     o_ref, *s_scr, bpt, W, R, G, Lc):
    i = pl.program_id(1)
    nblk = pl.num_programs(1) * bpt
    hd = LANES // G
    VR = SWA_VROWS
    kcat = jnp.concatenate([kp_ref[0], km_ref[0], kn_ref[0]], 0)
    vcat = jnp.concatenate([vp_ref[0], vm_ref[0], vn_ref[0]], 1)
    kc = kc_ref[0]
    nk = Lc + 3 * W
    kj = lax.broadcasted_iota(jnp.int32, (nk, R * W), 0) - Lc
    qi = lax.broadcasted_iota(jnp.int32, (nk, R * W), 1) % W
    band_bias = jnp.where((kj < 0) | (jnp.abs(W + qi - kj) <= W), 0.0, NEG)
    head = lax.broadcasted_iota(jnp.int32, (1, R * W), 1) // W
    row = lax.broadcasted_iota(jnp.int32, (LANES, 1), 0)
    sk2 = []
    for g in range(G):
        sk = jnp.full((1, R * W), sink_ref[g * R + R - 1], F32)
        for r in range(R - 2, -1, -1):
            sk = jnp.where(head <= r, sink_ref[g * R + r], sk)
        sk2.append(sk * LOG2E)
    kj_col = lax.broadcasted_iota(jnp.int32, (nk, 1), 0) - Lc
    items = [(jb, g) for jb in range(bpt) for g in range(G)]

    def scores(n):
        jb, g = items[n]
        gblk = i * bpt + jb
        keys = jnp.concatenate([kc, kcat[jb * W:(jb + 3) * W]], 0)
        pen_prev = jnp.where(gblk > 0, 0.0, NEG)
        pen_next = jnp.where(gblk < nblk - 1, 0.0, NEG)
        pen = jnp.where((kj_col >= 0) & (kj_col < W), pen_prev, jnp.where(kj_col >= 2 * W, pen_next, 0.0))
        sel = (row >= g * hd) & (row < (g + 1) * hd)
        tiles = [qt_ref[0, r * LANES:(r + 1) * LANES, jb * W:(jb + 1) * W] for r in range(R)]
        qg = jnp.concatenate([jnp.where(sel, t, jnp.zeros_like(t)) for t in tiles], 1)
        st = (_dot(keys, qg) + band_bias) + pen
        s_scr[n % 2][...] = st
        return jnp.max(st, 0, keepdims=True)

    def consume(n, mx):
        jb, g = items[n]
        m = jnp.maximum(mx, sk2[g])
        pt = jnp.exp2(s_scr[n % 2][...] - m).astype(BF16)
        o = (_dot(vct_ref[0, g * VR:(g + 1) * VR, :], pt[0:Lc])
             + _dot(vcat[g * VR:(g + 1) * VR, jb * W:(jb + 3) * W], pt[Lc:nk]))
        den = o[hd:hd + 1] + jnp.exp2(sk2[g] - m)
        return o[0:hd] / den

    mx = scores(0)
    outg = []
    for n, (jb, g) in enumerate(items):
        nxt = scores(n + 1) if n + 1 < len(items) else None
        outg.append(consume(n, mx))
        mx = nxt
        if g == G - 1:
            for r in range(R):
                yt = jnp.concatenate([og[:, r * W:(r + 1) * W] for og in outg], 0)
                gt = gate_ref[0, jb * W:(jb + 1) * W, r * LANES:(r + 1) * LANES]
                o_ref[0, jb * W:(jb + 1) * W, r * LANES:(r + 1) * LANES] = yt.T * _silu(gt)
            outg = []


def _swa_attn(qt, k, vt, kc, vct, sink, gate, bpt):
    B, NQ, S = qt.shape
    Lc = kc.shape[1]
    W = WINDOW
    G = SWA_KV_HEADS
    R = SWA_HEADS // G
    VR = G * SWA_VROWS
    nb = S // W
    T = bpt * W
    kern = functools.partial(_swa_attn_kernel, bpt=bpt, W=W, R=R, G=G, Lc=Lc)
    main = pl.BlockSpec((1, T, LANES), lambda b, i: (b, i, 0))
    prev = pl.BlockSpec((1, W, LANES), lambda b, i: (b, jnp.maximum(i * bpt - 1, 0), 0))
    nxt = pl.BlockSpec((1, W, LANES), lambda b, i: (b, jnp.minimum((i + 1) * bpt, nb - 1), 0))
    main_t = pl.BlockSpec((1, VR, T), lambda b, i: (b, 0, i))
    prev_t = pl.BlockSpec((1, VR, W), lambda b, i: (b, 0, jnp.maximum(i * bpt - 1, 0)))
    nxt_t = pl.BlockSpec((1, VR, W), lambda b, i: (b, 0, jnp.minimum((i + 1) * bpt, nb - 1)))
    tok = pl.BlockSpec((1, T, NQ), lambda b, i: (b, i, 0))
    return pl.pallas_call(
        kern, name="swa_attn", grid=(B, S // T),
        in_specs=[pl.BlockSpec(memory_space=pltpu.SMEM), pl.BlockSpec((1, NQ, T), lambda b, i: (b, 0, i)),
                  prev, main, nxt, prev_t, main_t, nxt_t,
                  pl.BlockSpec((1, Lc, LANES), lambda b, i: (b, 0, 0)),
                  pl.BlockSpec((1, VR, Lc), lambda b, i: (b, 0, 0)), tok],
        out_specs=tok,
        out_shape=jax.ShapeDtypeStruct((B, S, NQ), F32),
        scratch_shapes=[pltpu.VMEM((Lc + 3 * W, R * W), F32)] * 2,
        compiler_params=_params(("parallel", "arbitrary")),
    )(sink, qt, k, k, k, vt, vt, vt, kc, vct, gate)


def _swa_ctx_kernel(sink_ref, q_ref, kc_ref, vc_ref, gate_ref, o_ref, *, Lc, R, G, scale):
    kc = kc_ref[0]
    lane = lax.broadcasted_iota(jnp.int32, (Lc, LANES), 1)
    tiles = [(q_ref[0, :, r * LANES:(r + 1) * LANES] * scale).astype(BF16) for r in range(R)]
    outg = []
    for g in range(G):
        qg = jnp.concatenate([_group_select(t, g, 0.0) for t in tiles], 0)
        outg.append(_swa_group(qg, _sink_column(sink_ref, g, R, Lc) * LOG2E,
                               [(kc, _group_select(vc_ref[0], g, 1.0), None)]))
    for r in range(R):
        y = jnp.where(lane < LANES // 2, outg[0][r * Lc:(r + 1) * Lc], outg[1][r * Lc:(r + 1) * Lc])
        o_ref[0, :, r * LANES:(r + 1) * LANES] = y * _silu(gate_ref[0, :, r * LANES:(r + 1) * LANES])


def _swa_ctx(q, kc, vc, sink, gate):
    B, Lc, NQ = q.shape
    G = SWA_KV_HEADS
    R = SWA_HEADS // G
    kern = functools.partial(_swa_ctx_kernel, Lc=Lc, R=R, G=G, scale=SWA_HD ** -0.5 * LOG2E)
    qs = pl.BlockSpec((1, Lc, NQ), lambda b: (b, 0, 0))
    ctx = pl.BlockSpec((1, Lc, LANES), lambda b: (b, 0, 0))
    return pl.pallas_call(
        kern, name="swa_ctx", grid=(B,),
        in_specs=[pl.BlockSpec(memory_space=pltpu.SMEM), qs, ctx, ctx, qs],
        out_specs=qs,
        out_shape=jax.ShapeDtypeStruct((B, Lc, NQ), F32),
        compiler_params=_params(("arbitrary",)),
    )(sink, q, kc, vc, gate)


def _merge_kernel(yh_ref, ym_ref, ys_ref, mg_ref, x_ref, mod_ref, wh, wm, ws, wo, lg, lb, o_ref, *, D, alpha):
    m = (jax.nn.sigmoid(mg_ref[0, :, 0:D]) * _dot(yh_ref[0].astype(BF16), wh[...])
         + jax.nn.sigmoid(mg_ref[0, :, D:2 * D]) * _dot(ym_ref[0].astype(BF16), wm[...])
         + jax.nn.sigmoid(mg_ref[0, :, 2 * D:3 * D]) * _dot(ys_ref[0].astype(BF16), ws[...]))
    out = _dot(m.astype(BF16), wo[...])
    r = alpha * x_ref[0] + mod_ref[0, :, 2 * D:3 * D] * out
    mu = jnp.mean(r, -1, keepdims=True)
    rc = r - mu
    var = jnp.mean(rc * rc, -1, keepdims=True)
    o_ref[0] = rc * lax.rsqrt(var + EPS) * lg[...] + lb[...]


def _merge(yh, ym, ys, mg, x, mod, wh, wm, ws, wo, lg, lb, alpha, tm):
    Bx, S, D = x.shape
    kern = functools.partial(_merge_kernel, D=D, alpha=alpha)
    tok = lambda w: pl.BlockSpec((1, tm, w), lambda b, i: (b, i, 0))
    cs = _const_spec
    return pl.pallas_call(
        kern, name="merge", grid=(Bx, S // tm),
        in_specs=[tok(yh.shape[2]), tok(ym.shape[2]), tok(ys.shape[2]), tok(3 * D), tok(D),
                  pl.BlockSpec((1, 1, 3 * D), lambda b, i: (b, 0, 0)),
                  cs(wh.shape), cs(wm.shape), cs(ws.shape), cs(wo.shape), cs((1, D)), cs((1, D))],
        out_specs=tok(D),
        out_shape=jax.ShapeDtypeStruct((Bx, S, D), F32),
        compiler_params=_params(("parallel", "arbitrary")),
    )(yh, ym, ys, mg, x, mod, wh, wm, ws, wo, lg.reshape(1, D), lb.reshape(1, D))


_PAIR_ORDER = (0, 4, 1, 5, 2, 6, 3, 7)

_IN_COLS = (3 * HY_W, HY_W, MLA_Q_RANK, MLA_KV_RANK, MLA_ROPE, MLA_HEADS * MLA_V,
            SWA_HEADS * SWA_HD, SWA_KV_HEADS * SWA_HD, SWA_KV_HEADS * SWA_HD, SWA_HEADS * SWA_HD)
_SLAB_WIDTHS = (None, 3 * HY_W, HY_W, MLA_Q_RANK, MLA_KV_RANK, LANES, MLA_HEADS * MLA_V,
                SWA_HEADS * SWA_HD, LANES, LANES, SWA_HEADS * SWA_HD)


def _pair_cols(w):
    D = w.shape[0]
    return w.reshape(D, SWA_HEADS, SWA_HD)[:, _PAIR_ORDER, :].reshape(D, SWA_HEADS * SWA_HD)


def _layout_w_in(w_in, D):
    offs = [0]
    for cw in _IN_COLS:
        offs.append(offs[-1] + cw)
    sl = [w_in[:, offs[i]:offs[i + 1]] for i in range(len(_IN_COLS))]
    hy, hyg, cq, ckv, kr, mlag, sq, sk, sv, swag = sl
    mg = w_in[:, offs[-1]:]
    kr_p = jnp.pad(kr, ((0, 0), (MLA_NOPE, LANES - MLA_NOPE - MLA_ROPE)))
    return jnp.concatenate([mg, hy, hyg, cq, ckv, kr_p, mlag, _pair_cols(sq), sk, sv, _pair_cols(swag)],
                           1).astype(BF16)


def _layer_weights(p, D):
    w = {}
    w['w_in'] = _layout_w_in(p['w_in'], D)
    uq = p['mla_w_uq'].reshape(MLA_Q_RANK, MLA_HEADS, MLA_QK)
    w['w_uq'] = jnp.pad(uq, ((0, 0), (0, 0), (0, LANES - MLA_QK))).reshape(MLA_Q_RANK, MLA_HEADS * LANES).astype(BF16)
    ukv = p['mla_w_ukv'].reshape(MLA_KV_RANK, MLA_HEADS, MLA_NOPE + MLA_V)
    w['w_uk'] = jnp.pad(ukv[:, :, :MLA_NOPE], ((0, 0), (0, 0), (0, LANES - MLA_NOPE))).reshape(
        MLA_KV_RANK, MLA_HEADS * LANES).astype(BF16)
    w['w_uv'] = ukv[:, :, MLA_NOPE:].reshape(MLA_KV_RANK, MLA_HEADS * MLA_V).astype(BF16)
    w['w_proj_hy'] = p['w_proj_hy'].astype(BF16)
    w['w_proj_mla'] = p['w_proj_mla'].astype(BF16)
    w['w_proj_swa'] = p['w_proj_swa'].reshape(SWA_HEADS, SWA_HD, D)[_PAIR_ORDER, :, :].reshape(
        SWA_HEADS * SWA_HD, D).astype(BF16)
    w['w_out'] = p['w_out'].astype(BF16)
    return w


def _trunk_layer(xl, xc, cvec, p, tabs, alpha, ctx_out):
    B, S, D = xl.shape
    Lc = xc.shape[1]
    w = _layer_weights(p, D)
    widths = (N_BRANCH * D,) + _SLAB_WIDTHS[1:]

    mod = _ada_mod(cvec, p['w_ada'], p['b_ada'])
    mod_l = mod[:B].reshape(B, 1, 3 * D)
    mod_c = jnp.broadcast_to(mod[B].reshape(1, 1, 3 * D), (B, 1, 3 * D))

    tm_c = min(Lc, 256)
    (mg_l, hy_l, hyg_l, cq_l, ckv_l, kr_l, mlag_l, sq_l, sk_l, sv_l, swag_l) = _in_proj(xl, mod_l, w['w_in'], widths, 512)
    (mg_c, hy_c, hyg_c, cq_c, ckv_c, kr_c, mlag_c, sq_c, sk_c, sv_c, swag_c) = _in_proj(xc, mod_c, w['w_in'], widths, tm_c)

    k_mc, v_mc = _mla_kv(ckv_c, kr_c, p['mla_kv_norm'], w['w_uk'], w['w_uv'], tabs['id_c'], tm_c)
    k_sc = sk_c.astype(BF16)
    v_sc = sv_c.astype(BF16)

    u_l, e_l = _hy_pre(hy_l, hyg_l, p['hy_conv_w'], p['hy_conv_b'], 512)
    h2_l = _hy_filters(S, p)
    y_hy = _hy_long_conv(u_l, e_l, h2_l, p['hy_skip'], tabs['dft'])

    q_ml = _mla_q(cq_l, p['mla_q_norm'], w['w_uq'], tabs['mla'], 512)
    k_ml, v_ml = _mla_kv(ckv_l, kr_l, p['mla_kv_norm'], w['w_uk'], w['w_uv'], tabs['mla'], 512)
    y_mla = _mla_attn(q_ml, k_mc, v_mc, k_ml, v_ml, mlag_l, 512)

    q_sl, k_sl, v_sl = _swa_prep(sq_l, sk_l, sv_l, tabs['swa'], 512)
    hd = LANES // SWA_KV_HEADS
    ones = jnp.broadcast_to(_ones_rows(SWA_VROWS - hd, Lc, BF16)[None], (B, SWA_VROWS - hd, Lc))
    v_sct = jnp.swapaxes(v_sc, 1, 2)
    v_sct = jnp.concatenate([v_sct[:, :hd], ones, v_sct[:, hd:], ones], 1)
    y_swa = _swa_attn(q_sl, k_sl, v_sl, k_sc, v_sct, p['swa_sink'], swag_l, 4)

    xl_new = _merge(y_hy, y_mla, y_swa, mg_l, xl, mod_l, w['w_proj_hy'], w['w_proj_mla'], w['w_proj_swa'],
                    w['w_out'], p['ln_g'], p['ln_b'], alpha, 256)
    if not ctx_out:
        return xl_new, xc

    u_c, e_c = _hy_pre(hy_c, hyg_c, p['hy_conv_w'], p['hy_conv_b'], tm_c)
    h2_c = _hy_filters(Lc, p)
    yc_hy = _hy_ctx_conv(u_c, e_c, h2_c, p['hy_skip'], tabs['dft_c'])
    q_mc = _mla_q(cq_c, p['mla_q_norm'], w['w_uq'], tabs['id_c'], tm_c)
    yc_mla = _mla_attn(q_mc, k_mc, v_mc, None, None, mlag_c, tm_c)
    yc_swa = _swa_ctx(sq_c, k_sc, v_sc, p['swa_sink'], swag_c)
    xc_new = _merge(yc_hy, yc_mla, yc_swa, mg_c, xc, mod_c, w['w_proj_hy'], w['w_proj_mla'], w['w_proj_swa'],
                    w['w_out'], p['ln_g'], p['ln_b'], alpha, tm_c)
    return xl_new, xc_new


def kernel(x, c, ctx, c_ctx, w_ada, b_ada, w_in, hy_conv_w, hy_conv_b, filt_w1, filt_b1, filt_w2, filt_b2,
           filt_w3, filt_b3, filt_freq, filt_w_out, hy_skip, mla_q_norm, mla_w_uq, mla_kv_norm, mla_w_ukv,
           swa_sink, w_proj_hy, w_proj_mla, w_proj_swa, w_out, ln_g, ln_b):
    B, S, D = x.shape
    Lc = ctx.shape[1]
    depth = w_in.shape[0]
    alpha = (2 * depth) ** 0.25
    stacked = dict(w_ada=w_ada, b_ada=b_ada, w_in=w_in, hy_conv_w=hy_conv_w, hy_conv_b=hy_conv_b,
                   filt_w1=filt_w1, filt_b1=filt_b1, filt_w2=filt_w2, filt_b2=filt_b2, filt_w3=filt_w3,
                   filt_b3=filt_b3, filt_freq=filt_freq, filt_w_out=filt_w_out, hy_skip=hy_skip,
                   mla_q_norm=mla_q_norm, mla_w_uq=mla_w_uq, mla_kv_norm=mla_kv_norm, mla_w_ukv=mla_w_ukv,
                   swa_sink=swa_sink, w_proj_hy=w_proj_hy, w_proj_mla=w_proj_mla, w_proj_swa=w_proj_swa,
                   w_out=w_out, ln_g=ln_g, ln_b=ln_b)

    lane = jnp.arange(LANES, dtype=jnp.int32)
    mla_on = (lane >= MLA_NOPE) & (lane < MLA_QK)
    tabs = {
        'mla': _rope_tables(S, jnp.clip(lane - MLA_NOPE, 0, MLA_ROPE - 1), mla_on, MLA_ROPE),
        'swa': _rope_tables(S, lane % SWA_HD, jnp.ones((LANES,), bool), SWA_HD),
        'id_c': _rope_tables(Lc, None, None, None, identity=True),
        'dft': _dft_tables(S),
        'dft_c': _ctx_dft_tables(Lc),
    }
    cvec = jnp.concatenate([c, c_ctx[None, :], jnp.zeros((8 - B - 1, D), F32)], 0)

    xl, xc = x, ctx
    for l in range(depth):
        p = {k: v[l] for k, v in stacked.items()}
        xl, xc = _trunk_layer(xl, xc, cvec, p, tabs, alpha, l < depth - 1)
    return xl
```

```python
import functools
import math

import jax
import jax.numpy as jnp
from jax import lax
from jax.experimental import pallas as pl
from jax.experimental.pallas import tpu as pltpu

F32 = jnp.float32
BF16 = jnp.bfloat16
HIGHEST = lax.Precision.HIGHEST

GRID_W = 64
HY_W = 512
SHORT_K = 3
FILT_BANDS = 16
FILT_W = 64
DECAY_TARGET = 1e-2
FAST_DECAY_PCT = 0.3
SLOW_DECAY_PCT = 1.5
MLA_HEADS = 8
MLA_NOPE = 64
MLA_ROPE = 32
MLA_V = 64
MLA_QK = MLA_NOPE + MLA_ROPE
MLA_Q_RANK = 256
MLA_KV_RANK = 128
SWA_HEADS = 8
SWA_KV_HEADS = 2
SWA_HD = 64
WINDOW = 128
N_BRANCH = 3
ROPE_BASE = 10000.0
EPS = 1e-6
NEG = -1e30
LOG2E = math.log2(math.e)

LANES = 128
DFT_N2 = 128
VMEM_LIMIT = 56 * 1024 * 1024


def _silu(x):
    return x * jax.nn.sigmoid(x)


def _dot(a, b):
    return jnp.dot(a, b, preferred_element_type=F32)


def _dot_nt(a, b):
    return lax.dot_general(a, b, (((1,), (1,)), ((), ())), preferred_element_type=F32)


def _params(sem):
    return pltpu.CompilerParams(dimension_semantics=sem, vmem_limit_bytes=VMEM_LIMIT)


def _const_spec(shape):
    nd = len(shape)
    return pl.BlockSpec(shape, lambda *_: (0,) * nd)


def _ada_kernel(c_ref, w_ref, b_ref, o_ref):
    a = _silu(c_ref[...])
    o_ref[...] = jnp.dot(a, w_ref[...], precision=HIGHEST, preferred_element_type=F32) + b_ref[...]


def _ada_mod(cvec, w_ada, b_ada, layer):
    R, D = cvec.shape
    N = w_ada.shape[2]
    tn = 768
    return pl.pallas_call(
        _ada_kernel, name="ada_mod", grid=(N // tn,),
        in_specs=[_const_spec((R, D)), pl.BlockSpec((None, D, tn), lambda j: (layer, 0, j)),
                  pl.BlockSpec((1, tn), lambda j: (0, j))],
        out_specs=pl.BlockSpec((R, tn), lambda j: (0, j)),
        out_shape=jax.ShapeDtypeStruct((R, N), F32),
        compiler_params=_params(("arbitrary",)),
    )(cvec, w_ada, b_ada.reshape(1, N))


def _in_proj_kernel(x_ref, mod_ref, w_ref, *o_refs, widths, D, tm, sub):
    shift = mod_ref[0, :, 0:D]
    scale1 = 1.0 + mod_ref[0, :, D:2 * D]
    us = []
    for r in range(0, tm, sub):
        x = x_ref[0, r:r + sub, :]
        mu = jnp.mean(x, -1, keepdims=True)
        xc = x - mu
        var = jnp.mean(xc * xc, -1, keepdims=True)
        us.append(((xc * lax.rsqrt(var + EPS)) * scale1 + shift).astype(BF16))
    off = 0
    for o_ref, wd in zip(o_refs, widths):
        for k, u in enumerate(us):
            o_ref[0, k * sub:(k + 1) * sub, :] = _dot(u, w_ref[:, off:off + wd]).astype(o_ref.dtype)
        off += wd


def _in_proj(x, mod, w, widths, tm):
    Bx, S, D = x.shape
    P = w.shape[1]
    kern = functools.partial(_in_proj_kernel, widths=tuple(widths), D=D, tm=tm, sub=min(tm, 256))
    return pl.pallas_call(
        kern, name="in_proj", grid=(Bx, S // tm),
        in_specs=[pl.BlockSpec((1, tm, D), lambda b, i: (b, i, 0)),
                  pl.BlockSpec((1, 1, 3 * D), lambda b, i: (b, 0, 0)),
                  pl.BlockSpec((D, P), lambda b, i: (0, 0), pipeline_mode=pl.Buffered(1))],
        out_specs=[pl.BlockSpec((1, tm, wd), lambda b, i: (b, i, 0)) for wd in widths],
        out_shape=[jax.ShapeDtypeStruct((Bx, S, wd), F32) for wd in widths],
        compiler_params=_params(("parallel", "arbitrary")),
    )(x, mod, w)


def _filter_kernel(z_ref, t_ref, w1, b1, w2, b2, w3, b3, fr, wo, dl_ref, o_ref, *, tl, C):
    hp = lambda a, b: jnp.dot(a, b, precision=HIGHEST, preferred_element_type=F32)
    f = fr[...]
    h = jnp.sin(f * (hp(z_ref[...], w1[...]) + b1[...]))
    h = jnp.sin(f * (hp(h, w2[...]) + b2[...]))
    h = jnp.sin(f * (hp(h, w3[...]) + b3[...]))
    dl = jnp.abs(dl_ref[...])
    half = tl // 2
    for s in range(2):
        rows = slice(s * half, (s + 1) * half)
        o = hp(h, wo[s])
        decay = jnp.exp(-t_ref[rows, :] * dl)
        o_ref[rows, 0:C] = o[:, 0:C] * decay
        bwd = o[:, C:2 * C] * decay
        if s == 0:
            row = pl.program_id(0) * tl + lax.broadcasted_iota(jnp.int32, (half, 1), 0)
            bwd = jnp.where(row == 0, 0.0, bwd)
        o_ref[rows, C:2 * C] = bwd


def _hy_filters(L, p):
    C = HY_W
    FP = LANES
    FH = FP // 2
    t = jnp.linspace(0.0, 1.0, L, dtype=F32)[:, None]
    w = 2.0 * math.pi * jnp.arange(L, dtype=F32) / L
    f = jnp.linspace(1e-4, FILT_BANDS - 1, FILT_BANDS, dtype=F32)
    ang = w[:, None] * f[None, :]
    z = jnp.concatenate([t, jnp.cos(ang), -jnp.sin(ang)], -1)
    z = jnp.pad(z, ((0, 0), (0, FH - z.shape[1])))
    tl = min(L, 512)
    z = z.reshape(L // tl, 2, tl // 2, FH).transpose(0, 2, 1, 3).reshape(L // 2, FP)
    pad_h = lambda a: jnp.pad(a, ((0, FH - a.shape[0]), (0, FH - a.shape[1])))
    bdiag = lambda a: jnp.kron(jnp.eye(2, dtype=F32), pad_h(a))
    pad_v = lambda a: jnp.tile(jnp.pad(a, (0, FH - a.shape[0])), 2).reshape(1, FP)
    w1, w2, w3 = bdiag(p['filt_w1']), bdiag(p['filt_w2']), bdiag(p['filt_w3'])
    wo_h = jnp.pad(p['filt_w_out'], ((0, FH - FILT_W), (0, 0)))
    zeros = jnp.zeros_like(wo_h)
    wo = jnp.stack([jnp.concatenate([wo_h, zeros], 0), jnp.concatenate([zeros, wo_h], 0)], 0)
    min_decay = math.log(DECAY_TARGET) / SLOW_DECAY_PCT
    max_decay = math.log(DECAY_TARGET) / FAST_DECAY_PCT
    deltas = jnp.linspace(min_decay, max_decay, C, dtype=F32).reshape(1, C)
    kern = functools.partial(_filter_kernel, tl=tl, C=C)
    cs = _const_spec
    return pl.pallas_call(
        kern, name="hy_filter", grid=(L // tl,),
        in_specs=[pl.BlockSpec((tl // 2, FP), lambda i: (i, 0)), pl.BlockSpec((tl, 1), lambda i: (i, 0)),
                  cs((FP, FP)), cs((1, FP)), cs((FP, FP)), cs((1, FP)), cs((FP, FP)), cs((1, FP)),
                  cs((1, FP)), cs((2, FP, 2 * C)), cs((1, C))],
        out_specs=pl.BlockSpec((tl, 2 * C), lambda i: (i, 0)),
        out_shape=jax.ShapeDtypeStruct((L, 2 * C), F32),
        compiler_params=_params(("arbitrary",)),
    )(z, t, w1, pad_v(p['filt_b1']), w2, pad_v(p['filt_b2']), w3, pad_v(p['filt_b3']),
      pad_v(p['filt_freq']), wo, deltas)


def _hy_pre_kernel(x_ref, xp_ref, xn_ref, g_ref, w_ref, b_ref, u_ref, e_ref, *, ts, C):
    i = pl.program_id(1)
    nt = pl.num_programs(1)
    x = x_ref[0]
    prev_row = xp_ref[0, 7:8, :] * jnp.where(i > 0, 1.0, 0.0)
    next_row = xn_ref[0, 0:1, :] * jnp.where(i < nt - 1, 1.0, 0.0)
    rid = lax.broadcasted_iota(jnp.int32, (ts, 1), 0)
    xm = jnp.where(rid == 0, prev_row, pltpu.roll(x, 1, 0))
    xq = jnp.where(rid == ts - 1, next_row, pltpu.roll(x, ts - 1, 0))
    z = b_ref[...] + xm * w_ref[0:1, :] + x * w_ref[1:2, :] + xq * w_ref[2:3, :]
    u_ref[0] = z[:, 2 * C:3 * C] * z[:, C:2 * C]
    e_ref[0] = z[:, 0:C] * _silu(g_ref[0])


def _hy_pre(hy, hyg, conv_w, conv_b, ts):
    Bx, S, C3 = hy.shape
    C = C3 // 3
    nb8 = S // 8
    r = ts // 8
    w8 = jnp.pad(conv_w, ((0, 8 - SHORT_K), (0, 0)))
    kern = functools.partial(_hy_pre_kernel, ts=ts, C=C)
    return pl.pallas_call(
        kern, name="hy_pre", grid=(Bx, S // ts),
        in_specs=[pl.BlockSpec((1, ts, C3), lambda b, i: (b, i, 0)),
                  pl.BlockSpec((1, 8, C3), lambda b, i: (b, jnp.maximum(i * r - 1, 0), 0)),
                  pl.BlockSpec((1, 8, C3), lambda b, i: (b, jnp.minimum((i + 1) * r, nb8 - 1), 0)),
                  pl.BlockSpec((1, ts, C), lambda b, i: (b, i, 0)),
                  _const_spec((8, C3)), _const_spec((1, C3))],
        out_specs=[pl.BlockSpec((1, ts, C), lambda b, i: (b, i, 0))] * 2,
        out_shape=[jax.ShapeDtypeStruct((Bx, S, C), F32)] * 2,
        compiler_params=_params(("parallel", "arbitrary")),
    )(hy, hy, hy, hyg, w8, conv_b.reshape(1, C3))


def _dft_tables(L):
    n = 2 * L
    N2 = DFT_N2
    N1 = n // N2
    H1 = N1 // 2
    k1 = jnp.arange(N1, dtype=jnp.int32)
    t1 = jnp.arange(H1, dtype=jnp.int32)
    ang = (2.0 * math.pi / N1) * ((k1[:, None] * t1[None, :]) % N1).astype(F32)
    fa = jnp.stack([jnp.cos(ang), -jnp.sin(ang)], 1).reshape(2 * N1, H1)
    fai = jnp.concatenate([jnp.cos(ang), -jnp.sin(ang)], 0).T * (1.0 / n)
    k2 = jnp.arange(N2, dtype=jnp.int32)
    t2 = jnp.arange(N2, dtype=jnp.int32)
    a2 = (2.0 * math.pi / N2) * ((k2[:, None] * t2[None, :]) % N2).astype(F32)
    aw = (2.0 * math.pi / n) * ((k1[:, None] * t2[None, :]) % n).astype(F32)
    fr, fi = jnp.cos(a2)[None], -jnp.sin(a2)[None]
    wr, wi = jnp.cos(aw)[:, None, :], -jnp.sin(aw)[:, None, :]
    cr, ci = fr * wr - fi * wi, fr * wi + fi * wr
    mf = jnp.concatenate([jnp.concatenate([cr, -ci], 2), jnp.concatenate([ci, cr], 2)], 1)
    return fa.astype(BF16), fai.astype(BF16), mf.astype(BF16)


DFT_TT = 16
DFT_KB = 8
SUBLANES = 8
DFT_PITCH_B = 24
DFT_PITCH_A = 40


def _halves(shape_fn, idx_fn):
    return [pl.BlockSpec(shape_fn(SUBLANES), functools.partial(idx_fn, half=h)) for h in range(2)]


def _dft_a_fwd_kernel(xlo_ref, xhi_ref, f_ref, o_ref):
    f = f_ref[...]
    for j in range(DFT_TT):
        src = xlo_ref if j < SUBLANES else xhi_ref
        o_ref[0, j] = _dot(f, src[0, :, j % SUBLANES, :].astype(BF16)).astype(o_ref.dtype)


def _dft_a_fwd(x4, fa):
    Bx, H1, N2, C = x4.shape
    R = fa.shape[0]
    tok = _halves(lambda r: (1, H1, r, C), lambda b, j, half: (b, 0, 2 * j + half, 0))
    return pl.pallas_call(
        _dft_a_fwd_kernel, name="dft_a_fwd", grid=(Bx, N2 // DFT_TT),
        in_specs=tok + [_const_spec((R, H1))],
        out_specs=pl.BlockSpec((1, DFT_TT, R, C), lambda b, j: (b, j, 0, 0)),
        out_shape=jax.ShapeDtypeStruct((Bx, N2, R, C), BF16),
        compiler_params=_params(("parallel", "arbitrary")),
    )(x4, x4, fa)


def _stage_b_operand(scr, j):
    return jnp.concatenate([scr[:, 2 * j, :], scr[:, 2 * j + 1, :]], 0).astype(BF16)


def _dft_b_filter_kernel(a_ref, m_ref, k_ref, scr, *, kb, C, N2):
    scr[:, 0:2 * kb, :] = a_ref[0].astype(F32)
    for j in range(kb):
        h = _dot(m_ref[j], _stage_b_operand(scr, j))
        k_ref[j, 0:N2, :] = h[0:N2, 0:C] + h[0:N2, C:2 * C]
        k_ref[j, N2:2 * N2, :] = h[N2:2 * N2, 0:C] - h[N2:2 * N2, C:2 * C]


def _dft_b_filter(ah, mf):
    _, N2, R, C2 = ah.shape
    N1, C, kb = R // 2, C2 // 2, DFT_KB
    kern = functools.partial(_dft_b_filter_kernel, kb=kb, C=C, N2=N2)
    return pl.pallas_call(
        kern, name="dft_b_filter", grid=(N1 // kb,),
        in_specs=[pl.BlockSpec((1, N2, 2 * kb, C2), lambda i: (0, 0, i, 0)),
                  pl.BlockSpec((kb, 2 * N2, 2 * N2), lambda i: (i, 0, 0))],
        out_specs=pl.BlockSpec((kb, 2 * N2, C), lambda i: (i, 0, 0)),
        out_shape=jax.ShapeDtypeStruct((N1, 2 * N2, C), F32),
        scratch_shapes=[pltpu.VMEM((N2, DFT_PITCH_B, C2), F32)],
        compiler_params=_params(("arbitrary",)),
    )(ah, mf)


def _dft_mid_kernel(a_ref, mf_ref, k_ref, g_ref, scr, *, kb, nb, C, N2):
    for b in range(nb):
        scr[:, 0:2 * kb, :] = a_ref[b].astype(F32)
        for j in range(kb):
            kre = k_ref[j, 0:N2, :]
            kim = k_ref[j, N2:2 * N2, :]
            y = _dot(mf_ref[j], _stage_b_operand(scr, j))
            yre, yim = y[0:N2], y[N2:2 * N2]
            z = jnp.concatenate([yre * kre - yim * kim, yre * kim + yim * kre], 0).astype(BF16)
            g = lax.dot_general(mf_ref[j], z, (((0,), (0,)), ((), ())), preferred_element_type=F32)
            g_ref[b, j] = g.astype(g_ref.dtype)


def _dft_mid(a, mf, kf):
    Bx, N2, R, C = a.shape
    N1, kb = R // 2, DFT_KB
    kern = functools.partial(_dft_mid_kernel, kb=kb, nb=Bx, C=C, N2=N2)
    return pl.pallas_call(
        kern, name="dft_mid", grid=(N1 // kb,),
        in_specs=[pl.BlockSpec((Bx, N2, 2 * kb, C), lambda i: (0, 0, i, 0)),
                  pl.BlockSpec((kb, 2 * N2, 2 * N2), lambda i: (i, 0, 0)),
                  pl.BlockSpec((kb, 2 * N2, C), lambda i: (i, 0, 0))],
        out_specs=pl.BlockSpec((Bx, kb, 2 * N2, C), lambda i: (0, i, 0, 0)),
        out_shape=jax.ShapeDtypeStruct((Bx, N1, 2 * N2, C), BF16),
        scratch_shapes=[pltpu.VMEM((N2, DFT_PITCH_B, C), F32)],
        compiler_params=_params(("arbitrary",)),
    )(a, mf, kf)


def _dft_a_inv_kernel(g_ref, f_ref, ulo_ref, uhi_ref, elo_ref, ehi_ref, s_ref, o_ref, scr):
    f = f_ref[...]
    for ri in range(2):
        scr[:, ri * DFT_TT:(ri + 1) * DFT_TT, :] = g_ref[0, :, ri].astype(F32)
    for j in range(DFT_TT):
        g = jnp.concatenate([scr[:, j, :], scr[:, DFT_TT + j, :]], 0).astype(BF16)
        y = _dot(f, g)
        u_ref, e_ref = (ulo_ref, elo_ref) if j < SUBLANES else (uhi_ref, ehi_ref)
        jj = j % SUBLANES
        o_ref[0, :, j, :] = (y + u_ref[0, :, jj, :] * s_ref[...]) * e_ref[0, :, jj, :]


def _dft_a_inv(g5, fai, u4, e4, skip):
    Bx, N1, _, N2, C = g5.shape
    H1 = fai.shape[0]
    tok = _halves(lambda r: (1, H1, r, C), lambda b, j, half: (b, 0, 2 * j + half, 0))
    return pl.pallas_call(
        _dft_a_inv_kernel, name="dft_a_inv", grid=(Bx, N2 // DFT_TT),
        in_specs=[pl.BlockSpec((1, N1, 2, DFT_TT, C), lambda b, j: (b, 0, 0, j, 0)), _const_spec((H1, 2 * N1))]
        + tok + tok + [_const_spec((1, C))],
        out_specs=pl.BlockSpec((1, H1, DFT_TT, C), lambda b, j: (b, 0, j, 0)),
        out_shape=jax.ShapeDtypeStruct((Bx, H1, N2, C), F32),
        scratch_shapes=[pltpu.VMEM((N1, DFT_PITCH_A, C), F32)],
        compiler_params=_params(("parallel", "arbitrary")),
    )(g5, fai, u4, u4, e4, e4, skip)


def _hy_long_conv(u, e, h2, skip, tabs):
    B, L, C = u.shape
    fa, fai, mf = tabs
    N2 = DFT_N2
    N1 = 2 * L // N2
    H1 = N1 // 2
    kf = _dft_b_filter(_dft_a_fwd(h2.reshape(1, H1, N2, 2 * C), fa), mf)
    u4 = u.reshape(B, H1, N2, C)
    g = _dft_mid(_dft_a_fwd(u4, fa), mf, kf)
    y = _dft_a_inv(g.reshape(B, N1, 2, N2, C), fai, u4, e.reshape(B, H1, N2, C), skip.reshape(1, C))
    return y.reshape(B, L, C)


def _ctx_dft_tables(Lc):
    n = 2 * Lc
    k = jnp.arange(n, dtype=jnp.int32)
    t = jnp.arange(Lc, dtype=jnp.int32)
    ang = (2.0 * math.pi / n) * ((k[:, None] * t[None, :]) % n).astype(F32)
    fc = jnp.concatenate([jnp.cos(ang), -jnp.sin(ang)], 0)
    fi = fc.T * (1.0 / n)
    return fc.astype(BF16), fi.astype(BF16)


def _hy_ctx_conv_kernel(u_ref, e_ref, h_ref, fc_ref, fi_ref, s_ref, o_ref, *, n, C):
    u = u_ref[0]
    fc = fc_ref[...]
    uf = _dot(fc, u.astype(BF16))
    hf = _dot(fc, h_ref[...].astype(BF16))
    kre = hf[0:n, 0:C] + hf[0:n, C:2 * C]
    kim = hf[n:2 * n, 0:C] - hf[n:2 * n, C:2 * C]
    ure, uim = uf[0:n], uf[n:2 * n]
    z = jnp.concatenate([ure * kre - uim * kim, ure * kim + uim * kre], 0).astype(BF16)
    y = _dot(fi_ref[...], z)
    o_ref[0] = (y + u * s_ref[...]) * e_ref[0]


def _hy_ctx_conv(u, e, h2, skip, tabs):
    B, Lc, C = u.shape
    fc, fi = tabs
    n = 2 * Lc
    kern = functools.partial(_hy_ctx_conv_kernel, n=n, C=C)
    blk = pl.BlockSpec((1, Lc, C), lambda b: (b, 0, 0))
    return pl.pallas_call(
        kern, name="hy_ctx_conv", grid=(B,),
        in_specs=[blk, blk, _const_spec((Lc, 2 * C)), _const_spec((2 * n, Lc)), _const_spec((Lc, 2 * n)),
                  _const_spec((1, C))],
        out_specs=blk,
        out_shape=jax.ShapeDtypeStruct((B, Lc, C), F32),
        compiler_params=_params(("arbitrary",)),
    )(u, e, h2, fc, fi, skip.reshape(1, C))


def _rope_tables(S, lane_dim, lane_on, head_rot, identity=False):
    if identity:
        return (jnp.ones((S, LANES), F32), jnp.zeros((S, LANES), F32), jnp.zeros((S, LANES), F32))
    seg_w = head_rot // 2
    half = seg_w // 2
    seg = lane_dim // seg_w
    w = lane_dim % seg_w
    first = w < half
    inv = ROPE_BASE ** (-(w % half).astype(F32) / half)
    nrow = S // GRID_W
    ang_r = jnp.arange(nrow, dtype=jnp.int32).astype(F32)[:, None] * inv[None, :]
    ang_c = jnp.arange(GRID_W, dtype=jnp.int32).astype(F32)[:, None] * inv[None, :]

    def expand(fn):
        tab = jnp.where((seg == 0)[None, None, :], fn(ang_r)[:, None, :], fn(ang_c)[None, :, :])
        return tab.reshape(S, LANES)

    cos, sin = expand(jnp.cos), expand(jnp.sin)
    on = lane_on[None, :]
    c = jnp.where(on, cos, 1.0)
    sm = jnp.where(on & first[None, :], -sin, 0.0)
    sp = jnp.where(on & (~first)[None, :], sin, 0.0)
    return c, sm, sp


def _rope(x, c, sm, sp, hh):
    return x * c + pltpu.roll(x, LANES - hh, 1) * sm + pltpu.roll(x, hh, 1) * sp


MLA_VROWS = 80

def _mla_q_kernel(cq_ref, g_ref, w_ref, c_ref, sm_ref, sp_ref, o_ref, *, nh, scale, hh):
    x = cq_ref[0]
    xn = x * lax.rsqrt(jnp.mean(x * x, -1, keepdims=True) + EPS) * g_ref[...]
    q = _dot(xn.astype(BF16), w_ref[...])
    c, sm, sp = c_ref[...], sm_ref[...], sp_ref[...]
    for h in range(nh):
        qh = _rope(q[:, h * LANES:(h + 1) * LANES], c, sm, sp, hh) * scale
        o_ref[0, h * LANES:(h + 1) * LANES, :] = qh.T.astype(o_ref.dtype)


def _mla_q(cq, qnorm, w_uq_p, tabs, tm):
    Bx, S, R = cq.shape
    N = w_uq_p.shape[1]
    kern = functools.partial(_mla_q_kernel, nh=MLA_HEADS, scale=MLA_QK ** -0.5 * LOG2E,
                             hh=MLA_ROPE // 4)
    tab = pl.BlockSpec((tm, LANES), lambda b, i: (i, 0))
    return pl.pallas_call(
        kern, name="mla_q", grid=(Bx, S // tm),
        in_specs=[pl.BlockSpec((1, tm, R), lambda b, i: (b, i, 0)), _const_spec((1, R)), _const_spec((R, N)),
                  tab, tab, tab],
        out_specs=pl.BlockSpec((1, N, tm), lambda b, i: (b, 0, i)),
        out_shape=jax.ShapeDtypeStruct((Bx, N, S), BF16),
        compiler_params=_params(("parallel", "arbitrary")),
    )(cq, qnorm.reshape(1, R), w_uq_p, *tabs)


def _mla_kv_kernel(ckv_ref, kr_ref, g_ref, wk_ref, wv_ref, c_ref, sm_ref, sp_ref, k_ref, vt_ref, *, nh, hh, tm):
    x = ckv_ref[0]
    xn = (x * lax.rsqrt(jnp.mean(x * x, -1, keepdims=True) + EPS) * g_ref[...]).astype(BF16)
    kn = _dot(xn, wk_ref[...])
    krr = _rope(kr_ref[0], c_ref[...], sm_ref[...], sp_ref[...], hh)
    for h in range(nh):
        k_ref[0, :, h * LANES:(h + 1) * LANES] = (kn[:, h * LANES:(h + 1) * LANES] + krr).astype(k_ref.dtype)
    vt = _dot(xn, wv_ref[...]).T
    pad = MLA_VROWS - MLA_V
    ones_rows = jnp.where(lax.broadcasted_iota(jnp.int32, (pad, tm), 0) == 0, 1.0, 0.0).astype(vt_ref.dtype)
    for h in range(nh):
        vt_ref[0, 0, h * MLA_VROWS:h * MLA_VROWS + MLA_V, :] = vt[h * MLA_V:(h + 1) * MLA_V].astype(vt_ref.dtype)
        vt_ref[0, 0, h * MLA_VROWS + MLA_V:(h + 1) * MLA_VROWS, :] = ones_rows


def _mla_kv(ckv, kr, kvnorm, wk_p, wv, tabs, tm):
    Bx, S, R = ckv.shape
    NK, NV = wk_p.shape[1], wv.shape[1]
    kern = functools.partial(_mla_kv_kernel, nh=MLA_HEADS, hh=MLA_ROPE // 4, tm=tm)
    tab = pl.BlockSpec((tm, LANES), lambda b, i: (i, 0))
    VR = MLA_HEADS * MLA_VROWS
    return pl.pallas_call(
        kern, name="mla_kv", grid=(Bx, S // tm),
        in_specs=[pl.BlockSpec((1, tm, R), lambda b, i: (b, i, 0)),
                  pl.BlockSpec((1, tm, LANES), lambda b, i: (b, i, 0)),
                  _const_spec((1, R)), _const_spec((R, NK)), _const_spec((R, NV)), tab, tab, tab],
        out_specs=[pl.BlockSpec((1, tm, NK), lambda b, i: (b, i, 0)),
                   pl.BlockSpec((1, 1, VR, tm), lambda b, i: (b, i, 0, 0))],
        out_shape=[jax.ShapeDtypeStruct((Bx, S, NK), BF16), jax.ShapeDtypeStruct((Bx, S // tm, VR, tm), BF16)],
        compiler_params=_params(("parallel", "arbitrary")),
    )(ckv, kr, kvnorm.reshape(1, R), wk_p, wv, *tabs)


def _mla_attn_kernel(*refs, tq, tk, n_chunks):
    n_in = 7 if n_chunks else 5
    if n_chunks:
        qt_ref, kc_ref, vct_ref, k_ref, vt_ref, gate_ref, o_ref = refs[:n_in]
    else:
        qt_ref, kc_ref, vct_ref, gate_ref, o_ref = refs[:n_in]
    acc_scr = refs[n_in:n_in + 2]
    VR = MLA_VROWS
    qs = [qt_ref[0, j * LANES:(j + 1) * LANES, :] for j in range(2)]

    def accumulate(j, m, st, mx, vblk):
        m_new = mx if m is None else jnp.maximum(m, mx)
        pt = jnp.exp2(st - m_new).astype(BF16)
        pv = _dot(vblk, pt)
        acc_scr[j][...] = pv if m is None else jnp.exp2(m - m_new) * acc_scr[j][...] + pv
        return m_new

    m = []
    for j in range(2):
        st = _dot(kc_ref[0, :, j * LANES:(j + 1) * LANES], qs[j])
        m.append(accumulate(j, None, st, jnp.max(st, 0, keepdims=True), vct_ref[0, 0, j * VR:(j + 1) * VR, :]))

    if n_chunks:
        s_scr = refs[n_in + 2:]

        def scores(j, c, slot):
            st = pl.multiple_of(c * tk, tk)
            blk = _dot(k_ref[0, pl.ds(st, tk), j * LANES:(j + 1) * LANES], qs[j])
            s_scr[2 * j + slot][...] = blk
            return jnp.max(blk, 0, keepdims=True)

        def consume(j, c, slot, mj, mxj):
            return accumulate(j, mj, s_scr[2 * j + slot][...], mxj, vt_ref[0, c, j * VR:(j + 1) * VR, :])

        mx = [scores(j, 0, 0) for j in range(2)]

        unroll = 4 if n_chunks % 4 == 0 else 2

        def steps(c, m, mx, last):
            for s in range(unroll):
                slot = s % 2
                nxt = mx
                if not (last and s == unroll - 1):
                    nxt = [scores(j, c + s + 1, 1 - slot) for j in range(2)]
                for j in range(2):
                    m[j] = consume(j, c + s, slot, m[j], mx[j])
                mx = nxt
            return m, mx

        def body(ci, carry):
            m0, x0, m1, x1 = carry
            m, mx = steps(unroll * ci, [m0, m1], [x0, x1], False)
            return m[0], mx[0], m[1], mx[1]

        m0, x0, m1, x1 = lax.fori_loop(0, n_chunks // unroll - 1, body, (m[0], mx[0], m[1], mx[1]))
        steps(n_chunks - unroll, [m0, m1], [x0, x1], True)
    yt = jnp.concatenate([acc_scr[j][0:MLA_V, :] / acc_scr[j][MLA_V:MLA_V + 1, :] for j in range(2)], 0)
    o_ref[0] = yt.T * _silu(gate_ref[0])


def _mla_attn(qt, kc, vct, k, vt, gate, tq):
    B, _, S = qt.shape
    Lc = kc.shape[1]
    npair = MLA_HEADS // 2
    VR2 = 2 * MLA_VROWS
    n_chunks, tk = (0, 0) if k is None else (vt.shape[1], vt.shape[3])
    kern = functools.partial(_mla_attn_kernel, tq=tq, tk=tk, n_chunks=n_chunks)
    in_specs = [pl.BlockSpec((1, 2 * LANES, tq), lambda b, p, i: (b, p, i)),
                pl.BlockSpec((1, Lc, 2 * LANES), lambda b, p, i: (b, 0, p)),
                pl.BlockSpec((1, 1, VR2, Lc), lambda b, p, i: (b, 0, p, 0))]
    args = [qt, kc, vct]
    if n_chunks:
        Sk = k.shape[1]
        in_specs += [pl.BlockSpec((1, Sk, 2 * LANES), lambda b, p, i: (b, 0, p)),
                     pl.BlockSpec((1, n_chunks, VR2, tk), lambda b, p, i: (b, 0, p, 0))]
        args += [k, vt]
    in_specs.append(pl.BlockSpec((1, tq, LANES), lambda b, p, i: (b, i, p)))
    args.append(gate)
    return pl.pallas_call(
        kern, name="mla_attn", grid=(B, npair, S // tq),
        in_specs=in_specs,
        out_specs=pl.BlockSpec((1, tq, LANES), lambda b, p, i: (b, i, p)),
        out_shape=jax.ShapeDtypeStruct((B, S, npair * LANES), F32),
        scratch_shapes=[pltpu.VMEM((MLA_VROWS, tq), F32)] * 2 + [pltpu.VMEM((tk, tq), F32)] * (4 if n_chunks else 0),
        compiler_params=_params(("parallel", "arbitrary", "arbitrary")),
    )(*args)


SWA_VROWS = 80


def _ones_rows(n, width, dtype):
    return jnp.where(lax.broadcasted_iota(jnp.int32, (n, width), 0) == 0, 1.0, 0.0).astype(dtype)


def _swa_prep_kernel(q_ref, k_ref, v_ref, c_ref, sm_ref, sp_ref, qo_ref, ko_ref, vo_ref, *, scale, hh, nq, tm):
    c, sm, sp = c_ref[...], sm_ref[...], sp_ref[...]
    for r in range(nq):
        qr = _rope(q_ref[0, :, r * LANES:(r + 1) * LANES], c, sm, sp, hh) * scale
        qo_ref[0, r * LANES:(r + 1) * LANES, :] = qr.T.astype(qo_ref.dtype)
    ko_ref[0] = _rope(k_ref[0], c, sm, sp, hh).astype(ko_ref.dtype)
    vt = v_ref[0].T
    hd = LANES // SWA_KV_HEADS
    ones = _ones_rows(SWA_VROWS - hd, tm, vo_ref.dtype)
    for g in range(SWA_KV_HEADS):
        vo_ref[0, g * SWA_VROWS:g * SWA_VROWS + hd, :] = vt[g * hd:(g + 1) * hd].astype(vo_ref.dtype)
        vo_ref[0, g * SWA_VROWS + hd:(g + 1) * SWA_VROWS, :] = ones


def _swa_prep(sq, sk, sv, tabs, tm):
    Bx, S, NQ = sq.shape
    VR = SWA_KV_HEADS * SWA_VROWS
    kern = functools.partial(_swa_prep_kernel, scale=SWA_HD ** -0.5 * LOG2E, hh=SWA_HD // 4, nq=NQ // LANES, tm=tm)
    tab = pl.BlockSpec((tm, LANES), lambda b, i: (i, 0))
    qs = pl.BlockSpec((1, tm, NQ), lambda b, i: (b, i, 0))
    ks = pl.BlockSpec((1, tm, LANES), lambda b, i: (b, i, 0))
    return pl.pallas_call(
        kern, name="swa_prep", grid=(Bx, S // tm),
        in_specs=[qs, ks, ks, tab, tab, tab],
        out_specs=[pl.BlockSpec((1, NQ, tm), lambda b, i: (b, 0, i)), ks,
                   pl.BlockSpec((1, VR, tm), lambda b, i: (b, 0, i))],
        out_shape=[jax.ShapeDtypeStruct((Bx, NQ, S), BF16), jax.ShapeDtypeStruct((Bx, S, LANES), BF16),
                   jax.ShapeDtypeStruct((Bx, VR, S), BF16)],
        compiler_params=_params(("parallel", "arbitrary")),
    )(sq, sk, sv, *tabs)


def _sink_column(sink_ref, g, R, W):
    rid = lax.broadcasted_iota(jnp.int32, (R * W, 1), 0)
    col = jnp.full((R * W, 1), sink_ref[g * R + R - 1], F32)
    for r in range(R - 2, -1, -1):
        col = jnp.where(rid < (r + 1) * W, sink_ref[g * R + r], col)
    return col


def _swa_group(qg, sk2, blocks):
    ss = []
    for kk, _, bias in blocks:
        s = _dot_nt(qg, kk)
        ss.append(s if bias is None else s + bias)
    m = sk2
    for s in ss:
        m = jnp.maximum(m, jnp.max(s, -1, keepdims=True))
    o = None
    for s, (_, vv, _) in zip(ss, blocks):
        pv = _dot(jnp.exp2(s - m).astype(BF16), vv)
        o = pv if o is None else o + pv
    den = pltpu.roll(o, LANES // 2, 1) + jnp.exp2(sk2 - m)
    return o / den


def _group_select(x, g, fill):
    lane = lax.broadcasted_iota(jnp.int32, (1, LANES), 1)
    sel = (lane < LANES // 2) if g == 0 else (lane >= LANES // 2)
    return jnp.where(sel, x, jnp.full_like(x, fill))


def _swa_attn_kernel(sink_ref, qt_ref, kp_ref, km_ref, kn_ref, vp_ref, vm_ref, vn_ref, kc_ref, vct_ref, gate_ref,
                     o_ref, *s_scr, bpt, W, R, G, Lc):
    i = pl.program_id(1)
    nblk = pl.num_programs(1) * bpt
    hd = LANES // G
    VR = SWA_VROWS
    kcat = jnp.concatenate([kp_ref[0], km_ref[0], kn_ref[0]], 0)
    vcat = jnp.concatenate([vp_ref[0], vm_ref[0], vn_ref[0]], 1)
    kc = kc_ref[0]
    nk = Lc + 3 * W
    kj = lax.broadcasted_iota(jnp.int32, (nk, R * W), 0) - Lc
    qi = lax.broadcasted_iota(jnp.int32, (nk, R * W), 1) % W
    band_bias = jnp.where((kj < 0) | (jnp.abs(W + qi - kj) <= W), 0.0, NEG)
    head = lax.broadcasted_iota(jnp.int32, (1, R * W), 1) // W
    row = lax.broadcasted_iota(jnp.int32, (LANES, 1), 0)
    sk2 = []
    for g in range(G):
        sk = jnp.full((1, R * W), sink_ref[g * R + R - 1], F32)
        for r in range(R - 2, -1, -1):
            sk = jnp.where(head <= r, sink_ref[g * R + r], sk)
        sk2.append(sk * LOG2E)
    kj_col = lax.broadcasted_iota(jnp.int32, (nk, 1), 0) - Lc
    items = [(jb, g) for jb in range(bpt) for g in range(G)]

    def scores(n):
        jb, g = items[n]
        gblk = i * bpt + jb
        keys = jnp.concatenate([kc, kcat[jb * W:(jb + 3) * W]], 0)
        pen_prev = jnp.where(gblk > 0, 0.0, NEG)
        pen_next = jnp.where(gblk < nblk - 1, 0.0, NEG)
        pen = jnp.where((kj_col >= 0) & (kj_col < W), pen_prev, jnp.where(kj_col >= 2 * W, pen_next, 0.0))
        sel = (row >= g * hd) & (row < (g + 1) * hd)
        tiles = [qt_ref[0, r * LANES:(r + 1) * LANES, jb * W:(jb + 1) * W] for r in range(R)]
        qg = jnp.concatenate([jnp.where(sel, t, jnp.zeros_like(t)) for t in tiles], 1)
        st = (_dot(keys, qg) + band_bias) + pen
        s_scr[n % 2][...] = st
        return jnp.max(st, 0, keepdims=True)

    def consume(n, mx):
        jb, g = items[n]
        m = jnp.maximum(mx, sk2[g])
        pt = jnp.exp2(s_scr[n % 2][...] - m).astype(BF16)
        o = (_dot(vct_ref[0, g * VR:(g + 1) * VR, :], pt[0:Lc])
             + _dot(vcat[g * VR:(g + 1) * VR, jb * W:(jb + 3) * W], pt[Lc:nk]))
        den = o[hd:hd + 1] + jnp.exp2(sk2[g] - m)
        return o[0:hd] / den

    mx = scores(0)
    outg = []
    for n, (jb, g) in enumerate(items):
        nxt = scores(n + 1) if n + 1 < len(items) else None
        outg.append(consume(n, mx))
        mx = nxt
        if g == G - 1:
            for r in range(R):
                yt = jnp.concatenate([og[:, r * W:(r + 1) * W] for og in outg], 0)
                gt = gate_ref[0, jb * W:(jb + 1) * W, r * LANES:(r + 1) * LANES]
                o_ref[0, jb * W:(jb + 1) * W, r * LANES:(r + 1) * LANES] = yt.T * _silu(gt)
            outg = []


def _swa_attn(qt, k, vt, kc, vct, sink, gate, bpt):
    B, NQ, S = qt.shape
    Lc = kc.shape[1]
    W = WINDOW
    G = SWA_KV_HEADS
    R = SWA_HEADS // G
    VR = G * SWA_VROWS
    nb = S // W
    T = bpt * W
    kern = functools.partial(_swa_attn_kernel, bpt=bpt, W=W, R=R, G=G, Lc=Lc)
    main = pl.BlockSpec((1, T, LANES), lambda b, i: (b, i, 0))
    prev = pl.BlockSpec((1, W, LANES), lambda b, i: (b, jnp.maximum(i * bpt - 1, 0), 0))
    nxt = pl.BlockSpec((1, W, LANES), lambda b, i: (b, jnp.minimum((i + 1) * bpt, nb - 1), 0))
    main_t = pl.BlockSpec((1, VR, T), lambda b, i: (b, 0, i))
    prev_t = pl.BlockSpec((1, VR, W), lambda b, i: (b, 0, jnp.maximum(i * bpt - 1, 0)))
    nxt_t = pl.BlockSpec((1, VR, W), lambda b, i: (b, 0, jnp.minimum((i + 1) * bpt, nb - 1)))
    tok = pl.BlockSpec((1, T, NQ), lambda b, i: (b, i, 0))
    return pl.pallas_call(
        kern, name="swa_attn", grid=(B, S // T),
        in_specs=[pl.BlockSpec(memory_space=pltpu.SMEM), pl.BlockSpec((1, NQ, T), lambda b, i: (b, 0, i)),
                  prev, main, nxt, prev_t, main_t, nxt_t,
                  pl.BlockSpec((1, Lc, LANES), lambda b, i: (b, 0, 0)),
                  pl.BlockSpec((1, VR, Lc), lambda b, i: (b, 0, 0)), tok],
        out_specs=tok,
        out_shape=jax.ShapeDtypeStruct((B, S, NQ), F32),
        scratch_shapes=[pltpu.VMEM((Lc + 3 * W, R * W), F32)] * 2,
        compiler_params=_params(("parallel", "arbitrary")),
    )(sink, qt, k, k, k, vt, vt, vt, kc, vct, gate)


def _swa_ctx_kernel(sink_ref, q_ref, kc_ref, vc_ref, gate_ref, o_ref, *, Lc, R, G, scale):
    kc = kc_ref[0]
    lane = lax.broadcasted_iota(jnp.int32, (Lc, LANES), 1)
    tiles = [(q_ref[0, :, r * LANES:(r + 1) * LANES] * scale).astype(BF16) for r in range(R)]
    outg = []
    for g in range(G):
        qg = jnp.concatenate([_group_select(t, g, 0.0) for t in tiles], 0)
        outg.append(_swa_group(qg, _sink_column(sink_ref, g, R, Lc) * LOG2E,
                               [(kc, _group_select(vc_ref[0], g, 1.0), None)]))
    for r in range(R):
        y = jnp.where(lane < LANES // 2, outg[0][r * Lc:(r + 1) * Lc], outg[1][r * Lc:(r + 1) * Lc])
        o_ref[0, :, r * LANES:(r + 1) * LANES] = y * _silu(gate_ref[0, :, r * LANES:(r + 1) * LANES])


def _swa_ctx(q, kc, vc, sink, gate):
    B, Lc, NQ = q.shape
    G = SWA_KV_HEADS
    R = SWA_HEADS // G
    kern = functools.partial(_swa_ctx_kernel, Lc=Lc, R=R, G=G, scale=SWA_HD ** -0.5 * LOG2E)
    qs = pl.BlockSpec((1, Lc, NQ), lambda b: (b, 0, 0))
    ctx = pl.BlockSpec((1, Lc, LANES), lambda b: (b, 0, 0))
    return pl.pallas_call(
        kern, name="swa_ctx", grid=(B,),
        in_specs=[pl.BlockSpec(memory_space=pltpu.SMEM), qs, ctx, ctx, qs],
        out_specs=qs,
        out_shape=jax.ShapeDtypeStruct((B, Lc, NQ), F32),
        compiler_params=_params(("arbitrary",)),
    )(sink, q, kc, vc, gate)


def _merge_kernel(yh_ref, ym_ref, ys_ref, mg_ref, x_ref, mod_ref, wh, wm, ws, wo, lg, lb, o_ref, *, D, alpha):
    m = (jax.nn.sigmoid(mg_ref[0, :, 0:D]) * _dot(yh_ref[0].astype(BF16), wh[...])
         + jax.nn.sigmoid(mg_ref[0, :, D:2 * D]) * _dot(ym_ref[0].astype(BF16), wm[...])
         + jax.nn.sigmoid(mg_ref[0, :, 2 * D:3 * D]) * _dot(ys_ref[0].astype(BF16), ws[...]))
    out = _dot(m.astype(BF16), wo[...])
    r = alpha * x_ref[0] + mod_ref[0, :, 2 * D:3 * D] * out
    mu = jnp.mean(r, -1, keepdims=True)
    rc = r - mu
    var = jnp.mean(rc * rc, -1, keepdims=True)
    o_ref[0] = rc * lax.rsqrt(var + EPS) * lg[...] + lb[...]


def _merge(yh, ym, ys, mg, x, mod, wh, wm, ws, wo, lg, lb, alpha, tm):
    Bx, S, D = x.shape
    kern = functools.partial(_merge_kernel, D=D, alpha=alpha)
    tok = lambda w: pl.BlockSpec((1, tm, w), lambda b, i: (b, i, 0))
    cs = _const_spec
    return pl.pallas_call(
        kern, name="merge", grid=(Bx, S // tm),
        in_specs=[tok(yh.shape[2]), tok(ym.shape[2]), tok(ys.shape[2]), tok(3 * D), tok(D),
                  pl.BlockSpec((1, 1, 3 * D), lambda b, i: (b, 0, 0)),
                  cs(wh.shape), cs(wm.shape), cs(ws.shape), cs(wo.shape), cs((1, D)), cs((1, D))],
        out_specs=tok(D),
        out_shape=jax.ShapeDtypeStruct((Bx, S, D), F32),
        compiler_params=_params(("parallel", "arbitrary")),
    )(yh, ym, ys, mg, x, mod, wh, wm, ws, wo, lg.reshape(1, D), lb.reshape(1, D))


_PAIR_ORDER = (0, 4, 1, 5, 2, 6, 3, 7)

_IN_COLS = (3 * HY_W, HY_W, MLA_Q_RANK, MLA_KV_RANK, MLA_ROPE, MLA_HEADS * MLA_V,
            SWA_HEADS * SWA_HD, SWA_KV_HEADS * SWA_HD, SWA_KV_HEADS * SWA_HD, SWA_HEADS * SWA_HD)
_SLAB_WIDTHS = (None, 3 * HY_W, HY_W, MLA_Q_RANK, MLA_KV_RANK, LANES, MLA_HEADS * MLA_V,
                SWA_HEADS * SWA_HD, LANES, LANES, SWA_HEADS * SWA_HD)


def _pair_cols(w):
    D = w.shape[0]
    return w.reshape(D, SWA_HEADS, SWA_HD)[:, _PAIR_ORDER, :].reshape(D, SWA_HEADS * SWA_HD)


def _layout_w_in(w_in, D):
    offs = [0]
    for cw in _IN_COLS:
        offs.append(offs[-1] + cw)
    sl = [w_in[:, offs[i]:offs[i + 1]] for i in range(len(_IN_COLS))]
    hy, hyg, cq, ckv, kr, mlag, sq, sk, sv, swag = sl
    mg = w_in[:, offs[-1]:]
    kr_p = jnp.pad(kr, ((0, 0), (MLA_NOPE, LANES - MLA_NOPE - MLA_ROPE)))
    return jnp.concatenate([mg, hy, hyg, cq, ckv, kr_p, mlag, _pair_cols(sq), sk, sv, _pair_cols(swag)],
                           1).astype(BF16)


def _layer_weights(p, D):
    w = {}
    w['w_in'] = _layout_w_in(p['w_in'], D)
    uq = p['mla_w_uq'].reshape(MLA_Q_RANK, MLA_HEADS, MLA_QK)
    w['w_uq'] = jnp.pad(uq, ((0, 0), (0, 0), (0, LANES - MLA_QK))).reshape(MLA_Q_RANK, MLA_HEADS * LANES).astype(BF16)
    ukv = p['mla_w_ukv'].reshape(MLA_KV_RANK, MLA_HEADS, MLA_NOPE + MLA_V)
    w['w_uk'] = jnp.pad(ukv[:, :, :MLA_NOPE], ((0, 0), (0, 0), (0, LANES - MLA_NOPE))).reshape(
        MLA_KV_RANK, MLA_HEADS * LANES).astype(BF16)
    w['w_uv'] = ukv[:, :, MLA_NOPE:].reshape(MLA_KV_RANK, MLA_HEADS * MLA_V).astype(BF16)
    w['w_proj_hy'] = p['w_proj_hy'].astype(BF16)
    w['w_proj_mla'] = p['w_proj_mla'].astype(BF16)
    w['w_proj_swa'] = p['w_proj_swa'].reshape(SWA_HEADS, SWA_HD, D)[_PAIR_ORDER, :, :].reshape(
        SWA_HEADS * SWA_HD, D).astype(BF16)
    w['w_out'] = p['w_out'].astype(BF16)
    return w


def _trunk_layer(xl, xc, p, tabs, alpha, ctx_out):
    B, S, D = xl.shape
    Lc = xc.shape[1]
    w = _layer_weights(p, D)
    widths = (N_BRANCH * D,) + _SLAB_WIDTHS[1:]

    mod = p['mod']
    mod_l = mod[:B].reshape(B, 1, 3 * D)
    mod_c = jnp.broadcast_to(mod[B].reshape(1, 1, 3 * D), (B, 1, 3 * D))

    tm_c = min(Lc, 256)
    (mg_l, hy_l, hyg_l, cq_l, ckv_l, kr_l, mlag_l, sq_l, sk_l, sv_l, swag_l) = _in_proj(xl, mod_l, w['w_in'], widths, 512)
    (mg_c, hy_c, hyg_c, cq_c, ckv_c, kr_c, mlag_c, sq_c, sk_c, sv_c, swag_c) = _in_proj(xc, mod_c, w['w_in'], widths, tm_c)

    k_mc, v_mc = _mla_kv(ckv_c, kr_c, p['mla_kv_norm'], w['w_uk'], w['w_uv'], tabs['id_c'], tm_c)
    k_sc = sk_c.astype(BF16)
    v_sc = sv_c.astype(BF16)

    u_l, e_l = _hy_pre(hy_l, hyg_l, p['hy_conv_w'], p['hy_conv_b'], 512)
    h2_l = _hy_filters(S, p)
    y_hy = _hy_long_conv(u_l, e_l, h2_l, p['hy_skip'], tabs['dft'])

    q_ml = _mla_q(cq_l, p['mla_q_norm'], w['w_uq'], tabs['mla'], 512)
    k_ml, v_ml = _mla_kv(ckv_l, kr_l, p['mla_kv_norm'], w['w_uk'], w['w_uv'], tabs['mla'], 512)
    y_mla = _mla_attn(q_ml, k_mc, v_mc, k_ml, v_ml, mlag_l, 512)

    q_sl, k_sl, v_sl = _swa_prep(sq_l, sk_l, sv_l, tabs['swa'], 512)
    hd = LANES // SWA_KV_HEADS
    ones = jnp.broadcast_to(_ones_rows(SWA_VROWS - hd, Lc, BF16)[None], (B, SWA_VROWS - hd, Lc))
    v_sct = jnp.swapaxes(v_sc, 1, 2)
    v_sct = jnp.concatenate([v_sct[:, :hd], ones, v_sct[:, hd:], ones], 1)
    y_swa = _swa_attn(q_sl, k_sl, v_sl, k_sc, v_sct, p['swa_sink'], swag_l, 4)

    xl_new = _merge(y_hy, y_mla, y_swa, mg_l, xl, mod_l, w['w_proj_hy'], w['w_proj_mla'], w['w_proj_swa'],
                    w['w_out'], p['ln_g'], p['ln_b'], alpha, 256)
    if not ctx_out:
        return xl_new, xc

    u_c, e_c = _hy_pre(hy_c, hyg_c, p['hy_conv_w'], p['hy_conv_b'], tm_c)
    h2_c = _hy_filters(Lc, p)
    yc_hy = _hy_ctx_conv(u_c, e_c, h2_c, p['hy_skip'], tabs['dft_c'])
    q_mc = _mla_q(cq_c, p['mla_q_norm'], w['w_uq'], tabs['id_c'], tm_c)
    yc_mla = _mla_attn(q_mc, k_mc, v_mc, None, None, mlag_c, tm_c)
    yc_swa = _swa_ctx(sq_c, k_sc, v_sc, p['swa_sink'], swag_c)
    xc_new = _merge(yc_hy, yc_mla, yc_swa, mg_c, xc, mod_c, w['w_proj_hy'], w['w_proj_mla'], w['w_proj_swa'],
                    w['w_out'], p['ln_g'], p['ln_b'], alpha, tm_c)
    return xl_new, xc_new


def kernel(x, c, ctx, c_ctx, w_ada, b_ada, w_in, hy_conv_w, hy_conv_b, filt_w1, filt_b1, filt_w2, filt_b2,
           filt_w3, filt_b3, filt_freq, filt_w_out, hy_skip, mla_q_norm, mla_w_uq, mla_kv_norm, mla_w_ukv,
           swa_sink, w_proj_hy, w_proj_mla, w_proj_swa, w_out, ln_g, ln_b):
    B, S, D = x.shape
    Lc = ctx.shape[1]
    depth = w_in.shape[0]
    alpha = (2 * depth) ** 0.25
    stacked = dict(w_ada=w_ada, b_ada=b_ada, w_in=w_in, hy_conv_w=hy_conv_w, hy_conv_b=hy_conv_b,
                   filt_w1=filt_w1, filt_b1=filt_b1, filt_w2=filt_w2, filt_b2=filt_b2, filt_w3=filt_w3,
                   filt_b3=filt_b3, filt_freq=filt_freq, filt_w_out=filt_w_out, hy_skip=hy_skip,
                   mla_q_norm=mla_q_norm, mla_w_uq=mla_w_uq, mla_kv_norm=mla_kv_norm, mla_w_ukv=mla_w_ukv,
                   swa_sink=swa_sink, w_proj_hy=w_proj_hy, w_proj_mla=w_proj_mla, w_proj_swa=w_proj_swa,
                   w_out=w_out, ln_g=ln_g, ln_b=ln_b)

    lane = jnp.arange(LANES, dtype=jnp.int32)
    mla_on = (lane >= MLA_NOPE) & (lane < MLA_QK)
    tabs = {
        'mla': _rope_tables(S, jnp.clip(lane - MLA_NOPE, 0, MLA_ROPE - 1), mla_on, MLA_ROPE),
        'swa': _rope_tables(S, lane % SWA_HD, jnp.ones((LANES,), bool), SWA_HD),
        'id_c': _rope_tables(Lc, None, None, None, identity=True),
        'dft': _dft_tables(S),
        'dft_c': _ctx_dft_tables(Lc),
    }
    cvec = jnp.concatenate([c, c_ctx[None, :], jnp.zeros((8 - B - 1, D), F32)], 0)

    xl, xc = x, ctx
    for l in range(depth):
        p = {k: v[l] for k, v in stacked.items() if k != 'w_ada'}
        p['mod'] = _ada_mod(cvec, w_ada, p['b_ada'], l)
        xl, xc = _trunk_layer(xl, xc, p, tabs, alpha, l < depth - 1)
    return xl
```

```python
import functools
import math

import jax
import jax.numpy as jnp
from jax import lax
from jax.experimental import pallas as pl
from jax.experimental.pallas import tpu as pltpu

F32 = jnp.float32
BF16 = jnp.bfloat16
HIGHEST = lax.Precision.HIGHEST

GRID_W = 64
HY_W = 512
SHORT_K = 3
FILT_BANDS = 16
FILT_W = 64
DECAY_TARGET = 1e-2
FAST_DECAY_PCT = 0.3
SLOW_DECAY_PCT = 1.5
MLA_HEADS = 8
MLA_NOPE = 64
MLA_ROPE = 32
MLA_V = 64
MLA_QK = MLA_NOPE + MLA_ROPE
MLA_Q_RANK = 256
MLA_KV_RANK = 128
SWA_HEADS = 8
SWA_KV_HEADS = 2
SWA_HD = 64
WINDOW = 128
N_BRANCH = 3
ROPE_BASE = 10000.0
EPS = 1e-6
NEG = -1e30
LOG2E = math.log2(math.e)

LANES = 128
DFT_N2 = 128
VMEM_LIMIT = 56 * 1024 * 1024


def _silu(x):
    return x * jax.nn.sigmoid(x)


def _dot(a, b):
    return jnp.dot(a, b, preferred_element_type=F32)


def _dot_nt(a, b):
    return lax.dot_general(a, b, (((1,), (1,)), ((), ())), preferred_element_type=F32)


def _params(sem):
    return pltpu.CompilerParams(dimension_semantics=sem, vmem_limit_bytes=VMEM_LIMIT)


def _const_spec(shape):
    nd = len(shape)
    return pl.BlockSpec(shape, lambda *_: (0,) * nd)


def _ada_kernel(c_ref, w_ref, b_ref, o_ref):
    a = _silu(c_ref[...])
    o_ref[...] = jnp.dot(a, w_ref[...], precision=HIGHEST, preferred_element_type=F32) + b_ref[...]


def _ada_mod(cvec, w_ada, b_ada, layer):
    R, D = cvec.shape
    N = w_ada.shape[2]
    tn = 768
    return pl.pallas_call(
        _ada_kernel, name="ada_mod", grid=(N // tn,),
        in_specs=[_const_spec((R, D)), pl.BlockSpec((None, D, tn), lambda j: (layer, 0, j)),
                  pl.BlockSpec((1, tn), lambda j: (0, j))],
        out_specs=pl.BlockSpec((R, tn), lambda j: (0, j)),
        out_shape=jax.ShapeDtypeStruct((R, N), F32),
        compiler_params=_params(("arbitrary",)),
    )(cvec, w_ada, b_ada.reshape(1, N))


def _in_proj_kernel(x_ref, mod_ref, w_ref, *o_refs, widths, D, tm, sub):
    shift = mod_ref[0, :, 0:D]
    scale1 = 1.0 + mod_ref[0, :, D:2 * D]
    us = []
    for r in range(0, tm, sub):
        x = x_ref[0, r:r + sub, :]
        mu = jnp.mean(x, -1, keepdims=True)
        xc = x - mu
        var = jnp.mean(xc * xc, -1, keepdims=True)
        us.append(((xc * lax.rsqrt(var + EPS)) * scale1 + shift).astype(BF16))
    off = 0
    for o_ref, wd in zip(o_refs, widths):
        for k, u in enumerate(us):
            o_ref[0, k * sub:(k + 1) * sub, :] = _dot(u, w_ref[:, off:off + wd]).astype(o_ref.dtype)
        off += wd


def _in_proj(x, mod, w, widths, tm):
    Bx, S, D = x.shape
    P = w.shape[1]
    kern = functools.partial(_in_proj_kernel, widths=tuple(widths), D=D, tm=tm, sub=min(tm, 256))
    return pl.pallas_call(
        kern, name="in_proj", grid=(Bx, S // tm),
        in_specs=[pl.BlockSpec((1, tm, D), lambda b, i: (b, i, 0)),
                  pl.BlockSpec((1, 1, 3 * D), lambda b, i: (b, 0, 0)),
                  pl.BlockSpec((D, P), lambda b, i: (0, 0), pipeline_mode=pl.Buffered(1))],
        out_specs=[pl.BlockSpec((1, tm, wd), lambda b, i: (b, i, 0)) for wd in widths],
        out_shape=[jax.ShapeDtypeStruct((Bx, S, wd), F32) for wd in widths],
        compiler_params=_params(("parallel", "arbitrary")),
    )(x, mod, w)


def _filter_kernel(z_ref, t_ref, w1, b1, w2, b2, w3, b3, fr, wo, dl_ref, o_ref, *, tl, C):
    hp = lambda a, b: jnp.dot(a, b, precision=HIGHEST, preferred_element_type=F32)
    f = fr[...]
    h = jnp.sin(f * (hp(z_ref[...], w1[...]) + b1[...]))
    h = jnp.sin(f * (hp(h, w2[...]) + b2[...]))
    h = jnp.sin(f * (hp(h, w3[...]) + b3[...]))
    dl = jnp.abs(dl_ref[...])
    half = tl // 2
    for s in range(2):
        rows = slice(s * half, (s + 1) * half)
        o = hp(h, wo[s])
        decay = jnp.exp(-t_ref[rows, :] * dl)
        o_ref[rows, 0:C] = o[:, 0:C] * decay
        bwd = o[:, C:2 * C] * decay
        if s == 0:
            row = pl.program_id(0) * tl + lax.broadcasted_iota(jnp.int32, (half, 1), 0)
            bwd = jnp.where(row == 0, 0.0, bwd)
        o_ref[rows, C:2 * C] = bwd


def _hy_filters(L, p):
    C = HY_W
    FP = LANES
    FH = FP // 2
    t = jnp.linspace(0.0, 1.0, L, dtype=F32)[:, None]
    w = 2.0 * math.pi * jnp.arange(L, dtype=F32) / L
    f = jnp.linspace(1e-4, FILT_BANDS - 1, FILT_BANDS, dtype=F32)
    ang = w[:, None] * f[None, :]
    z = jnp.concatenate([t, jnp.cos(ang), -jnp.sin(ang)], -1)
    z = jnp.pad(z, ((0, 0), (0, FH - z.shape[1])))
    tl = min(L, 512)
    z = z.reshape(L // tl, 2, tl // 2, FH).transpose(0, 2, 1, 3).reshape(L // 2, FP)
    pad_h = lambda a: jnp.pad(a, ((0, FH - a.shape[0]), (0, FH - a.shape[1])))
    bdiag = lambda a: jnp.kron(jnp.eye(2, dtype=F32), pad_h(a))
    pad_v = lambda a: jnp.tile(jnp.pad(a, (0, FH - a.shape[0])), 2).reshape(1, FP)
    w1, w2, w3 = bdiag(p['filt_w1']), bdiag(p['filt_w2']), bdiag(p['filt_w3'])
    wo_h = jnp.pad(p['filt_w_out'], ((0, FH - FILT_W), (0, 0)))
    zeros = jnp.zeros_like(wo_h)
    wo = jnp.stack([jnp.concatenate([wo_h, zeros], 0), jnp.concatenate([zeros, wo_h], 0)], 0)
    min_decay = math.log(DECAY_TARGET) / SLOW_DECAY_PCT
    max_decay = math.log(DECAY_TARGET) / FAST_DECAY_PCT
    deltas = jnp.linspace(min_decay, max_decay, C, dtype=F32).reshape(1, C)
    kern = functools.partial(_filter_kernel, tl=tl, C=C)
    cs = _const_spec
    return pl.pallas_call(
        kern, name="hy_filter", grid=(L // tl,),
        in_specs=[pl.BlockSpec((tl // 2, FP), lambda i: (i, 0)), pl.BlockSpec((tl, 1), lambda i: (i, 0)),
                  cs((FP, FP)), cs((1, FP)), cs((FP, FP)), cs((1, FP)), cs((FP, FP)), cs((1, FP)),
                  cs((1, FP)), cs((2, FP, 2 * C)), cs((1, C))],
        out_specs=pl.BlockSpec((tl, 2 * C), lambda i: (i, 0)),
        out_shape=jax.ShapeDtypeStruct((L, 2 * C), F32),
        compiler_params=_params(("arbitrary",)),
    )(z, t, w1, pad_v(p['filt_b1']), w2, pad_v(p['filt_b2']), w3, pad_v(p['filt_b3']),
      pad_v(p['filt_freq']), wo, deltas)


def _hy_pre_kernel(x_ref, xp_ref, xn_ref, g_ref, w_ref, b_ref, u_ref, e_ref, *, ts, C):
    i = pl.program_id(1)
    nt = pl.num_programs(1)
    x = x_ref[0]
    prev_row = xp_ref[0, 7:8, :] * jnp.where(i > 0, 1.0, 0.0)
    next_row = xn_ref[0, 0:1, :] * jnp.where(i < nt - 1, 1.0, 0.0)
    rid = lax.broadcasted_iota(jnp.int32, (ts, 1), 0)
    xm = jnp.where(rid == 0, prev_row, pltpu.roll(x, 1, 0))
    xq = jnp.where(rid == ts - 1, next_row, pltpu.roll(x, ts - 1, 0))
    z = b_ref[...] + xm * w_ref[0:1, :] + x * w_ref[1:2, :] + xq * w_ref[2:3, :]
    u_ref[0] = z[:, 2 * C:3 * C] * z[:, C:2 * C]
    e_ref[0] = z[:, 0:C] * _silu(g_ref[0])


def _hy_pre(hy, hyg, conv_w, conv_b, ts):
    Bx, S, C3 = hy.shape
    C = C3 // 3
    nb8 = S // 8
    r = ts // 8
    w8 = jnp.pad(conv_w, ((0, 8 - SHORT_K), (0, 0)))
    kern = functools.partial(_hy_pre_kernel, ts=ts, C=C)
    return pl.pallas_call(
        kern, name="hy_pre", grid=(Bx, S // ts),
        in_specs=[pl.BlockSpec((1, ts, C3), lambda b, i: (b, i, 0)),
                  pl.BlockSpec((1, 8, C3), lambda b, i: (b, jnp.maximum(i * r - 1, 0), 0)),
                  pl.BlockSpec((1, 8, C3), lambda b, i: (b, jnp.minimum((i + 1) * r, nb8 - 1), 0)),
                  pl.BlockSpec((1, ts, C), lambda b, i: (b, i, 0)),
                  _const_spec((8, C3)), _const_spec((1, C3))],
        out_specs=[pl.BlockSpec((1, ts, C), lambda b, i: (b, i, 0))] * 2,
        out_shape=[jax.ShapeDtypeStruct((Bx, S, C), F32)] * 2,
        compiler_params=_params(("parallel", "arbitrary")),
    )(hy, hy, hy, hyg, w8, conv_b.reshape(1, C3))


def _dft_tables(L):
    n = 2 * L
    N2 = DFT_N2
    N1 = n // N2
    H1 = N1 // 2
    k1 = jnp.arange(N1, dtype=jnp.int32)
    t1 = jnp.arange(H1, dtype=jnp.int32)
    ang = (2.0 * math.pi / N1) * ((k1[:, None] * t1[None, :]) % N1).astype(F32)
    fa = jnp.stack([jnp.cos(ang), -jnp.sin(ang)], 1).reshape(2 * N1, H1)
    fai = jnp.concatenate([jnp.cos(ang), -jnp.sin(ang)], 0).T * (1.0 / n)
    k2 = jnp.arange(N2, dtype=jnp.int32)
    t2 = jnp.arange(N2, dtype=jnp.int32)
    a2 = (2.0 * math.pi / N2) * ((k2[:, None] * t2[None, :]) % N2).astype(F32)
    aw = (2.0 * math.pi / n) * ((k1[:, None] * t2[None, :]) % n).astype(F32)
    fr, fi = jnp.cos(a2)[None], -jnp.sin(a2)[None]
    wr, wi = jnp.cos(aw)[:, None, :], -jnp.sin(aw)[:, None, :]
    cr, ci = fr * wr - fi * wi, fr * wi + fi * wr
    mf = jnp.concatenate([jnp.concatenate([cr, -ci], 2), jnp.concatenate([ci, cr], 2)], 1)
    return fa.astype(BF16), fai.astype(BF16), mf.astype(BF16)


DFT_TT = 16
DFT_KB = 8
SUBLANES = 8
DFT_PITCH_B = 24
DFT_PITCH_A = 40


def _halves(shape_fn, idx_fn):
    return [pl.BlockSpec(shape_fn(SUBLANES), functools.partial(idx_fn, half=h)) for h in range(2)]


def _dft_a_fwd_kernel(xlo_ref, xhi_ref, f_ref, o_ref):
    f = f_ref[...]
    for j in range(DFT_TT):
        src = xlo_ref if j < SUBLANES else xhi_ref
        o_ref[0, j] = _dot(f, src[0, :, j % SUBLANES, :].astype(BF16)).astype(o_ref.dtype)


def _dft_a_fwd(x4, fa):
    Bx, H1, N2, C = x4.shape
    R = fa.shape[0]
    tok = _halves(lambda r: (1, H1, r, C), lambda b, j, half: (b, 0, 2 * j + half, 0))
    return pl.pallas_call(
        _dft_a_fwd_kernel, name="dft_a_fwd", grid=(Bx, N2 // DFT_TT),
        in_specs=tok + [_const_spec((R, H1))],
        out_specs=pl.BlockSpec((1, DFT_TT, R, C), lambda b, j: (b, j, 0, 0)),
        out_shape=jax.ShapeDtypeStruct((Bx, N2, R, C), BF16),
        compiler_params=_params(("parallel", "arbitrary")),
    )(x4, x4, fa)


def _stage_b_operand(scr, j):
    return jnp.concatenate([scr[:, 2 * j, :], scr[:, 2 * j + 1, :]], 0).astype(BF16)


def _dft_b_filter_kernel(a_ref, m_ref, k_ref, scr, *, kb, C, N2):
    scr[:, 0:2 * kb, :] = a_ref[0].astype(F32)
    for j in range(kb):
        h = _dot(m_ref[j], _stage_b_operand(scr, j))
        k_ref[j, 0:N2, :] = h[0:N2, 0:C] + h[0:N2, C:2 * C]
        k_ref[j, N2:2 * N2, :] = h[N2:2 * N2, 0:C] - h[N2:2 * N2, C:2 * C]


def _dft_b_filter(ah, mf):
    _, N2, R, C2 = ah.shape
    N1, C, kb = R // 2, C2 // 2, DFT_KB
    kern = functools.partial(_dft_b_filter_kernel, kb=kb, C=C, N2=N2)
    return pl.pallas_call(
        kern, name="dft_b_filter", grid=(N1 // kb,),
        in_specs=[pl.BlockSpec((1, N2, 2 * kb, C2), lambda i: (0, 0, i, 0)),
                  pl.BlockSpec((kb, 2 * N2, 2 * N2), lambda i: (i, 0, 0))],
        out_specs=pl.BlockSpec((kb, 2 * N2, C), lambda i: (i, 0, 0)),
        out_shape=jax.ShapeDtypeStruct((N1, 2 * N2, C), F32),
        scratch_shapes=[pltpu.VMEM((N2, DFT_PITCH_B, C2), F32)],
        compiler_params=_params(("arbitrary",)),
    )(ah, mf)


def _dft_mid_kernel(a_ref, mf_ref, k_ref, g_ref, scr, *, kb, nb, C, N2):
    for b in range(nb):
        scr[:, 0:2 * kb, :] = a_ref[b].astype(F32)
        for j in range(kb):
            kre = k_ref[j, 0:N2, :]
            kim = k_ref[j, N2:2 * N2, :]
            y = _dot(mf_ref[j], _stage_b_operand(scr, j))
            yre, yim = y[0:N2], y[N2:2 * N2]
            z = jnp.concatenate([yre * kre - yim * kim, yre * kim + yim * kre], 0).astype(BF16)
            g = lax.dot_general(mf_ref[j], z, (((0,), (0,)), ((), ())), preferred_element_type=F32)
            g_ref[b, j] = g.astype(g_ref.dtype)


def _dft_mid(a, mf, kf):
    Bx, N2, R, C = a.shape
    N1, kb = R // 2, DFT_KB
    kern = functools.partial(_dft_mid_kernel, kb=kb, nb=Bx, C=C, N2=N2)
    return pl.pallas_call(
        kern, name="dft_mid", grid=(N1 // kb,),
        in_specs=[pl.BlockSpec((Bx, N2, 2 * kb, C), lambda i: (0, 0, i, 0)),
                  pl.BlockSpec((kb, 2 * N2, 2 * N2), lambda i: (i, 0, 0)),
                  pl.BlockSpec((kb, 2 * N2, C), lambda i: (i, 0, 0))],
        out_specs=pl.BlockSpec((Bx, kb, 2 * N2, C), lambda i: (0, i, 0, 0)),
        out_shape=jax.ShapeDtypeStruct((Bx, N1, 2 * N2, C), BF16),
        scratch_shapes=[pltpu.VMEM((N2, DFT_PITCH_B, C), F32)],
        compiler_params=_params(("arbitrary",)),
    )(a, mf, kf)


def _dft_a_inv_kernel(g_ref, f_ref, ulo_ref, uhi_ref, elo_ref, ehi_ref, s_ref, o_ref, scr):
    f = f_ref[...]
    for ri in range(2):
        scr[:, ri * DFT_TT:(ri + 1) * DFT_TT, :] = g_ref[0, :, ri].astype(F32)
    for j in range(DFT_TT):
        g = jnp.concatenate([scr[:, j, :], scr[:, DFT_TT + j, :]], 0).astype(BF16)
        y = _dot(f, g)
        u_ref, e_ref = (ulo_ref, elo_ref) if j < SUBLANES else (uhi_ref, ehi_ref)
        jj = j % SUBLANES
        o_ref[0, :, j, :] = (y + u_ref[0, :, jj, :] * s_ref[...]) * e_ref[0, :, jj, :]


def _dft_a_inv(g5, fai, u4, e4, skip):
    Bx, N1, _, N2, C = g5.shape
    H1 = fai.shape[0]
    tok = _halves(lambda r: (1, H1, r, C), lambda b, j, half: (b, 0, 2 * j + half, 0))
    return pl.pallas_call(
        _dft_a_inv_kernel, name="dft_a_inv", grid=(Bx, N2 // DFT_TT),
        in_specs=[pl.BlockSpec((1, N1, 2, DFT_TT, C), lambda b, j: (b, 0, 0, j, 0)), _const_spec((H1, 2 * N1))]
        + tok + tok + [_const_spec((1, C))],
        out_specs=pl.BlockSpec((1, H1, DFT_TT, C), lambda b, j: (b, 0, j, 0)),
        out_shape=jax.ShapeDtypeStruct((Bx, H1, N2, C), F32),
        scratch_shapes=[pltpu.VMEM((N1, DFT_PITCH_A, C), F32)],
        compiler_params=_params(("parallel", "arbitrary")),
    )(g5, fai, u4, u4, e4, e4, skip)


def _hy_long_conv(u, e, h2, skip, tabs):
    B, L, C = u.shape
    fa, fai, mf = tabs
    N2 = DFT_N2
    N1 = 2 * L // N2
    H1 = N1 // 2
    kf = _dft_b_filter(_dft_a_fwd(h2.reshape(1, H1, N2, 2 * C), fa), mf)
    u4 = u.reshape(B, H1, N2, C)
    g = _dft_mid(_dft_a_fwd(u4, fa), mf, kf)
    y = _dft_a_inv(g.reshape(B, N1, 2, N2, C), fai, u4, e.reshape(B, H1, N2, C), skip.reshape(1, C))
    return y.reshape(B, L, C)


def _ctx_dft_tables(Lc):
    n = 2 * Lc
    k = jnp.arange(n, dtype=jnp.int32)
    t = jnp.arange(Lc, dtype=jnp.int32)
    ang = (2.0 * math.pi / n) * ((k[:, None] * t[None, :]) % n).astype(F32)
    fc = jnp.concatenate([jnp.cos(ang), -jnp.sin(ang)], 0)
    fi = fc.T * (1.0 / n)
    return fc.astype(BF16), fi.astype(BF16)


def _hy_ctx_conv_kernel(u_ref, e_ref, h_ref, fc_ref, fi_ref, s_ref, o_ref, *, n, C):
    u = u_ref[0]
    fc = fc_ref[...]
    uf = _dot(fc, u.astype(BF16))
    hf = _dot(fc, h_ref[...].astype(BF16))
    kre = hf[0:n, 0:C] + hf[0:n, C:2 * C]
    kim = hf[n:2 * n, 0:C] - hf[n:2 * n, C:2 * C]
    ure, uim = uf[0:n], uf[n:2 * n]
    z = jnp.concatenate([ure * kre - uim * kim, ure * kim + uim * kre], 0).astype(BF16)
    y = _dot(fi_ref[...], z)
    o_ref[0] = (y + u * s_ref[...]) * e_ref[0]


def _hy_ctx_conv(u, e, h2, skip, tabs):
    B, Lc, C = u.shape
    fc, fi = tabs
    n = 2 * Lc
    kern = functools.partial(_hy_ctx_conv_kernel, n=n, C=C)
    blk = pl.BlockSpec((1, Lc, C), lambda b: (b, 0, 0))
    return pl.pallas_call(
        kern, name="hy_ctx_conv", grid=(B,),
        in_specs=[blk, blk, _const_spec((Lc, 2 * C)), _const_spec((2 * n, Lc)), _const_spec((Lc, 2 * n)),
                  _const_spec((1, C))],
        out_specs=blk,
        out_shape=jax.ShapeDtypeStruct((B, Lc, C), F32),
        compiler_params=_params(("arbitrary",)),
    )(u, e, h2, fc, fi, skip.reshape(1, C))


def _rope_tables(S, lane_dim, lane_on, head_rot, identity=False):
    if identity:
        return (jnp.ones((S, LANES), F32), jnp.zeros((S, LANES), F32), jnp.zeros((S, LANES), F32))
    seg_w = head_rot // 2
    half = seg_w // 2
    seg = lane_dim // seg_w
    w = lane_dim % seg_w
    first = w < half
    inv = ROPE_BASE ** (-(w % half).astype(F32) / half)
    nrow = S // GRID_W
    ang_r = jnp.arange(nrow, dtype=jnp.int32).astype(F32)[:, None] * inv[None, :]
    ang_c = jnp.arange(GRID_W, dtype=jnp.int32).astype(F32)[:, None] * inv[None, :]

    def expand(fn):
        tab = jnp.where((seg == 0)[None, None, :], fn(ang_r)[:, None, :], fn(ang_c)[None, :, :])
        return tab.reshape(S, LANES)

    cos, sin = expand(jnp.cos), expand(jnp.sin)
    on = lane_on[None, :]
    c = jnp.where(on, cos, 1.0)
    sm = jnp.where(on & first[None, :], -sin, 0.0)
    sp = jnp.where(on & (~first)[None, :], sin, 0.0)
    return c, sm, sp


def _rope(x, c, sm, sp, hh):
    return x * c + pltpu.roll(x, LANES - hh, 1) * sm + pltpu.roll(x, hh, 1) * sp


MLA_VROWS = 80

def _mla_q_kernel(cq_ref, g_ref, w_ref, wr_ref, c_ref, sm_ref, sp_ref, o_ref, *, nh, scale):
    x = cq_ref[0]
    xn = (x * lax.rsqrt(jnp.mean(x * x, -1, keepdims=True) + EPS) * g_ref[...]).astype(BF16)
    q = _dot(xn, w_ref[...])
    qr = _dot(xn, wr_ref[...])
    c = c_ref[...]
    s = sm_ref[...] + sp_ref[...]
    for h in range(nh):
        qh = (q[:, h * LANES:(h + 1) * LANES] * c + qr[:, h * LANES:(h + 1) * LANES] * s) * scale
        o_ref[0, h * LANES:(h + 1) * LANES, :] = qh.T.astype(o_ref.dtype)


def _mla_q(cq, qnorm, w_uq_p, w_uq_r, tabs, tm):
    Bx, S, R = cq.shape
    N = w_uq_p.shape[1]
    kern = functools.partial(_mla_q_kernel, nh=MLA_HEADS, scale=MLA_QK ** -0.5 * LOG2E)
    tab = pl.BlockSpec((tm, LANES), lambda b, i: (i, 0))
    return pl.pallas_call(
        kern, name="mla_q", grid=(Bx, S // tm),
        in_specs=[pl.BlockSpec((1, tm, R), lambda b, i: (b, i, 0)), _const_spec((1, R)), _const_spec((R, N)),
                  _const_spec((R, N)), tab, tab, tab],
        out_specs=pl.BlockSpec((1, N, tm), lambda b, i: (b, 0, i)),
        out_shape=jax.ShapeDtypeStruct((Bx, N, S), BF16),
        compiler_params=_params(("parallel", "arbitrary")),
    )(cq, qnorm.reshape(1, R), w_uq_p, w_uq_r, *tabs)


def _mla_kv_kernel(ckv_ref, kr_ref, g_ref, wk_ref, wv_ref, c_ref, sm_ref, sp_ref, k_ref, vt_ref, *, nh, hh, tm):
    x = ckv_ref[0]
    xn = (x * lax.rsqrt(jnp.mean(x * x, -1, keepdims=True) + EPS) * g_ref[...]).astype(BF16)
    kn = _dot(xn, wk_ref[...])
    krr = _rope(kr_ref[0], c_ref[...], sm_ref[...], sp_ref[...], hh)
    for h in range(nh):
        k_ref[0, :, h * LANES:(h + 1) * LANES] = (kn[:, h * LANES:(h + 1) * LANES] + krr).astype(k_ref.dtype)
    vt = _dot(xn, wv_ref[...]).T
    pad = MLA_VROWS - MLA_V
    ones_rows = jnp.where(lax.broadcasted_iota(jnp.int32, (pad, tm), 0) == 0, 1.0, 0.0).astype(vt_ref.dtype)
    for h in range(nh):
        vt_ref[0, 0, h * MLA_VROWS:h * MLA_VROWS + MLA_V, :] = vt[h * MLA_V:(h + 1) * MLA_V].astype(vt_ref.dtype)
        vt_ref[0, 0, h * MLA_VROWS + MLA_V:(h + 1) * MLA_VROWS, :] = ones_rows


def _mla_kv(ckv, kr, kvnorm, wk_p, wv, tabs, tm):
    Bx, S, R = ckv.shape
    NK, NV = wk_p.shape[1], wv.shape[1]
    kern = functools.partial(_mla_kv_kernel, nh=MLA_HEADS, hh=MLA_ROPE // 4, tm=tm)
    tab = pl.BlockSpec((tm, LANES), lambda b, i: (i, 0))
    VR = MLA_HEADS * MLA_VROWS
    return pl.pallas_call(
        kern, name="mla_kv", grid=(Bx, S // tm),
        in_specs=[pl.BlockSpec((1, tm, R), lambda b, i: (b, i, 0)),
                  pl.BlockSpec((1, tm, LANES), lambda b, i: (b, i, 0)),
                  _const_spec((1, R)), _const_spec((R, NK)), _const_spec((R, NV)), tab, tab, tab],
        out_specs=[pl.BlockSpec((1, tm, NK), lambda b, i: (b, i, 0)),
                   pl.BlockSpec((1, 1, VR, tm), lambda b, i: (b, i, 0, 0))],
        out_shape=[jax.ShapeDtypeStruct((Bx, S, NK), BF16), jax.ShapeDtypeStruct((Bx, S // tm, VR, tm), BF16)],
        compiler_params=_params(("parallel", "arbitrary")),
    )(ckv, kr, kvnorm.reshape(1, R), wk_p, wv, *tabs)


def _mla_attn_kernel(*refs, tq, tk, n_chunks):
    n_in = 7 if n_chunks else 5
    if n_chunks:
        qt_ref, kc_ref, vct_ref, k_ref, vt_ref, gate_ref, o_ref = refs[:n_in]
    else:
        qt_ref, kc_ref, vct_ref, gate_ref, o_ref = refs[:n_in]
    acc_scr = refs[n_in:n_in + 2]
    VR = MLA_VROWS
    qs = [qt_ref[0, j * LANES:(j + 1) * LANES, :] for j in range(2)]

    def accumulate(j, m, st, mx, vblk):
        m_new = mx if m is None else jnp.maximum(m, mx)
        pt = jnp.exp2(st - m_new).astype(BF16)
        pv = _dot(vblk, pt)
        acc_scr[j][...] = pv if m is None else jnp.exp2(m - m_new) * acc_scr[j][...] + pv
        return m_new

    Lc = kc_ref.shape[1]

    def ctx_scores(j):
        return _dot(kc_ref[0, :, j * LANES:(j + 1) * LANES], qs[j])

    if not n_chunks:
        for j in range(2):
            st = ctx_scores(j)
            accumulate(j, None, st, jnp.max(st, 0, keepdims=True), vct_ref[0, 0, j * VR:(j + 1) * VR, :])
    else:
        s_scr = refs[n_in + 2:]

        def scores(j, c, slot):
            st = pl.multiple_of(c * tk, tk)
            blk = _dot(k_ref[0, pl.ds(st, tk), j * LANES:(j + 1) * LANES], qs[j])
            s_scr[2 * j + slot][...] = blk
            return jnp.max(blk, 0, keepdims=True)

        def scores_ctx(j, slot):
            blk = ctx_scores(j)
            s_scr[2 * j + slot][0:Lc, :] = blk
            return jnp.max(blk, 0, keepdims=True)

        def consume(j, c, slot, mj, mxj):
            return accumulate(j, mj, s_scr[2 * j + slot][...], mxj, vt_ref[0, c, j * VR:(j + 1) * VR, :])

        m = [jnp.full((1, tq), NEG, F32) for _ in range(2)]
        for j in range(2):
            acc_scr[j][...] = jnp.zeros((VR, tq), F32)
        mx = [scores(j, 0, 0) for j in range(2)]

        unroll = 4 if n_chunks % 4 == 0 else 2

        def steps(c, m, mx, last):
            for s in range(unroll):
                slot = s % 2
                if last and s == unroll - 1:
                    nxt = [scores_ctx(j, 1 - slot) for j in range(2)]
                else:
                    nxt = [scores(j, c + s + 1, 1 - slot) for j in range(2)]
                for j in range(2):
                    m[j] = consume(j, c + s, slot, m[j], mx[j])
                mx = nxt
            return m, mx

        def body(ci, carry):
            m0, x0, m1, x1 = carry
            m, mx = steps(unroll * ci, [m0, m1], [x0, x1], False)
            return m[0], mx[0], m[1], mx[1]

        m0, x0, m1, x1 = lax.fori_loop(0, n_chunks // unroll - 1, body, (m[0], mx[0], m[1], mx[1]))
        m, mx = steps(n_chunks - unroll, [m0, m1], [x0, x1], True)
        cslot = unroll % 2
        for j in range(2):
            accumulate(j, m[j], s_scr[2 * j + cslot][0:Lc, :], mx[j], vct_ref[0, 0, j * VR:(j + 1) * VR, :])
    yt = jnp.concatenate([acc_scr[j][0:MLA_V, :] / acc_scr[j][MLA_V:MLA_V + 1, :] for j in range(2)], 0)
    o_ref[0] = yt.T * _silu(gate_ref[0])


def _mla_attn(qt, kc, vct, k, vt, gate, tq):
    B, _, S = qt.shape
    Lc = kc.shape[1]
    npair = MLA_HEADS // 2
    VR2 = 2 * MLA_VROWS
    n_chunks, tk = (0, 0) if k is None else (vt.shape[1], vt.shape[3])
    kern = functools.partial(_mla_attn_kernel, tq=tq, tk=tk, n_chunks=n_chunks)
    in_specs = [pl.BlockSpec((1, 2 * LANES, tq), lambda b, p, i: (b, p, i)),
                pl.BlockSpec((1, Lc, 2 * LANES), lambda b, p, i: (b, 0, p)),
                pl.BlockSpec((1, 1, VR2, Lc), lambda b, p, i: (b, 0, p, 0))]
    args = [qt, kc, vct]
    if n_chunks:
        Sk = k.shape[1]
        in_specs += [pl.BlockSpec((1, Sk, 2 * LANES), lambda b, p, i: (b, 0, p)),
                     pl.BlockSpec((1, n_chunks, VR2, tk), lambda b, p, i: (b, 0, p, 0))]
        args += [k, vt]
    in_specs.append(pl.BlockSpec((1, tq, LANES), lambda b, p, i: (b, i, p)))
    args.append(gate)
    return pl.pallas_call(
        kern, name="mla_attn", grid=(B, npair, S // tq),
        in_specs=in_specs,
        out_specs=pl.BlockSpec((1, tq, LANES), lambda b, p, i: (b, i, p)),
        out_shape=jax.ShapeDtypeStruct((B, S, npair * LANES), F32),
        scratch_shapes=[pltpu.VMEM((MLA_VROWS, tq), F32)] * 2 + [pltpu.VMEM((tk, tq), F32)] * (4 if n_chunks else 0),
        compiler_params=_params(("parallel", "arbitrary", "arbitrary")),
    )(*args)


SWA_VROWS = 80


def _ones_rows(n, width, dtype):
    return jnp.where(lax.broadcasted_iota(jnp.int32, (n, width), 0) == 0, 1.0, 0.0).astype(dtype)


def _swa_prep_kernel(q_ref, k_ref, v_ref, c_ref, sm_ref, sp_ref, qo_ref, ko_ref, vo_ref, *, scale, hh, nq, tm):
    c, sm, sp = c_ref[...], sm_ref[...], sp_ref[...]
    for r in range(nq):
        qr = _rope(q_ref[0, :, r * LANES:(r + 1) * LANES], c, sm, sp, hh) * scale
        qo_ref[0, r * LANES:(r + 1) * LANES, :] = qr.T.astype(qo_ref.dtype)
    ko_ref[0] = _rope(k_ref[0], c, sm, sp, hh).astype(ko_ref.dtype)
    vt = v_ref[0].T
    hd = LANES // SWA_KV_HEADS
    ones = _ones_rows(SWA_VROWS - hd, tm, vo_ref.dtype)
    for g in range(SWA_KV_HEADS):
        vo_ref[0, g * SWA_VROWS:g * SWA_VROWS + hd, :] = vt[g * hd:(g + 1) * hd].astype(vo_ref.dtype)
        vo_ref[0, g * SWA_VROWS + hd:(g + 1) * SWA_VROWS, :] = ones


def _swa_prep(sq, sk, sv, tabs, tm):
    Bx, S, NQ = sq.shape
    VR = SWA_KV_HEADS * SWA_VROWS
    kern = functools.partial(_swa_prep_kernel, scale=SWA_HD ** -0.5 * LOG2E, hh=SWA_HD // 4, nq=NQ // LANES, tm=tm)
    tab = pl.BlockSpec((tm, LANES), lambda b, i: (i, 0))
    qs = pl.BlockSpec((1, tm, NQ), lambda b, i: (b, i, 0))
    ks = pl.BlockSpec((1, tm, LANES), lambda b, i: (b, i, 0))
    return pl.pallas_call(
        kern, name="swa_prep", grid=(Bx, S // tm),
        in_specs=[qs, ks, ks, tab, tab, tab],
        out_specs=[pl.BlockSpec((1, NQ, tm), lambda b, i: (b, 0, i)), ks,
                   pl.BlockSpec((1, VR, tm), lambda b, i: (b, 0, i))],
        out_shape=[jax.ShapeDtypeStruct((Bx, NQ, S), BF16), jax.ShapeDtypeStruct((Bx, S, LANES), BF16),
                   jax.ShapeDtypeStruct((Bx, VR, S), BF16)],
        compiler_params=_params(("parallel", "arbitrary")),
    )(sq, sk, sv, *tabs)


def _sink_column(sink_ref, g, R, W):
    rid = lax.broadcasted_iota(jnp.int32, (R * W, 1), 0)
    col = jnp.full((R * W, 1), sink_ref[g * R + R - 1], F32)
    for r in range(R - 2, -1, -1):
        col = jnp.where(rid < (r + 1) * W, sink_ref[g * R + r], col)
    return col


def _swa_group(qg, sk2, blocks):
    ss = []
    for kk, _, bias in blocks:
        s = _dot_nt(qg, kk)
        ss.append(s if bias is None else s + bias)
    m = sk2
    for s in ss:
        m = jnp.maximum(m, jnp.max(s, -1, keepdims=True))
    o = None
    for s, (_, vv, _) in zip(ss, blocks):
        pv = _dot(jnp.exp2(s - m).astype(BF16), vv)
        o = pv if o is None else o + pv
    den = pltpu.roll(o, LANES // 2, 1) + jnp.exp2(sk2 - m)
    return o / den


def _group_select(x, g, fill):
    lane = lax.broadcasted_iota(jnp.int32, (1, LANES), 1)
    sel = (lane < LANES // 2) if g == 0 else (lane >= LANES // 2)
    return jnp.where(sel, x, jnp.full_like(x, fill))


def _swa_attn_kernel(sink_ref, qt_ref, kp_ref, km_ref, kn_ref, vp_ref, vm_ref, vn_ref, kc_ref, vct_ref, gate_ref,
                     o_ref, *s_scr, bpt, W, R, G, Lc):
    i = pl.program_id(1)
    nblk = pl.num_programs(1) * bpt
    hd = LANES // G
    VR = SWA_VROWS
    kcat = jnp.concatenate([kp_ref[0], km_ref[0], kn_ref[0]], 0)
    vcat = jnp.concatenate([vp_ref[0], vm_ref[0], vn_ref[0]], 1)
    kc = kc_ref[0]
    nk = Lc + 3 * W
    kj = lax.broadcasted_iota(jnp.int32, (nk, R * W), 0) - Lc
    qi = lax.broadcasted_iota(jnp.int32, (nk, R * W), 1) % W
    band_bias = jnp.where((kj < 0) | (jnp.abs(W + qi - kj) <= W), 0.0, NEG)
    head = lax.broadcasted_iota(jnp.int32, (1, R * W), 1) // W
    row = lax.broadcasted_iota(jnp.int32, (LANES, 1), 0)
    sk2 = []
    for g in range(G):
        sk = jnp.full((1, R * W), sink_ref[g * R + R - 1], F32)
        for r in range(R - 2, -1, -1):
            sk = jnp.where(head <= r, sink_ref[g * R + r], sk)
        sk2.append(sk * LOG2E)
    kj_col = lax.broadcasted_iota(jnp.int32, (nk, 1), 0) - Lc
    items = [(jb, g) for jb in range(bpt) for g in range(G)]

    def scores(n):
        jb, g = items[n]
        gblk = i * bpt + jb
        keys = jnp.concatenate([kc, kcat[jb * W:(jb + 3) * W]], 0)
        pen_prev = jnp.where(gblk > 0, 0.0, NEG)
        pen_next = jnp.where(gblk < nblk - 1, 0.0, NEG)
        pen = jnp.where((kj_col >= 0) & (kj_col < W), pen_prev, jnp.where(kj_col >= 2 * W, pen_next, 0.0))
        sel = (row >= g * hd) & (row < (g + 1) * hd)
        tiles = [qt_ref[0, r * LANES:(r + 1) * LANES, jb * W:(jb + 1) * W] for r in range(R)]
        qg = jnp.concatenate([jnp.where(sel, t, jnp.zeros_like(t)) for t in tiles], 1)
        st = (_dot(keys, qg) + band_bias) + pen
        s_scr[n % 2][...] = st
        return jnp.max(st, 0, keepdims=True)

    def consume(n, mx):
        jb, g = items[n]
        m = jnp.maximum(mx, sk2[g])
        pt = jnp.exp2(s_scr[n % 2][...] - m).astype(BF16)
        o = (_dot(vct_ref[0, g * VR:(g + 1) * VR, :], pt[0:Lc])
             + _dot(vcat[g * VR:(g + 1) * VR, jb * W:(jb + 3) * W], pt[Lc:nk]))
        den = o[hd:hd + 1] + jnp.exp2(sk2[g] - m)
        return o[0:hd] / den

    mx = scores(0)
    outg = []
    for n, (jb, g) in enumerate(items):
        nxt = scores(n + 1) if n + 1 < len(items) else None
        outg.append(consume(n, mx))
        mx = nxt
        if g == G - 1:
            for r in range(R):
                yt = jnp.concatenate([og[:, r * W:(r + 1) * W] for og in outg], 0)
                gt = gate_ref[0, jb * W:(jb + 1) * W, r * LANES:(r + 1) * LANES]
                o_ref[0, jb * W:(jb + 1) * W, r * LANES:(r + 1) * LANES] = yt.T * _silu(gt)
            outg = []


def _swa_attn(qt, k, vt, kc, vct, sink, gate, bpt):
    B, NQ, S = qt.shape
    Lc = kc.shape[1]
    W = WINDOW
    G = SWA_KV_HEADS
    R = SWA_HEADS // G
    VR = G * SWA_VROWS
    nb = S // W
    T = bpt * W
    kern = functools.partial(_swa_attn_kernel, bpt=bpt, W=W, R=R, G=G, Lc=Lc)
    main = pl.BlockSpec((1, T, LANES), lambda b, i: (b, i, 0))
    prev = pl.BlockSpec((1, W, LANES), lambda b, i: (b, jnp.maximum(i * bpt - 1, 0), 0))
    nxt = pl.BlockSpec((1, W, LANES), lambda b, i: (b, jnp.minimum((i + 1) * bpt, nb - 1), 0))
    main_t = pl.BlockSpec((1, VR, T), lambda b, i: (b, 0, i))
    prev_t = pl.BlockSpec((1, VR, W), lambda b, i: (b, 0, jnp.maximum(i * bpt - 1, 0)))
    nxt_t = pl.BlockSpec((1, VR, W), lambda b, i: (b, 0, jnp.minimum((i + 1) * bpt, nb - 1)))
    tok = pl.BlockSpec((1, T, NQ), lambda b, i: (b, i, 0))
    return pl.pallas_call(
        kern, name="swa_attn", grid=(B, S // T),
        in_specs=[pl.BlockSpec(memory_space=pltpu.SMEM), pl.BlockSpec((1, NQ, T), lambda b, i: (b, 0, i)),
                  prev, main, nxt, prev_t, main_t, nxt_t,
                  pl.BlockSpec((1, Lc, LANES), lambda b, i: (b, 0, 0)),
                  pl.BlockSpec((1, VR, Lc), lambda b, i: (b, 0, 0)), tok],
        out_specs=tok,
        out_shape=jax.ShapeDtypeStruct((B, S, NQ), F32),
        scratch_shapes=[pltpu.VMEM((Lc + 3 * W, R * W), F32)] * 2,
        compiler_params=_params(("parallel", "arbitrary")),
    )(sink, qt, k, k, k, vt, vt, vt, kc, vct, gate)


def _swa_ctx_kernel(sink_ref, q_ref, kc_ref, vc_ref, gate_ref, o_ref, *, Lc, R, G, scale):
    kc = kc_ref[0]
    lane = lax.broadcasted_iota(jnp.int32, (Lc, LANES), 1)
    tiles = [(q_ref[0, :, r * LANES:(r + 1) * LANES] * scale).astype(BF16) for r in range(R)]
    outg = []
    for g in range(G):
        qg = jnp.concatenate([_group_select(t, g, 0.0) for t in tiles], 0)
        outg.append(_swa_group(qg, _sink_column(sink_ref, g, R, Lc) * LOG2E,
                               [(kc, _group_select(vc_ref[0], g, 1.0), None)]))
    for r in range(R):
        y = jnp.where(lane < LANES // 2, outg[0][r * Lc:(r + 1) * Lc], outg[1][r * Lc:(r + 1) * Lc])
        o_ref[0, :, r * LANES:(r + 1) * LANES] = y * _silu(gate_ref[0, :, r * LANES:(r + 1) * LANES])


def _swa_ctx(q, kc, vc, sink, gate):
    B, Lc, NQ = q.shape
    G = SWA_KV_HEADS
    R = SWA_HEADS // G
    kern = functools.partial(_swa_ctx_kernel, Lc=Lc, R=R, G=G, scale=SWA_HD ** -0.5 * LOG2E)
    qs = pl.BlockSpec((1, Lc, NQ), lambda b: (b, 0, 0))
    ctx = pl.BlockSpec((1, Lc, LANES), lambda b: (b, 0, 0))
    return pl.pallas_call(
        kern, name="swa_ctx", grid=(B,),
        in_specs=[pl.BlockSpec(memory_space=pltpu.SMEM), qs, ctx, ctx, qs],
        out_specs=qs,
        out_shape=jax.ShapeDtypeStruct((B, Lc, NQ), F32),
        compiler_params=_params(("arbitrary",)),
    )(sink, q, kc, vc, gate)


def _merge_kernel(yh_ref, ym_ref, ys_ref, mg_ref, x_ref, mod_ref, wh, wm, ws, wo, lg, lb, o_ref, *, D, alpha):
    m = (jax.nn.sigmoid(mg_ref[0, :, 0:D]) * _dot(yh_ref[0].astype(BF16), wh[...])
         + jax.nn.sigmoid(mg_ref[0, :, D:2 * D]) * _dot(ym_ref[0].astype(BF16), wm[...])
         + jax.nn.sigmoid(mg_ref[0, :, 2 * D:3 * D]) * _dot(ys_ref[0].astype(BF16), ws[...]))
    out = _dot(m.astype(BF16), wo[...])
    r = alpha * x_ref[0] + mod_ref[0, :, 2 * D:3 * D] * out
    mu = jnp.mean(r, -1, keepdims=True)
    rc = r - mu
    var = jnp.mean(rc * rc, -1, keepdims=True)
    o_ref[0] = rc * lax.rsqrt(var + EPS) * lg[...] + lb[...]


def _merge(yh, ym, ys, mg, x, mod, wh, wm, ws, wo, lg, lb, alpha, tm):
    Bx, S, D = x.shape
    kern = functools.partial(_merge_kernel, D=D, alpha=alpha)
    tok = lambda w: pl.BlockSpec((1, tm, w), lambda b, i: (b, i, 0))
    cs = _const_spec
    return pl.pallas_call(
        kern, name="merge", grid=(Bx, S // tm),
        in_specs=[tok(yh.shape[2]), tok(ym.shape[2]), tok(ys.shape[2]), tok(3 * D), tok(D),
                  pl.BlockSpec((1, 1, 3 * D), lambda b, i: (b, 0, 0)),
                  cs(wh.shape), cs(wm.shape), cs(ws.shape), cs(wo.shape), cs((1, D)), cs((1, D))],
        out_specs=tok(D),
        out_shape=jax.ShapeDtypeStruct((Bx, S, D), F32),
        compiler_params=_params(("parallel", "arbitrary")),
    )(yh, ym, ys, mg, x, mod, wh, wm, ws, wo, lg.reshape(1, D), lb.reshape(1, D))


_PAIR_ORDER = (0, 4, 1, 5, 2, 6, 3, 7)

_IN_COLS = (3 * HY_W, HY_W, MLA_Q_RANK, MLA_KV_RANK, MLA_ROPE, MLA_HEADS * MLA_V,
            SWA_HEADS * SWA_HD, SWA_KV_HEADS * SWA_HD, SWA_KV_HEADS * SWA_HD, SWA_HEADS * SWA_HD)
_SLAB_WIDTHS = (None, 3 * HY_W, HY_W, MLA_Q_RANK, MLA_KV_RANK, LANES, MLA_HEADS * MLA_V,
                SWA_HEADS * SWA_HD, LANES, LANES, SWA_HEADS * SWA_HD)


def _pair_cols(w):
    D = w.shape[0]
    return w.reshape(D, SWA_HEADS, SWA_HD)[:, _PAIR_ORDER, :].reshape(D, SWA_HEADS * SWA_HD)


def _layout_w_in(w_in, D):
    offs = [0]
    for cw in _IN_COLS:
        offs.append(offs[-1] + cw)
    sl = [w_in[:, offs[i]:offs[i + 1]] for i in range(len(_IN_COLS))]
    hy, hyg, cq, ckv, kr, mlag, sq, sk, sv, swag = sl
    mg = w_in[:, offs[-1]:]
    kr_p = jnp.pad(kr, ((0, 0), (MLA_NOPE, LANES - MLA_NOPE - MLA_ROPE)))
    return jnp.concatenate([mg, hy, hyg, cq, ckv, kr_p, mlag, _pair_cols(sq), sk, sv, _pair_cols(swag)],
                           1).astype(BF16)


def _layer_weights(p, D):
    w = {}
    w['w_in'] = _layout_w_in(p['w_in'], D)
    uq = p['mla_w_uq'].reshape(MLA_Q_RANK, MLA_HEADS, MLA_QK)
    w['w_uq'] = jnp.pad(uq, ((0, 0), (0, 0), (0, LANES - MLA_QK))).reshape(MLA_Q_RANK, MLA_HEADS * LANES).astype(BF16)
    half = MLA_ROPE // 4
    rot = uq[:, :, MLA_NOPE:].reshape(MLA_Q_RANK, MLA_HEADS, 2, 2, half)[:, :, :, ::-1, :]
    rot = rot.reshape(MLA_Q_RANK, MLA_HEADS, MLA_ROPE)
    w['w_uq_r'] = jnp.pad(rot, ((0, 0), (0, 0), (MLA_NOPE, LANES - MLA_QK))).reshape(
        MLA_Q_RANK, MLA_HEADS * LANES).astype(BF16)
    ukv = p['mla_w_ukv'].reshape(MLA_KV_RANK, MLA_HEADS, MLA_NOPE + MLA_V)
    w['w_uk'] = jnp.pad(ukv[:, :, :MLA_NOPE], ((0, 0), (0, 0), (0, LANES - MLA_NOPE))).reshape(
        MLA_KV_RANK, MLA_HEADS * LANES).astype(BF16)
    w['w_uv'] = ukv[:, :, MLA_NOPE:].reshape(MLA_KV_RANK, MLA_HEADS * MLA_V).astype(BF16)
    w['w_proj_hy'] = p['w_proj_hy'].astype(BF16)
    w['w_proj_mla'] = p['w_proj_mla'].astype(BF16)
    w['w_proj_swa'] = p['w_proj_swa'].reshape(SWA_HEADS, SWA_HD, D)[_PAIR_ORDER, :, :].reshape(
        SWA_HEADS * SWA_HD, D).astype(BF16)
    w['w_out'] = p['w_out'].astype(BF16)
    return w


def _trunk_layer(xl, xc, p, tabs, alpha, ctx_out):
    B, S, D = xl.shape
    Lc = xc.shape[1]
    w = _layer_weights(p, D)
    widths = (N_BRANCH * D,) + _SLAB_WIDTHS[1:]

    mod = p['mod']
    mod_l = mod[:B].reshape(B, 1, 3 * D)
    mod_c = jnp.broadcast_to(mod[B].reshape(1, 1, 3 * D), (B, 1, 3 * D))

    tm_c = min(Lc, 256)
    (mg_l, hy_l, hyg_l, cq_l, ckv_l, kr_l, mlag_l, sq_l, sk_l, sv_l, swag_l) = _in_proj(xl, mod_l, w['w_in'], widths, 512)
    (mg_c, hy_c, hyg_c, cq_c, ckv_c, kr_c, mlag_c, sq_c, sk_c, sv_c, swag_c) = _in_proj(xc, mod_c, w['w_in'], widths, tm_c)

    k_mc, v_mc = _mla_kv(ckv_c, kr_c, p['mla_kv_norm'], w['w_uk'], w['w_uv'], tabs['id_c'], tm_c)
    k_sc = sk_c.astype(BF16)
    v_sc = sv_c.astype(BF16)

    u_l, e_l = _hy_pre(hy_l, hyg_l, p['hy_conv_w'], p['hy_conv_b'], 512)
    h2_l = _hy_filters(S, p)
    y_hy = _hy_long_conv(u_l, e_l, h2_l, p['hy_skip'], tabs['dft'])

    q_ml = _mla_q(cq_l, p['mla_q_norm'], w['w_uq'], w['w_uq_r'], tabs['mla'], 512)
    k_ml, v_ml = _mla_kv(ckv_l, kr_l, p['mla_kv_norm'], w['w_uk'], w['w_uv'], tabs['mla'], 512)
    y_mla = _mla_attn(q_ml, k_mc, v_mc, k_ml, v_ml, mlag_l, 512)

    q_sl, k_sl, v_sl = _swa_prep(sq_l, sk_l, sv_l, tabs['swa'], 512)
    hd = LANES // SWA_KV_HEADS
    ones = jnp.broadcast_to(_ones_rows(SWA_VROWS - hd, Lc, BF16)[None], (B, SWA_VROWS - hd, Lc))
    v_sct = jnp.swapaxes(v_sc, 1, 2)
    v_sct = jnp.concatenate([v_sct[:, :hd], ones, v_sct[:, hd:], ones], 1)
    y_swa = _swa_attn(q_sl, k_sl, v_sl, k_sc, v_sct, p['swa_sink'], swag_l, 4)

    xl_new = _merge(y_hy, y_mla, y_swa, mg_l, xl, mod_l, w['w_proj_hy'], w['w_proj_mla'], w['w_proj_swa'],
                    w['w_out'], p['ln_g'], p['ln_b'], alpha, 256)
    if not ctx_out:
        return xl_new, xc

    u_c, e_c = _hy_pre(hy_c, hyg_c, p['hy_conv_w'], p['hy_conv_b'], tm_c)
    h2_c = _hy_filters(Lc, p)
    yc_hy = _hy_ctx_conv(u_c, e_c, h2_c, p['hy_skip'], tabs['dft_c'])
    q_mc = _mla_q(cq_c, p['mla_q_norm'], w['w_uq'], w['w_uq_r'], tabs['id_c'], tm_c)
    yc_mla = _mla_attn(q_mc, k_mc, v_mc, None, None, mlag_c, tm_c)
    yc_swa = _swa_ctx(sq_c, k_sc, v_sc, p['swa_sink'], swag_c)
    xc_new = _merge(yc_hy, yc_mla, yc_swa, mg_c, xc, mod_c, w['w_proj_hy'], w['w_proj_mla'], w['w_proj_swa'],
                    w['w_out'], p['ln_g'], p['ln_b'], alpha, tm_c)
    return xl_new, xc_new


def kernel(x, c, ctx, c_ctx, w_ada, b_ada, w_in, hy_conv_w, hy_conv_b, filt_w1, filt_b1, filt_w2, filt_b2,
           filt_w3, filt_b3, filt_freq, filt_w_out, hy_skip, mla_q_norm, mla_w_uq, mla_kv_norm, mla_w_ukv,
           swa_sink, w_proj_hy, w_proj_mla, w_proj_swa, w_out, ln_g, ln_b):
    B, S, D = x.shape
    Lc = ctx.shape[1]
    depth = w_in.shape[0]
    alpha = (2 * depth) ** 0.25
    stacked = dict(w_ada=w_ada, b_ada=b_ada, w_in=w_in, hy_conv_w=hy_conv_w, hy_conv_b=hy_conv_b,
                   filt_w1=filt_w1, filt_b1=filt_b1, filt_w2=filt_w2, filt_b2=filt_b2, filt_w3=filt_w3,
                   filt_b3=filt_b3, filt_freq=filt_freq, filt_w_out=filt_w_out, hy_skip=hy_skip,
                   mla_q_norm=mla_q_norm, mla_w_uq=mla_w_uq, mla_kv_norm=mla_kv_norm, mla_w_ukv=mla_w_ukv,
                   swa_sink=swa_sink, w_proj_hy=w_proj_hy, w_proj_mla=w_proj_mla, w_proj_swa=w_proj_swa,
                   w_out=w_out, ln_g=ln_g, ln_b=ln_b)

    lane = jnp.arange(LANES, dtype=jnp.int32)
    mla_on = (lane >= MLA_NOPE) & (lane < MLA_QK)
    tabs = {
        'mla': _rope_tables(S, jnp.clip(lane - MLA_NOPE, 0, MLA_ROPE - 1), mla_on, MLA_ROPE),
        'swa': _rope_tables(S, lane % SWA_HD, jnp.ones((LANES,), bool), SWA_HD),
        'id_c': _rope_tables(Lc, None, None, None, identity=True),
        'dft': _dft_tables(S),
        'dft_c': _ctx_dft_tables(Lc),
    }
    cvec = jnp.concatenate([c, c_ctx[None, :], jnp.zeros((8 - B - 1, D), F32)], 0)

    xl, xc = x, ctx
    for l in range(depth):
        p = {k: v[l] for k, v in stacked.items() if k != 'w_ada'}
        p['mod'] = _ada_mod(cvec, w_ada, p['b_ada'], l)
        xl, xc = _trunk_layer(xl, xc, p, tabs, alpha, l < depth - 1)
    return xl
```

```python
import functools
import math

import jax
import jax.numpy as jnp
from jax import lax
from jax.experimental import pallas as pl
from jax.experimental.pallas import tpu as pltpu

F32 = jnp.float32
BF16 = jnp.bfloat16
HIGHEST = lax.Precision.HIGHEST

GRID_W = 64
HY_W = 512
SHORT_K = 3
FILT_BANDS = 16
FILT_W = 64
DECAY_TARGET = 1e-2
FAST_DECAY_PCT = 0.3
SLOW_DECAY_PCT = 1.5
MLA_HEADS = 8
MLA_NOPE = 64
MLA_ROPE = 32
MLA_V = 64
MLA_QK = MLA_NOPE + MLA_ROPE
MLA_Q_RANK = 256
MLA_KV_RANK = 128
SWA_HEADS = 8
SWA_KV_HEADS = 2
SWA_HD = 64
WINDOW = 128
N_BRANCH = 3
ROPE_BASE = 10000.0
EPS = 1e-6
NEG = -1e30
LOG2E = math.log2(math.e)

LANES = 128
DFT_N2 = 128
VMEM_LIMIT = 56 * 1024 * 1024


def _silu(x):
    return x * jax.nn.sigmoid(x)


def _dot(a, b):
    return jnp.dot(a, b, preferred_element_type=F32)


def _dot_nt(a, b):
    return lax.dot_general(a, b, (((1,), (1,)), ((), ())), preferred_element_type=F32)


def _params(sem):
    return pltpu.CompilerParams(dimension_semantics=sem, vmem_limit_bytes=VMEM_LIMIT)


def _const_spec(shape):
    nd = len(shape)
    return pl.BlockSpec(shape, lambda *_: (0,) * nd)


def _ada_kernel(c_ref, w_ref, b_ref, o_ref):
    a = _silu(c_ref[...])
    o_ref[...] = jnp.dot(a, w_ref[...], precision=HIGHEST, preferred_element_type=F32) + b_ref[...]


def _ada_mod(cvec, w_ada, b_ada, layer):
    R, D = cvec.shape
    N = w_ada.shape[2]
    tn = 768
    return pl.pallas_call(
        _ada_kernel, name="ada_mod", grid=(N // tn,),
        in_specs=[_const_spec((R, D)), pl.BlockSpec((None, D, tn), lambda j: (layer, 0, j)),
                  pl.BlockSpec((1, tn), lambda j: (0, j))],
        out_specs=pl.BlockSpec((R, tn), lambda j: (0, j)),
        out_shape=jax.ShapeDtypeStruct((R, N), F32),
        compiler_params=_params(("arbitrary",)),
    )(cvec, w_ada, b_ada.reshape(1, N))


def _in_proj_kernel(x_ref, mod_ref, w_ref, *o_refs, widths, D, tm, sub):
    shift = mod_ref[0, :, 0:D]
    scale1 = 1.0 + mod_ref[0, :, D:2 * D]
    us = []
    for r in range(0, tm, sub):
        x = x_ref[0, r:r + sub, :]
        mu = jnp.mean(x, -1, keepdims=True)
        xc = x - mu
        var = jnp.mean(xc * xc, -1, keepdims=True)
        us.append(((xc * lax.rsqrt(var + EPS)) * scale1 + shift).astype(BF16))
    off = 0
    for o_ref, wd in zip(o_refs, widths):
        for k, u in enumerate(us):
            o_ref[0, k * sub:(k + 1) * sub, :] = _dot(u, w_ref[:, off:off + wd]).astype(o_ref.dtype)
        off += wd


def _in_proj(x, mod, w, widths, tm):
    Bx, S, D = x.shape
    P = w.shape[1]
    kern = functools.partial(_in_proj_kernel, widths=tuple(widths), D=D, tm=tm, sub=min(tm, 256))
    return pl.pallas_call(
        kern, name="in_proj", grid=(Bx, S // tm),
        in_specs=[pl.BlockSpec((1, tm, D), lambda b, i: (b, i, 0)),
                  pl.BlockSpec((1, 1, 3 * D), lambda b, i: (b, 0, 0)),
                  pl.BlockSpec((D, P), lambda b, i: (0, 0), pipeline_mode=pl.Buffered(1))],
        out_specs=[pl.BlockSpec((1, tm, wd), lambda b, i: (b, i, 0)) for wd in widths],
        out_shape=[jax.ShapeDtypeStruct((Bx, S, wd), F32) for wd in widths],
        compiler_params=_params(("parallel", "arbitrary")),
    )(x, mod, w)


def _filter_kernel(z_ref, t_ref, w1, b1, w2, b2, w3, b3, fr, wo, dl_ref, o_ref, *, tl, C):
    hp = lambda a, b: jnp.dot(a, b, precision=HIGHEST, preferred_element_type=F32)
    f = fr[...]
    h = jnp.sin(f * (hp(z_ref[...], w1[...]) + b1[...]))
    h = jnp.sin(f * (hp(h, w2[...]) + b2[...]))
    h = jnp.sin(f * (hp(h, w3[...]) + b3[...]))
    dl = jnp.abs(dl_ref[...])
    half = tl // 2
    for s in range(2):
        rows = slice(s * half, (s + 1) * half)
        o = hp(h, wo[s])
        decay = jnp.exp(-t_ref[rows, :] * dl)
        o_ref[rows, 0:C] = o[:, 0:C] * decay
        bwd = o[:, C:2 * C] * decay
        if s == 0:
            row = pl.program_id(0) * tl + lax.broadcasted_iota(jnp.int32, (half, 1), 0)
            bwd = jnp.where(row == 0, 0.0, bwd)
        o_ref[rows, C:2 * C] = bwd


def _hy_filters(L, p):
    C = HY_W
    FP = LANES
    FH = FP // 2
    t = jnp.linspace(0.0, 1.0, L, dtype=F32)[:, None]
    w = 2.0 * math.pi * jnp.arange(L, dtype=F32) / L
    f = jnp.linspace(1e-4, FILT_BANDS - 1, FILT_BANDS, dtype=F32)
    ang = w[:, None] * f[None, :]
    z = jnp.concatenate([t, jnp.cos(ang), -jnp.sin(ang)], -1)
    z = jnp.pad(z, ((0, 0), (0, FH - z.shape[1])))
    tl = min(L, 512)
    z = z.reshape(L // tl, 2, tl // 2, FH).transpose(0, 2, 1, 3).reshape(L // 2, FP)
    pad_h = lambda a: jnp.pad(a, ((0, FH - a.shape[0]), (0, FH - a.shape[1])))
    bdiag = lambda a: jnp.kron(jnp.eye(2, dtype=F32), pad_h(a))
    pad_v = lambda a: jnp.tile(jnp.pad(a, (0, FH - a.shape[0])), 2).reshape(1, FP)
    w1, w2, w3 = bdiag(p['filt_w1']), bdiag(p['filt_w2']), bdiag(p['filt_w3'])
    wo_h = jnp.pad(p['filt_w_out'], ((0, FH - FILT_W), (0, 0)))
    zeros = jnp.zeros_like(wo_h)
    wo = jnp.stack([jnp.concatenate([wo_h, zeros], 0), jnp.concatenate([zeros, wo_h], 0)], 0)
    min_decay = math.log(DECAY_TARGET) / SLOW_DECAY_PCT
    max_decay = math.log(DECAY_TARGET) / FAST_DECAY_PCT
    deltas = jnp.linspace(min_decay, max_decay, C, dtype=F32).reshape(1, C)
    kern = functools.partial(_filter_kernel, tl=tl, C=C)
    cs = _const_spec
    return pl.pallas_call(
        kern, name="hy_filter", grid=(L // tl,),
        in_specs=[pl.BlockSpec((tl // 2, FP), lambda i: (i, 0)), pl.BlockSpec((tl, 1), lambda i: (i, 0)),
                  cs((FP, FP)), cs((1, FP)), cs((FP, FP)), cs((1, FP)), cs((FP, FP)), cs((1, FP)),
                  cs((1, FP)), cs((2, FP, 2 * C)), cs((1, C))],
        out_specs=pl.BlockSpec((tl, 2 * C), lambda i: (i, 0)),
        out_shape=jax.ShapeDtypeStruct((L, 2 * C), F32),
        compiler_params=_params(("arbitrary",)),
    )(z, t, w1, pad_v(p['filt_b1']), w2, pad_v(p['filt_b2']), w3, pad_v(p['filt_b3']),
      pad_v(p['filt_freq']), wo, deltas)


def _hy_pre_kernel(x_ref, xp_ref, xn_ref, g_ref, w_ref, b_ref, u_ref, e_ref, *, ts, C):
    i = pl.program_id(1)
    nt = pl.num_programs(1)
    x = x_ref[0]
    prev_row = xp_ref[0, 7:8, :] * jnp.where(i > 0, 1.0, 0.0)
    next_row = xn_ref[0, 0:1, :] * jnp.where(i < nt - 1, 1.0, 0.0)
    rid = lax.broadcasted_iota(jnp.int32, (ts, 1), 0)
    xm = jnp.where(rid == 0, prev_row, pltpu.roll(x, 1, 0))
    xq = jnp.where(rid == ts - 1, next_row, pltpu.roll(x, ts - 1, 0))
    z = b_ref[...] + xm * w_ref[0:1, :] + x * w_ref[1:2, :] + xq * w_ref[2:3, :]
    u_ref[0] = z[:, 2 * C:3 * C] * z[:, C:2 * C]
    e_ref[0] = z[:, 0:C] * _silu(g_ref[0])


def _hy_pre(hy, hyg, conv_w, conv_b, ts):
    Bx, S, C3 = hy.shape
    C = C3 // 3
    nb8 = S // 8
    r = ts // 8
    w8 = jnp.pad(conv_w, ((0, 8 - SHORT_K), (0, 0)))
    kern = functools.partial(_hy_pre_kernel, ts=ts, C=C)
    return pl.pallas_call(
        kern, name="hy_pre", grid=(Bx, S // ts),
        in_specs=[pl.BlockSpec((1, ts, C3), lambda b, i: (b, i, 0)),
                  pl.BlockSpec((1, 8, C3), lambda b, i: (b, jnp.maximum(i * r - 1, 0), 0)),
                  pl.BlockSpec((1, 8, C3), lambda b, i: (b, jnp.minimum((i + 1) * r, nb8 - 1), 0)),
                  pl.BlockSpec((1, ts, C), lambda b, i: (b, i, 0)),
                  _const_spec((8, C3)), _const_spec((1, C3))],
        out_specs=[pl.BlockSpec((1, ts, C), lambda b, i: (b, i, 0))] * 2,
        out_shape=[jax.ShapeDtypeStruct((Bx, S, C), F32)] * 2,
        compiler_params=_params(("parallel", "arbitrary")),
    )(hy, hy, hy, hyg, w8, conv_b.reshape(1, C3))


def _dft_tables(L):
    n = 2 * L
    N2 = DFT_N2
    N1 = n // N2
    H1 = N1 // 2
    k1 = jnp.arange(N1, dtype=jnp.int32)
    t1 = jnp.arange(H1, dtype=jnp.int32)
    ang = (2.0 * math.pi / N1) * ((k1[:, None] * t1[None, :]) % N1).astype(F32)
    fa = jnp.stack([jnp.cos(ang), -jnp.sin(ang)], 1).reshape(2 * N1, H1)
    fai = jnp.concatenate([jnp.cos(ang), -jnp.sin(ang)], 0).T * (1.0 / n)
    k2 = jnp.arange(N2, dtype=jnp.int32)
    t2 = jnp.arange(N2, dtype=jnp.int32)
    a2 = (2.0 * math.pi / N2) * ((k2[:, None] * t2[None, :]) % N2).astype(F32)
    aw = (2.0 * math.pi / n) * ((k1[:, None] * t2[None, :]) % n).astype(F32)
    fr, fi = jnp.cos(a2)[None], -jnp.sin(a2)[None]
    wr, wi = jnp.cos(aw)[:, None, :], -jnp.sin(aw)[:, None, :]
    cr, ci = fr * wr - fi * wi, fr * wi + fi * wr
    mf = jnp.concatenate([jnp.concatenate([cr, -ci], 2), jnp.concatenate([ci, cr], 2)], 1)
    return fa.astype(BF16), fai.astype(BF16), mf.astype(BF16)


DFT_TT = 16
DFT_KB = 8
SUBLANES = 8
DFT_PITCH_B = 24
DFT_PITCH_A = 40


def _halves(shape_fn, idx_fn):
    return [pl.BlockSpec(shape_fn(SUBLANES), functools.partial(idx_fn, half=h)) for h in range(2)]


def _dft_a_fwd_kernel(xlo_ref, xhi_ref, f_ref, o_ref):
    f = f_ref[...]
    for j in range(DFT_TT):
        src = xlo_ref if j < SUBLANES else xhi_ref
        o_ref[0, j] = _dot(f, src[0, :, j % SUBLANES, :].astype(BF16)).astype(o_ref.dtype)


def _dft_a_fwd(x4, fa):
    Bx, H1, N2, C = x4.shape
    R = fa.shape[0]
    tok = _halves(lambda r: (1, H1, r, C), lambda b, j, half: (b, 0, 2 * j + half, 0))
    return pl.pallas_call(
        _dft_a_fwd_kernel, name="dft_a_fwd", grid=(Bx, N2 // DFT_TT),
        in_specs=tok + [_const_spec((R, H1))],
        out_specs=pl.BlockSpec((1, DFT_TT, R, C), lambda b, j: (b, j, 0, 0)),
        out_shape=jax.ShapeDtypeStruct((Bx, N2, R, C), BF16),
        compiler_params=_params(("parallel", "arbitrary")),
    )(x4, x4, fa)


def _stage_b_operand(scr, j):
    return jnp.concatenate([scr[:, 2 * j, :], scr[:, 2 * j + 1, :]], 0).astype(BF16)


def _dft_b_filter_kernel(a_ref, m_ref, k_ref, scr, *, kb, C, N2):
    scr[:, 0:2 * kb, :] = a_ref[0].astype(F32)
    for j in range(kb):
        h = _dot(m_ref[j], _stage_b_operand(scr, j))
        k_ref[j, 0:N2, :] = h[0:N2, 0:C] + h[0:N2, C:2 * C]
        k_ref[j, N2:2 * N2, :] = h[N2:2 * N2, 0:C] - h[N2:2 * N2, C:2 * C]


def _dft_b_filter(ah, mf):
    _, N2, R, C2 = ah.shape
    N1, C, kb = R // 2, C2 // 2, DFT_KB
    kern = functools.partial(_dft_b_filter_kernel, kb=kb, C=C, N2=N2)
    return pl.pallas_call(
        kern, name="dft_b_filter", grid=(N1 // kb,),
        in_specs=[pl.BlockSpec((1, N2, 2 * kb, C2), lambda i: (0, 0, i, 0)),
                  pl.BlockSpec((kb, 2 * N2, 2 * N2), lambda i: (i, 0, 0))],
        out_specs=pl.BlockSpec((kb, 2 * N2, C), lambda i: (i, 0, 0)),
        out_shape=jax.ShapeDtypeStruct((N1, 2 * N2, C), F32),
        scratch_shapes=[pltpu.VMEM((N2, DFT_PITCH_B, C2), F32)],
        compiler_params=_params(("arbitrary",)),
    )(ah, mf)


def _dft_mid_kernel(a_ref, mf_ref, k_ref, g_ref, scr, *, kb, nb, C, N2):
    for b in range(nb):
        scr[:, 0:2 * kb, :] = a_ref[b].astype(F32)
        for j in range(kb):
            kre = k_ref[j, 0:N2, :]
            kim = k_ref[j, N2:2 * N2, :]
            y = _dot(mf_ref[j], _stage_b_operand(scr, j))
            yre, yim = y[0:N2], y[N2:2 * N2]
            z = jnp.concatenate([yre * kre - yim * kim, yre * kim + yim * kre], 0).astype(BF16)
            g = lax.dot_general(mf_ref[j], z, (((0,), (0,)), ((), ())), preferred_element_type=F32)
            g_ref[b, j] = g.astype(g_ref.dtype)


def _dft_mid(a, mf, kf):
    Bx, N2, R, C = a.shape
    N1, kb = R // 2, DFT_KB
    kern = functools.partial(_dft_mid_kernel, kb=kb, nb=Bx, C=C, N2=N2)
    return pl.pallas_call(
        kern, name="dft_mid", grid=(N1 // kb,),
        in_specs=[pl.BlockSpec((Bx, N2, 2 * kb, C), lambda i: (0, 0, i, 0)),
                  pl.BlockSpec((kb, 2 * N2, 2 * N2), lambda i: (i, 0, 0)),
                  pl.BlockSpec((kb, 2 * N2, C), lambda i: (i, 0, 0))],
        out_specs=pl.BlockSpec((Bx, kb, 2 * N2, C), lambda i: (0, i, 0, 0)),
        out_shape=jax.ShapeDtypeStruct((Bx, N1, 2 * N2, C), BF16),
        scratch_shapes=[pltpu.VMEM((N2, DFT_PITCH_B, C), F32)],
        compiler_params=_params(("arbitrary",)),
    )(a, mf, kf)


def _dft_a_inv_kernel(g_ref, f_ref, ulo_ref, uhi_ref, elo_ref, ehi_ref, s_ref, o_ref, scr):
    f = f_ref[...]
    for ri in range(2):
        scr[:, ri * DFT_TT:(ri + 1) * DFT_TT, :] = g_ref[0, :, ri].astype(F32)
    for j in range(DFT_TT):
        g = jnp.concatenate([scr[:, j, :], scr[:, DFT_TT + j, :]], 0).astype(BF16)
        y = _dot(f, g)
        u_ref, e_ref = (ulo_ref, elo_ref) if j < SUBLANES else (uhi_ref, ehi_ref)
        jj = j % SUBLANES
        o_ref[0, :, j, :] = (y + u_ref[0, :, jj, :] * s_ref[...]) * e_ref[0, :, jj, :]


def _dft_a_inv(g5, fai, u4, e4, skip):
    Bx, N1, _, N2, C = g5.shape
    H1 = fai.shape[0]
    tok = _halves(lambda r: (1, H1, r, C), lambda b, j, half: (b, 0, 2 * j + half, 0))
    return pl.pallas_call(
        _dft_a_inv_kernel, name="dft_a_inv", grid=(Bx, N2 // DFT_TT),
        in_specs=[pl.BlockSpec((1, N1, 2, DFT_TT, C), lambda b, j: (b, 0, 0, j, 0)), _const_spec((H1, 2 * N1))]
        + tok + tok + [_const_spec((1, C))],
        out_specs=pl.BlockSpec((1, H1, DFT_TT, C), lambda b, j: (b, 0, j, 0)),
        out_shape=jax.ShapeDtypeStruct((Bx, H1, N2, C), F32),
        scratch_shapes=[pltpu.VMEM((N1, DFT_PITCH_A, C), F32)],
        compiler_params=_params(("parallel", "arbitrary")),
    )(g5, fai, u4, u4, e4, e4, skip)


def _hy_long_conv(u, e, h2, skip, tabs):
    B, L, C = u.shape
    fa, fai, mf = tabs
    N2 = DFT_N2
    N1 = 2 * L // N2
    H1 = N1 // 2
    kf = _dft_b_filter(_dft_a_fwd(h2.reshape(1, H1, N2, 2 * C), fa), mf)
    u4 = u.reshape(B, H1, N2, C)
    g = _dft_mid(_dft_a_fwd(u4, fa), mf, kf)
    y = _dft_a_inv(g.reshape(B, N1, 2, N2, C), fai, u4, e.reshape(B, H1, N2, C), skip.reshape(1, C))
    return y.reshape(B, L, C)


def _ctx_dft_tables(Lc):
    n = 2 * Lc
    k = jnp.arange(n, dtype=jnp.int32)
    t = jnp.arange(Lc, dtype=jnp.int32)
    ang = (2.0 * math.pi / n) * ((k[:, None] * t[None, :]) % n).astype(F32)
    fc = jnp.concatenate([jnp.cos(ang), -jnp.sin(ang)], 0)
    fi = fc.T * (1.0 / n)
    return fc.astype(BF16), fi.astype(BF16)


def _hy_ctx_conv_kernel(u_ref, e_ref, h_ref, fc_ref, fi_ref, s_ref, o_ref, *, n, C):
    u = u_ref[0]
    fc = fc_ref[...]
    uf = _dot(fc, u.astype(BF16))
    hf = _dot(fc, h_ref[...].astype(BF16))
    kre = hf[0:n, 0:C] + hf[0:n, C:2 * C]
    kim = hf[n:2 * n, 0:C] - hf[n:2 * n, C:2 * C]
    ure, uim = uf[0:n], uf[n:2 * n]
    z = jnp.concatenate([ure * kre - uim * kim, ure * kim + uim * kre], 0).astype(BF16)
    y = _dot(fi_ref[...], z)
    o_ref[0] = (y + u * s_ref[...]) * e_ref[0]


def _hy_ctx_conv(u, e, h2, skip, tabs):
    B, Lc, C = u.shape
    fc, fi = tabs
    n = 2 * Lc
    kern = functools.partial(_hy_ctx_conv_kernel, n=n, C=C)
    blk = pl.BlockSpec((1, Lc, C), lambda b: (b, 0, 0))
    return pl.pallas_call(
        kern, name="hy_ctx_conv", grid=(B,),
        in_specs=[blk, blk, _const_spec((Lc, 2 * C)), _const_spec((2 * n, Lc)), _const_spec((Lc, 2 * n)),
                  _const_spec((1, C))],
        out_specs=blk,
        out_shape=jax.ShapeDtypeStruct((B, Lc, C), F32),
        compiler_params=_params(("arbitrary",)),
    )(u, e, h2, fc, fi, skip.reshape(1, C))


def _rope_tables(S, lane_dim, lane_on, head_rot, identity=False):
    if identity:
        return (jnp.ones((S, LANES), F32), jnp.zeros((S, LANES), F32), jnp.zeros((S, LANES), F32))
    seg_w = head_rot // 2
    half = seg_w // 2
    seg = lane_dim // seg_w
    w = lane_dim % seg_w
    first = w < half
    inv = ROPE_BASE ** (-(w % half).astype(F32) / half)
    nrow = S // GRID_W
    ang_r = jnp.arange(nrow, dtype=jnp.int32).astype(F32)[:, None] * inv[None, :]
    ang_c = jnp.arange(GRID_W, dtype=jnp.int32).astype(F32)[:, None] * inv[None, :]

    def expand(fn):
        tab = jnp.where((seg == 0)[None, None, :], fn(ang_r)[:, None, :], fn(ang_c)[None, :, :])
        return tab.reshape(S, LANES)

    cos, sin = expand(jnp.cos), expand(jnp.sin)
    on = lane_on[None, :]
    c = jnp.where(on, cos, 1.0)
    sm = jnp.where(on & first[None, :], -sin, 0.0)
    sp = jnp.where(on & (~first)[None, :], sin, 0.0)
    return c, sm, sp


def _rope(x, c, sm, sp, hh):
    return x * c + pltpu.roll(x, LANES - hh, 1) * sm + pltpu.roll(x, hh, 1) * sp


MLA_VROWS = 80

def _mla_q_kernel(cq_ref, g_ref, w_ref, wr_ref, c_ref, sm_ref, sp_ref, o_ref, *, nh, scale):
    x = cq_ref[0]
    xn = (x * lax.rsqrt(jnp.mean(x * x, -1, keepdims=True) + EPS) * g_ref[...]).astype(BF16)
    q = _dot(xn, w_ref[...])
    qr = _dot(xn, wr_ref[...])
    c = c_ref[...]
    s = sm_ref[...] + sp_ref[...]
    for h in range(nh):
        qh = (q[:, h * LANES:(h + 1) * LANES] * c + qr[:, h * LANES:(h + 1) * LANES] * s) * scale
        o_ref[0, h * LANES:(h + 1) * LANES, :] = qh.T.astype(o_ref.dtype)


def _mla_q(cq, qnorm, w_uq_p, w_uq_r, tabs, tm):
    Bx, S, R = cq.shape
    N = w_uq_p.shape[1]
    kern = functools.partial(_mla_q_kernel, nh=MLA_HEADS, scale=MLA_QK ** -0.5 * LOG2E)
    tab = pl.BlockSpec((tm, LANES), lambda b, i: (i, 0))
    return pl.pallas_call(
        kern, name="mla_q", grid=(Bx, S // tm),
        in_specs=[pl.BlockSpec((1, tm, R), lambda b, i: (b, i, 0)), _const_spec((1, R)), _const_spec((R, N)),
                  _const_spec((R, N)), tab, tab, tab],
        out_specs=pl.BlockSpec((1, N, tm), lambda b, i: (b, 0, i)),
        out_shape=jax.ShapeDtypeStruct((Bx, N, S), BF16),
        compiler_params=_params(("parallel", "arbitrary")),
    )(cq, qnorm.reshape(1, R), w_uq_p, w_uq_r, *tabs)


def _mla_kv_kernel(ckv_ref, kr_ref, g_ref, wk_ref, wv_ref, c_ref, sm_ref, sp_ref, k_ref, vt_ref, *, nh, hh, tm):
    x = ckv_ref[0]
    xn = (x * lax.rsqrt(jnp.mean(x * x, -1, keepdims=True) + EPS) * g_ref[...]).astype(BF16)
    kn = _dot(xn, wk_ref[...])
    krr = _rope(kr_ref[0], c_ref[...], sm_ref[...], sp_ref[...], hh)
    for h in range(nh):
        k_ref[0, :, h * LANES:(h + 1) * LANES] = (kn[:, h * LANES:(h + 1) * LANES] + krr).astype(k_ref.dtype)
    vt = _dot(xn, wv_ref[...]).T
    pad = MLA_VROWS - MLA_V
    ones_rows = jnp.where(lax.broadcasted_iota(jnp.int32, (pad, tm), 0) == 0, 1.0, 0.0).astype(vt_ref.dtype)
    for h in range(nh):
        vt_ref[0, 0, h * MLA_VROWS:h * MLA_VROWS + MLA_V, :] = vt[h * MLA_V:(h + 1) * MLA_V].astype(vt_ref.dtype)
        vt_ref[0, 0, h * MLA_VROWS + MLA_V:(h + 1) * MLA_VROWS, :] = ones_rows


def _mla_kv(ckv, kr, kvnorm, wk_p, wv, tabs, tm):
    Bx, S, R = ckv.shape
    NK, NV = wk_p.shape[1], wv.shape[1]
    kern = functools.partial(_mla_kv_kernel, nh=MLA_HEADS, hh=MLA_ROPE // 4, tm=tm)
    tab = pl.BlockSpec((tm, LANES), lambda b, i: (i, 0))
    VR = MLA_HEADS * MLA_VROWS
    return pl.pallas_call(
        kern, name="mla_kv", grid=(Bx, S // tm),
        in_specs=[pl.BlockSpec((1, tm, R), lambda b, i: (b, i, 0)),
                  pl.BlockSpec((1, tm, LANES), lambda b, i: (b, i, 0)),
                  _const_spec((1, R)), _const_spec((R, NK)), _const_spec((R, NV)), tab, tab, tab],
        out_specs=[pl.BlockSpec((1, tm, NK), lambda b, i: (b, i, 0)),
                   pl.BlockSpec((1, 1, VR, tm), lambda b, i: (b, i, 0, 0))],
        out_shape=[jax.ShapeDtypeStruct((Bx, S, NK), BF16), jax.ShapeDtypeStruct((Bx, S // tm, VR, tm), BF16)],
        compiler_params=_params(("parallel", "arbitrary")),
    )(ckv, kr, kvnorm.reshape(1, R), wk_p, wv, *tabs)


def _mla_attn_kernel(*refs, tq, tk, n_chunks):
    n_in = 7 if n_chunks else 5
    if n_chunks:
        qt_ref, kc_ref, vct_ref, k_ref, vt_ref, gate_ref, o_ref = refs[:n_in]
    else:
        qt_ref, kc_ref, vct_ref, gate_ref, o_ref = refs[:n_in]
    acc_scr = refs[n_in:n_in + 2]
    VR = MLA_VROWS
    qs = [qt_ref[0, j * LANES:(j + 1) * LANES, :] for j in range(2)]

    def accumulate(j, m, st, mx, vblk):
        m_new = mx if m is None else jnp.maximum(m, mx)
        pt = jnp.exp2(st - m_new).astype(BF16)
        pv = _dot(vblk, pt)
        acc_scr[j][...] = pv if m is None else jnp.exp2(m - m_new) * acc_scr[j][...] + pv
        return m_new

    Lc = kc_ref.shape[1]

    def ctx_scores(j):
        return _dot(kc_ref[0, :, j * LANES:(j + 1) * LANES], qs[j])

    if not n_chunks:
        for j in range(2):
            st = ctx_scores(j)
            accumulate(j, None, st, jnp.max(st, 0, keepdims=True), vct_ref[0, 0, j * VR:(j + 1) * VR, :])
    else:
        s_scr = refs[n_in + 2:]

        def scores(j, c, slot):
            st = pl.multiple_of(c * tk, tk)
            blk = _dot(k_ref[0, pl.ds(st, tk), j * LANES:(j + 1) * LANES], qs[j])
            s_scr[2 * j + slot][...] = blk
            return jnp.max(blk, 0, keepdims=True)

        def scores_ctx(j, slot):
            blk = ctx_scores(j)
            s_scr[2 * j + slot][0:Lc, :] = blk
            return jnp.max(blk, 0, keepdims=True)

        def consume(j, c, slot, mj, mxj):
            return accumulate(j, mj, s_scr[2 * j + slot][...], mxj, vt_ref[0, c, j * VR:(j + 1) * VR, :])

        m = [jnp.full((1, tq), NEG, F32) for _ in range(2)]
        for j in range(2):
            acc_scr[j][...] = jnp.zeros((VR, tq), F32)
        mx = scores(0, 0, 0)

        unroll = 4 if n_chunks % 4 == 0 else 2

        def steps(c, m, mx, last):
            for s in range(unroll):
                slot = s % 2
                for j in range(2):
                    if j == 0:
                        nxt = scores(1, c + s, slot)
                    elif last and s == unroll - 1:
                        nxt = scores_ctx(0, 1 - slot)
                    else:
                        nxt = scores(0, c + s + 1, 1 - slot)
                    m[j] = consume(j, c + s, slot, m[j], mx)
                    mx = nxt
            return m, mx

        def body(ci, carry):
            m0, m1, x = carry
            m, mx = steps(unroll * ci, [m0, m1], x, False)
            return m[0], m[1], mx

        m0, m1, x = lax.fori_loop(0, n_chunks // unroll - 1, body, (m[0], m[1], mx))
        m, mx0 = steps(n_chunks - unroll, [m0, m1], x, True)
        cslot = unroll % 2
        mx1 = scores_ctx(1, cslot)
        for j, mxj in enumerate((mx0, mx1)):
            accumulate(j, m[j], s_scr[2 * j + cslot][0:Lc, :], mxj, vct_ref[0, 0, j * VR:(j + 1) * VR, :])
    yt = jnp.concatenate([acc_scr[j][0:MLA_V, :] / acc_scr[j][MLA_V:MLA_V + 1, :] for j in range(2)], 0)
    o_ref[0] = yt.T * _silu(gate_ref[0])


def _mla_attn(qt, kc, vct, k, vt, gate, tq):
    B, _, S = qt.shape
    Lc = kc.shape[1]
    npair = MLA_HEADS // 2
    VR2 = 2 * MLA_VROWS
    n_chunks, tk = (0, 0) if k is None else (vt.shape[1], vt.shape[3])
    kern = functools.partial(_mla_attn_kernel, tq=tq, tk=tk, n_chunks=n_chunks)
    in_specs = [pl.BlockSpec((1, 2 * LANES, tq), lambda b, p, i: (b, p, i)),
                pl.BlockSpec((1, Lc, 2 * LANES), lambda b, p, i: (b, 0, p)),
                pl.BlockSpec((1, 1, VR2, Lc), lambda b, p, i: (b, 0, p, 0))]
    args = [qt, kc, vct]
    if n_chunks:
        Sk = k.shape[1]
        in_specs += [pl.BlockSpec((1, Sk, 2 * LANES), lambda b, p, i: (b, 0, p)),
                     pl.BlockSpec((1, n_chunks, VR2, tk), lambda b, p, i: (b, 0, p, 0))]
        args += [k, vt]
    in_specs.append(pl.BlockSpec((1, tq, LANES), lambda b, p, i: (b, i, p)))
    args.append(gate)
    return pl.pallas_call(
        kern, name="mla_attn", grid=(B, npair, S // tq),
        in_specs=in_specs,
        out_specs=pl.BlockSpec((1, tq, LANES), lambda b, p, i: (b, i, p)),
        out_shape=jax.ShapeDtypeStruct((B, S, npair * LANES), F32),
        scratch_shapes=[pltpu.VMEM((MLA_VROWS, tq), F32)] * 2 + [pltpu.VMEM((tk, tq), F32)] * (4 if n_chunks else 0),
        compiler_params=_params(("parallel", "arbitrary", "arbitrary")),
    )(*args)


SWA_VROWS = 80


def _ones_rows(n, width, dtype):
    return jnp.where(lax.broadcasted_iota(jnp.int32, (n, width), 0) == 0, 1.0, 0.0).astype(dtype)


def _swa_prep_kernel(q_ref, k_ref, v_ref, c_ref, sm_ref, sp_ref, qo_ref, ko_ref, vo_ref, *, scale, hh, nq, tm):
    c, sm, sp = c_ref[...], sm_ref[...], sp_ref[...]
    for r in range(nq):
        qr = _rope(q_ref[0, :, r * LANES:(r + 1) * LANES], c, sm, sp, hh) * scale
        qo_ref[0, r * LANES:(r + 1) * LANES, :] = qr.T.astype(qo_ref.dtype)
    ko_ref[0] = _rope(k_ref[0], c, sm, sp, hh).astype(ko_ref.dtype)
    vt = v_ref[0].T
    hd = LANES // SWA_KV_HEADS
    ones = _ones_rows(SWA_VROWS - hd, tm, vo_ref.dtype)
    for g in range(SWA_KV_HEADS):
        vo_ref[0, g * SWA_VROWS:g * SWA_VROWS + hd, :] = vt[g * hd:(g + 1) * hd].astype(vo_ref.dtype)
        vo_ref[0, g * SWA_VROWS + hd:(g + 1) * SWA_VROWS, :] = ones


def _swa_prep(sq, sk, sv, tabs, tm):
    Bx, S, NQ = sq.shape
    VR = SWA_KV_HEADS * SWA_VROWS
    kern = functools.partial(_swa_prep_kernel, scale=SWA_HD ** -0.5 * LOG2E, hh=SWA_HD // 4, nq=NQ // LANES, tm=tm)
    tab = pl.BlockSpec((tm, LANES), lambda b, i: (i, 0))
    qs = pl.BlockSpec((1, tm, NQ), lambda b, i: (b, i, 0))
    ks = pl.BlockSpec((1, tm, LANES), lambda b, i: (b, i, 0))
    return pl.pallas_call(
        kern, name="swa_prep", grid=(Bx, S // tm),
        in_specs=[qs, ks, ks, tab, tab, tab],
        out_specs=[pl.BlockSpec((1, NQ, tm), lambda b, i: (b, 0, i)), ks,
                   pl.BlockSpec((1, VR, tm), lambda b, i: (b, 0, i))],
        out_shape=[jax.ShapeDtypeStruct((Bx, NQ, S), BF16), jax.ShapeDtypeStruct((Bx, S, LANES), BF16),
                   jax.ShapeDtypeStruct((Bx, VR, S), BF16)],
        compiler_params=_params(("parallel", "arbitrary")),
    )(sq, sk, sv, *tabs)


def _sink_column(sink_ref, g, R, W):
    rid = lax.broadcasted_iota(jnp.int32, (R * W, 1), 0)
    col = jnp.full((R * W, 1), sink_ref[g * R + R - 1], F32)
    for r in range(R - 2, -1, -1):
        col = jnp.where(rid < (r + 1) * W, sink_ref[g * R + r], col)
    return col


def _swa_group(qg, sk2, blocks):
    ss = []
    for kk, _, bias in blocks:
        s = _dot_nt(qg, kk)
        ss.append(s if bias is None else s + bias)
    m = sk2
    for s in ss:
        m = jnp.maximum(m, jnp.max(s, -1, keepdims=True))
    o = None
    for s, (_, vv, _) in zip(ss, blocks):
        pv = _dot(jnp.exp2(s - m).astype(BF16), vv)
        o = pv if o is None else o + pv
    den = pltpu.roll(o, LANES // 2, 1) + jnp.exp2(sk2 - m)
    return o / den


def _group_select(x, g, fill):
    lane = lax.broadcasted_iota(jnp.int32, (1, LANES), 1)
    sel = (lane < LANES // 2) if g == 0 else (lane >= LANES // 2)
    return jnp.where(sel, x, jnp.full_like(x, fill))


def _swa_attn_kernel(sink_ref, qt_ref, kp_ref, km_ref, kn_ref, vp_ref, vm_ref, vn_ref, kc_ref, vct_ref, gate_ref,
                     o_ref, *s_scr, bpt, W, R, G, Lc):
    i = pl.program_id(1)
    nblk = pl.num_programs(1) * bpt
    hd = LANES // G
    VR = SWA_VROWS
    kcat = jnp.concatenate([kp_ref[0], km_ref[0], kn_ref[0]], 0)
    vcat = jnp.concatenate([vp_ref[0], vm_ref[0], vn_ref[0]], 1)
    kc = kc_ref[0]
    nk = Lc + 3 * W
    kj = lax.broadcasted_iota(jnp.int32, (nk, R * W), 0) - Lc
    qi = lax.broadcasted_iota(jnp.int32, (nk, R * W), 1) % W
    band_bias = jnp.where((kj < 0) | (jnp.abs(W + qi - kj) <= W), 0.0, NEG)
    head = lax.broadcasted_iota(jnp.int32, (1, R * W), 1) // W
    row = lax.broadcasted_iota(jnp.int32, (LANES, 1), 0)
    sk2 = []
    for g in range(G):
        sk = jnp.full((1, R * W), sink_ref[g * R + R - 1], F32)
        for r in range(R - 2, -1, -1):
            sk = jnp.where(head <= r, sink_ref[g * R + r], sk)
        sk2.append(sk * LOG2E)
    kj_col = lax.broadcasted_iota(jnp.int32, (nk, 1), 0) - Lc
    items = [(jb, g) for jb in range(bpt) for g in range(G)]

    def scores(n):
        jb, g = items[n]
        gblk = i * bpt + jb
        keys = jnp.concatenate([kc, kcat[jb * W:(jb + 3) * W]], 0)
        pen_prev = jnp.where(gblk > 0, 0.0, NEG)
        pen_next = jnp.where(gblk < nblk - 1, 0.0, NEG)
        pen = jnp.where((kj_col >= 0) & (kj_col < W), pen_prev, jnp.where(kj_col >= 2 * W, pen_next, 0.0))
        sel = (row >= g * hd) & (row < (g + 1) * hd)
        tiles = [qt_ref[0, r * LANES:(r + 1) * LANES, jb * W:(jb + 1) * W] for r in range(R)]
        qg = jnp.concatenate([jnp.where(sel, t, jnp.zeros_like(t)) for t in tiles], 1)
        st = (_dot(keys, qg) + band_bias) + pen
        s_scr[n % 2][...] = st
        return jnp.max(st, 0, keepdims=True)

    def consume(n, mx):
        jb, g = items[n]
        m = jnp.maximum(mx, sk2[g])
        pt = jnp.exp2(s_scr[n % 2][...] - m).astype(BF16)
        o = (_dot(vct_ref[0, g * VR:(g + 1) * VR, :], pt[0:Lc])
             + _dot(vcat[g * VR:(g + 1) * VR, jb * W:(jb + 3) * W], pt[Lc:nk]))
        den = o[hd:hd + 1] + jnp.exp2(sk2[g] - m)
        return o[0:hd] / den

    mx = scores(0)
    outg = []
    for n, (jb, g) in enumerate(items):
        nxt = scores(n + 1) if n + 1 < len(items) else None
        outg.append(consume(n, mx))
        mx = nxt
        if g == G - 1:
            for r in range(R):
                yt = jnp.concatenate([og[:, r * W:(r + 1) * W] for og in outg], 0)
                gt = gate_ref[0, jb * W:(jb + 1) * W, r * LANES:(r + 1) * LANES]
                o_ref[0, jb * W:(jb + 1) * W, r * LANES:(r + 1) * LANES] = yt.T * _silu(gt)
            outg = []


def _swa_attn(qt, k, vt, kc, vct, sink, gate, bpt):
    B, NQ, S = qt.shape
    Lc = kc.shape[1]
    W = WINDOW
    G = SWA_KV_HEADS
    R = SWA_HEADS // G
    VR = G * SWA_VROWS
    nb = S // W
    T = bpt * W
    kern = functools.partial(_swa_attn_kernel, bpt=bpt, W=W, R=R, G=G, Lc=Lc)
    main = pl.BlockSpec((1, T, LANES), lambda b, i: (b, i, 0))
    prev = pl.BlockSpec((1, W, LANES), lambda b, i: (b, jnp.maximum(i * bpt - 1, 0), 0))
    nxt = pl.BlockSpec((1, W, LANES), lambda b, i: (b, jnp.minimum((i + 1) * bpt, nb - 1), 0))
    main_t = pl.BlockSpec((1, VR, T), lambda b, i: (b, 0, i))
    prev_t = pl.BlockSpec((1, VR, W), lambda b, i: (b, 0, jnp.maximum(i * bpt - 1, 0)))
    nxt_t = pl.BlockSpec((1, VR, W), lambda b, i: (b, 0, jnp.minimum((i + 1) * bpt, nb - 1)))
    tok = pl.BlockSpec((1, T, NQ), lambda b, i: (b, i, 0))
    return pl.pallas_call(
        kern, name="swa_attn", grid=(B, S // T),
        in_specs=[pl.BlockSpec(memory_space=pltpu.SMEM), pl.BlockSpec((1, NQ, T), lambda b, i: (b, 0, i)),
                  prev, main, nxt, prev_t, main_t, nxt_t,
                  pl.BlockSpec((1, Lc, LANES), lambda b, i: (b, 0, 0)),
                  pl.BlockSpec((1, VR, Lc), lambda b, i: (b, 0, 0)), tok],
        out_specs=tok,
        out_shape=jax.ShapeDtypeStruct((B, S, NQ), F32),
        scratch_shapes=[pltpu.VMEM((Lc + 3 * W, R * W), F32)] * 2,
        compiler_params=_params(("parallel", "arbitrary")),
    )(sink, qt, k, k, k, vt, vt, vt, kc, vct, gate)


def _swa_ctx_kernel(sink_ref, q_ref, kc_ref, vc_ref, gate_ref, o_ref, *, Lc, R, G, scale):
    kc = kc_ref[0]
    lane = lax.broadcasted_iota(jnp.int32, (Lc, LANES), 1)
    tiles = [(q_ref[0, :, r * LANES:(r + 1) * LANES] * scale).astype(BF16) for r in range(R)]
    outg = []
    for g in range(G):
        qg = jnp.concatenate([_group_select(t, g, 0.0) for t in tiles], 0)
        outg.append(_swa_group(qg, _sink_column(sink_ref, g, R, Lc) * LOG2E,
                               [(kc, _group_select(vc_ref[0], g, 1.0), None)]))
    for r in range(R):
        y = jnp.where(lane < LANES // 2, outg[0][r * Lc:(r + 1) * Lc], outg[1][r * Lc:(r + 1) * Lc])
        o_ref[0, :, r * LANES:(r + 1) * LANES] = y * _silu(gate_ref[0, :, r * LANES:(r + 1) * LANES])


def _swa_ctx(q, kc, vc, sink, gate):
    B, Lc, NQ = q.shape
    G = SWA_KV_HEADS
    R = SWA_HEADS // G
    kern = functools.partial(_swa_ctx_kernel, Lc=Lc, R=R, G=G, scale=SWA_HD ** -0.5 * LOG2E)
    qs = pl.BlockSpec((1, Lc, NQ), lambda b: (b, 0, 0))
    ctx = pl.BlockSpec((1, Lc, LANES), lambda b: (b, 0, 0))
    return pl.pallas_call(
        kern, name="swa_ctx", grid=(B,),
        in_specs=[pl.BlockSpec(memory_space=pltpu.SMEM), qs, ctx, ctx, qs],
        out_specs=qs,
        out_shape=jax.ShapeDtypeStruct((B, Lc, NQ), F32),
        compiler_params=_params(("arbitrary",)),
    )(sink, q, kc, vc, gate)


def _merge_kernel(yh_ref, ym_ref, ys_ref, mg_ref, x_ref, mod_ref, wh, wm, ws, wo, lg, lb, o_ref, *, D, alpha):
    m = (jax.nn.sigmoid(mg_ref[0, :, 0:D]) * _dot(yh_ref[0].astype(BF16), wh[...])
         + jax.nn.sigmoid(mg_ref[0, :, D:2 * D]) * _dot(ym_ref[0].astype(BF16), wm[...])
         + jax.nn.sigmoid(mg_ref[0, :, 2 * D:3 * D]) * _dot(ys_ref[0].astype(BF16), ws[...]))
    out = _dot(m.astype(BF16), wo[...])
    r = alpha * x_ref[0] + mod_ref[0, :, 2 * D:3 * D] * out
    mu = jnp.mean(r, -1, keepdims=True)
    rc = r - mu
    var = jnp.mean(rc * rc, -1, keepdims=True)
    o_ref[0] = rc * lax.rsqrt(var + EPS) * lg[...] + lb[...]


def _merge(yh, ym, ys, mg, x, mod, wh, wm, ws, wo, lg, lb, alpha, tm):
    Bx, S, D = x.shape
    kern = functools.partial(_merge_kernel, D=D, alpha=alpha)
    tok = lambda w: pl.BlockSpec((1, tm, w), lambda b, i: (b, i, 0))
    cs = _const_spec
    return pl.pallas_call(
        kern, name="merge", grid=(Bx, S // tm),
        in_specs=[tok(yh.shape[2]), tok(ym.shape[2]), tok(ys.shape[2]), tok(3 * D), tok(D),
                  pl.BlockSpec((1, 1, 3 * D), lambda b, i: (b, 0, 0)),
                  cs(wh.shape), cs(wm.shape), cs(ws.shape), cs(wo.shape), cs((1, D)), cs((1, D))],
        out_specs=tok(D),
        out_shape=jax.ShapeDtypeStruct((Bx, S, D), F32),
        compiler_params=_params(("parallel", "arbitrary")),
    )(yh, ym, ys, mg, x, mod, wh, wm, ws, wo, lg.reshape(1, D), lb.reshape(1, D))


_PAIR_ORDER = (0, 4, 1, 5, 2, 6, 3, 7)

_IN_COLS = (3 * HY_W, HY_W, MLA_Q_RANK, MLA_KV_RANK, MLA_ROPE, MLA_HEADS * MLA_V,
            SWA_HEADS * SWA_HD, SWA_KV_HEADS * SWA_HD, SWA_KV_HEADS * SWA_HD, SWA_HEADS * SWA_HD)
_SLAB_WIDTHS = (None, 3 * HY_W, HY_W, MLA_Q_RANK, MLA_KV_RANK, LANES, MLA_HEADS * MLA_V,
                SWA_HEADS * SWA_HD, LANES, LANES, SWA_HEADS * SWA_HD)


def _pair_cols(w):
    D = w.shape[0]
    return w.reshape(D, SWA_HEADS, SWA_HD)[:, _PAIR_ORDER, :].reshape(D, SWA_HEADS * SWA_HD)


def _layout_w_in(w_in, D):
    offs = [0]
    for cw in _IN_COLS:
        offs.append(offs[-1] + cw)
    sl = [w_in[:, offs[i]:offs[i + 1]] for i in range(len(_IN_COLS))]
    hy, hyg, cq, ckv, kr, mlag, sq, sk, sv, swag = sl
    mg = w_in[:, offs[-1]:]
    kr_p = jnp.pad(kr, ((0, 0), (MLA_NOPE, LANES - MLA_NOPE - MLA_ROPE)))
    return jnp.concatenate([mg, hy, hyg, cq, ckv, kr_p, mlag, _pair_cols(sq), sk, sv, _pair_cols(swag)],
                           1).astype(BF16)


def _layer_weights(p, D):
    w = {}
    w['w_in'] = _layout_w_in(p['w_in'], D)
    uq = p['mla_w_uq'].reshape(MLA_Q_RANK, MLA_HEADS, MLA_QK)
    w['w_uq'] = jnp.pad(uq, ((0, 0), (0, 0), (0, LANES - MLA_QK))).reshape(MLA_Q_RANK, MLA_HEADS * LANES).astype(BF16)
    half = MLA_ROPE // 4
    rot = uq[:, :, MLA_NOPE:].reshape(MLA_Q_RANK, MLA_HEADS, 2, 2, half)[:, :, :, ::-1, :]
    rot = rot.reshape(MLA_Q_RANK, MLA_HEADS, MLA_ROPE)
    w['w_uq_r'] = jnp.pad(rot, ((0, 0), (0, 0), (MLA_NOPE, LANES - MLA_QK))).reshape(
        MLA_Q_RANK, MLA_HEADS * LANES).astype(BF16)
    ukv = p['mla_w_ukv'].reshape(MLA_KV_RANK, MLA_HEADS, MLA_NOPE + MLA_V)
    w['w_uk'] = jnp.pad(ukv[:, :, :MLA_NOPE], ((0, 0), (0, 0), (0, LANES - MLA_NOPE))).reshape(
        MLA_KV_RANK, MLA_HEADS * LANES).astype(BF16)
    w['w_uv'] = ukv[:, :, MLA_NOPE:].reshape(MLA_KV_RANK, MLA_HEADS * MLA_V).astype(BF16)
    w['w_proj_hy'] = p['w_proj_hy'].astype(BF16)
    w['w_proj_mla'] = p['w_proj_mla'].astype(BF16)
    w['w_proj_swa'] = p['w_proj_swa'].reshape(SWA_HEADS, SWA_HD, D)[_PAIR_ORDER, :, :].reshape(
        SWA_HEADS * SWA_HD, D).astype(BF16)
    w['w_out'] = p['w_out'].astype(BF16)
    return w


def _trunk_layer(xl, xc, p, tabs, alpha, ctx_out):
    B, S, D = xl.shape
    Lc = xc.shape[1]
    w = _layer_weights(p, D)
    widths = (N_BRANCH * D,) + _SLAB_WIDTHS[1:]

    mod = p['mod']
    mod_l = mod[:B].reshape(B, 1, 3 * D)
    mod_c = jnp.broadcast_to(mod[B].reshape(1, 1, 3 * D), (B, 1, 3 * D))

    tm_c = min(Lc, 256)
    (mg_l, hy_l, hyg_l, cq_l, ckv_l, kr_l, mlag_l, sq_l, sk_l, sv_l, swag_l) = _in_proj(xl, mod_l, w['w_in'], widths, 512)
    (mg_c, hy_c, hyg_c, cq_c, ckv_c, kr_c, mlag_c, sq_c, sk_c, sv_c, swag_c) = _in_proj(xc, mod_c, w['w_in'], widths, tm_c)

    k_mc, v_mc = _mla_kv(ckv_c, kr_c, p['mla_kv_norm'], w['w_uk'], w['w_uv'], tabs['id_c'], tm_c)
    k_sc = sk_c.astype(BF16)
    v_sc = sv_c.astype(BF16)

    u_l, e_l = _hy_pre(hy_l, hyg_l, p['hy_conv_w'], p['hy_conv_b'], 512)
    h2_l = _hy_filters(S, p)
    y_hy = _hy_long_conv(u_l, e_l, h2_l, p['hy_skip'], tabs['dft'])

    q_ml = _mla_q(cq_l, p['mla_q_norm'], w['w_uq'], w['w_uq_r'], tabs['mla'], 512)
    k_ml, v_ml = _mla_kv(ckv_l, kr_l, p['mla_kv_norm'], w['w_uk'], w['w_uv'], tabs['mla'], 512)
    y_mla = _mla_attn(q_ml, k_mc, v_mc, k_ml, v_ml, mlag_l, 512)

    q_sl, k_sl, v_sl = _swa_prep(sq_l, sk_l, sv_l, tabs['swa'], 512)
    hd = LANES // SWA_KV_HEADS
    ones = jnp.broadcast_to(_ones_rows(SWA_VROWS - hd, Lc, BF16)[None], (B, SWA_VROWS - hd, Lc))
    v_sct = jnp.swapaxes(v_sc, 1, 2)
    v_sct = jnp.concatenate([v_sct[:, :hd], ones, v_sct[:, hd:], ones], 1)
    y_swa = _swa_attn(q_sl, k_sl, v_sl, k_sc, v_sct, p['swa_sink'], swag_l, 4)

    xl_new = _merge(y_hy, y_mla, y_swa, mg_l, xl, mod_l, w['w_proj_hy'], w['w_proj_mla'], w['w_proj_swa'],
                    w['w_out'], p['ln_g'], p['ln_b'], alpha, 256)
    if not ctx_out:
        return xl_new, xc

    u_c, e_c = _hy_pre(hy_c, hyg_c, p['hy_conv_w'], p['hy_conv_b'], tm_c)
    h2_c = _hy_filters(Lc, p)
    yc_hy = _hy_ctx_conv(u_c, e_c, h2_c, p['hy_skip'], tabs['dft_c'])
    q_mc = _mla_q(cq_c, p['mla_q_norm'], w['w_uq'], w['w_uq_r'], tabs['id_c'], tm_c)
    yc_mla = _mla_attn(q_mc, k_mc, v_mc, None, None, mlag_c, tm_c)
    yc_swa = _swa_ctx(sq_c, k_sc, v_sc, p['swa_sink'], swag_c)
    xc_new = _merge(yc_hy, yc_mla, yc_swa, mg_c, xc, mod_c, w['w_proj_hy'], w['w_proj_mla'], w['w_proj_swa'],
                    w['w_out'], p['ln_g'], p['ln_b'], alpha, tm_c)
    return xl_new, xc_new


def kernel(x, c, ctx, c_ctx, w_ada, b_ada, w_in, hy_conv_w, hy_conv_b, filt_w1, filt_b1, filt_w2, filt_b2,
           filt_w3, filt_b3, filt_freq, filt_w_out, hy_skip, mla_q_norm, mla_w_uq, mla_kv_norm, mla_w_ukv,
           swa_sink, w_proj_hy, w_proj_mla, w_proj_swa, w_out, ln_g, ln_b):
    B, S, D = x.shape
    Lc = ctx.shape[1]
    depth = w_in.shape[0]
    alpha = (2 * depth) ** 0.25
    stacked = dict(w_ada=w_ada, b_ada=b_ada, w_in=w_in, hy_conv_w=hy_conv_w, hy_conv_b=hy_conv_b,
                   filt_w1=filt_w1, filt_b1=filt_b1, filt_w2=filt_w2, filt_b2=filt_b2, filt_w3=filt_w3,
                   filt_b3=filt_b3, filt_freq=filt_freq, filt_w_out=filt_w_out, hy_skip=hy_skip,
                   mla_q_norm=mla_q_norm, mla_w_uq=mla_w_uq, mla_kv_norm=mla_kv_norm, mla_w_ukv=mla_w_ukv,
                   swa_sink=swa_sink, w_proj_hy=w_proj_hy, w_proj_mla=w_proj_mla, w_proj_swa=w_proj_swa,
                   w_out=w_out, ln_g=ln_g, ln_b=ln_b)

    lane = jnp.arange(LANES, dtype=jnp.int32)
    mla_on = (lane >= MLA_NOPE) & (lane < MLA_QK)
    tabs = {
        'mla': _rope_tables(S, jnp.clip(lane - MLA_NOPE, 0, MLA_ROPE - 1), mla_on, MLA_ROPE),
        'swa': _rope_tables(S, lane % SWA_HD, jnp.ones((LANES,), bool), SWA_HD),
        'id_c': _rope_tables(Lc, None, None, None, identity=True),
        'dft': _dft_tables(S),
        'dft_c': _ctx_dft_tables(Lc),
    }
    cvec = jnp.concatenate([c, c_ctx[None, :], jnp.zeros((8 - B - 1, D), F32)], 0)

    xl, xc = x, ctx
    for l in range(depth):
        p = {k: v[l] for k, v in stacked.items() if k != 'w_ada'}
        p['mod'] = _ada_mod(cvec, w_ada, p['b_ada'], l)
        xl, xc = _trunk_layer(xl, xc, p, tabs, alpha, l < depth - 1)
    return xl
```

```python
import functools
import math

import jax
import jax.numpy as jnp
from jax import lax
from jax.experimental import pallas as pl
from jax.experimental.pallas import tpu as pltpu

F32 = jnp.float32
BF16 = jnp.bfloat16
HIGHEST = lax.Precision.HIGHEST

GRID_W = 64
HY_W = 512
SHORT_K = 3
FILT_BANDS = 16
FILT_W = 64
DECAY_TARGET = 1e-2
FAST_DECAY_PCT = 0.3
SLOW_DECAY_PCT = 1.5
MLA_HEADS = 8
MLA_NOPE = 64
MLA_ROPE = 32
MLA_V = 64
MLA_QK = MLA_NOPE + MLA_ROPE
MLA_Q_RANK = 256
MLA_KV_RANK = 128
SWA_HEADS = 8
SWA_KV_HEADS = 2
SWA_HD = 64
WINDOW = 128
N_BRANCH = 3
ROPE_BASE = 10000.0
EPS = 1e-6
NEG = -1e30
LOG2E = math.log2(math.e)

LANES = 128
DFT_N2 = 128
VMEM_LIMIT = 56 * 1024 * 1024


def _silu(x):
    return x * jax.nn.sigmoid(x)


def _dot(a, b):
    return jnp.dot(a, b, preferred_element_type=F32)


def _dot_nt(a, b):
    return lax.dot_general(a, b, (((1,), (1,)), ((), ())), preferred_element_type=F32)


def _params(sem):
    return pltpu.CompilerParams(dimension_semantics=sem, vmem_limit_bytes=VMEM_LIMIT)


def _const_spec(shape):
    nd = len(shape)
    return pl.BlockSpec(shape, lambda *_: (0,) * nd)


def _ada_kernel(c_ref, w_ref, b_ref, o_ref):
    a = _silu(c_ref[...])
    o_ref[...] = jnp.dot(a, w_ref[...], precision=HIGHEST, preferred_element_type=F32) + b_ref[...]


def _ada_mod(cvec, w_ada, b_ada, layer):
    R, D = cvec.shape
    N = w_ada.shape[2]
    tn = 768
    return pl.pallas_call(
        _ada_kernel, name="ada_mod", grid=(N // tn,),
        in_specs=[_const_spec((R, D)), pl.BlockSpec((None, D, tn), lambda j: (layer, 0, j)),
                  pl.BlockSpec((1, tn), lambda j: (0, j))],
        out_specs=pl.BlockSpec((R, tn), lambda j: (0, j)),
        out_shape=jax.ShapeDtypeStruct((R, N), F32),
        compiler_params=_params(("arbitrary",)),
    )(cvec, w_ada, b_ada.reshape(1, N))


def _in_proj_kernel(x_ref, mod_ref, w_ref, *o_refs, widths, D, tm, sub):
    shift = mod_ref[0, :, 0:D]
    scale1 = 1.0 + mod_ref[0, :, D:2 * D]
    us = []
    for r in range(0, tm, sub):
        x = x_ref[0, r:r + sub, :]
        mu = jnp.mean(x, -1, keepdims=True)
        xc = x - mu
        var = jnp.mean(xc * xc, -1, keepdims=True)
        us.append(((xc * lax.rsqrt(var + EPS)) * scale1 + shift).astype(BF16))
    off = 0
    for o_ref, wd in zip(o_refs, widths):
        for k, u in enumerate(us):
            o_ref[0, k * sub:(k + 1) * sub, :] = _dot(u, w_ref[:, off:off + wd]).astype(o_ref.dtype)
        off += wd


def _in_proj(x, mod, w, widths, tm):
    Bx, S, D = x.shape
    P = w.shape[1]
    kern = functools.partial(_in_proj_kernel, widths=tuple(widths), D=D, tm=tm, sub=min(tm, 256))
    return pl.pallas_call(
        kern, name="in_proj", grid=(Bx, S // tm),
        in_specs=[pl.BlockSpec((1, tm, D), lambda b, i: (b, i, 0)),
                  pl.BlockSpec((1, 1, 3 * D), lambda b, i: (b, 0, 0)),
                  pl.BlockSpec((D, P), lambda b, i: (0, 0), pipeline_mode=pl.Buffered(1))],
        out_specs=[pl.BlockSpec((1, tm, wd), lambda b, i: (b, i, 0)) for wd in widths],
        out_shape=[jax.ShapeDtypeStruct((Bx, S, wd), F32) for wd in widths],
        compiler_params=_params(("parallel", "arbitrary")),
    )(x, mod, w)


def _filter_kernel(z_ref, t_ref, w1, b1, w2, b2, w3, b3, fr, wo, dl_ref, o_ref, *, tl, C):
    hp = lambda a, b: jnp.dot(a, b, precision=HIGHEST, preferred_element_type=F32)
    f = fr[...]
    h = jnp.sin(f * (hp(z_ref[...], w1[...]) + b1[...]))
    h = jnp.sin(f * (hp(h, w2[...]) + b2[...]))
    h = jnp.sin(f * (hp(h, w3[...]) + b3[...]))
    dl = jnp.abs(dl_ref[...])
    half = tl // 2
    for s in range(2):
        rows = slice(s * half, (s + 1) * half)
        o = hp(h, wo[s])
        decay = jnp.exp(-t_ref[rows, :] * dl)
        o_ref[rows, 0:C] = o[:, 0:C] * decay
        bwd = o[:, C:2 * C] * decay
        if s == 0:
            row = pl.program_id(0) * tl + lax.broadcasted_iota(jnp.int32, (half, 1), 0)
            bwd = jnp.where(row == 0, 0.0, bwd)
        o_ref[rows, C:2 * C] = bwd


def _hy_filters(L, p):
    C = HY_W
    FP = LANES
    FH = FP // 2
    t = jnp.linspace(0.0, 1.0, L, dtype=F32)[:, None]
    w = 2.0 * math.pi * jnp.arange(L, dtype=F32) / L
    f = jnp.linspace(1e-4, FILT_BANDS - 1, FILT_BANDS, dtype=F32)
    ang = w[:, None] * f[None, :]
    z = jnp.concatenate([t, jnp.cos(ang), -jnp.sin(ang)], -1)
    z = jnp.pad(z, ((0, 0), (0, FH - z.shape[1])))
    tl = min(L, 512)
    z = z.reshape(L // tl, 2, tl // 2, FH).transpose(0, 2, 1, 3).reshape(L // 2, FP)
    pad_h = lambda a: jnp.pad(a, ((0, FH - a.shape[0]), (0, FH - a.shape[1])))
    bdiag = lambda a: jnp.kron(jnp.eye(2, dtype=F32), pad_h(a))
    pad_v = lambda a: jnp.tile(jnp.pad(a, (0, FH - a.shape[0])), 2).reshape(1, FP)
    w1, w2, w3 = bdiag(p['filt_w1']), bdiag(p['filt_w2']), bdiag(p['filt_w3'])
    wo_h = jnp.pad(p['filt_w_out'], ((0, FH - FILT_W), (0, 0)))
    zeros = jnp.zeros_like(wo_h)
    wo = jnp.stack([jnp.concatenate([wo_h, zeros], 0), jnp.concatenate([zeros, wo_h], 0)], 0)
    min_decay = math.log(DECAY_TARGET) / SLOW_DECAY_PCT
    max_decay = math.log(DECAY_TARGET) / FAST_DECAY_PCT
    deltas = jnp.linspace(min_decay, max_decay, C, dtype=F32).reshape(1, C)
    kern = functools.partial(_filter_kernel, tl=tl, C=C)
    cs = _const_spec
    return pl.pallas_call(
        kern, name="hy_filter", grid=(L // tl,),
        in_specs=[pl.BlockSpec((tl // 2, FP), lambda i: (i, 0)), pl.BlockSpec((tl, 1), lambda i: (i, 0)),
                  cs((FP, FP)), cs((1, FP)), cs((FP, FP)), cs((1, FP)), cs((FP, FP)), cs((1, FP)),
                  cs((1, FP)), cs((2, FP, 2 * C)), cs((1, C))],
        out_specs=pl.BlockSpec((tl, 2 * C), lambda i: (i, 0)),
        out_shape=jax.ShapeDtypeStruct((L, 2 * C), F32),
        compiler_params=_params(("arbitrary",)),
    )(z, t, w1, pad_v(p['filt_b1']), w2, pad_v(p['filt_b2']), w3, pad_v(p['filt_b3']),
      pad_v(p['filt_freq']), wo, deltas)


def _hy_pre_kernel(x_ref, xp_ref, xn_ref, g_ref, w_ref, b_ref, u_ref, e_ref, *, ts, C):
    i = pl.program_id(1)
    nt = pl.num_programs(1)
    x = x_ref[0]
    prev_row = xp_ref[0, 7:8, :] * jnp.where(i > 0, 1.0, 0.0)
    next_row = xn_ref[0, 0:1, :] * jnp.where(i < nt - 1, 1.0, 0.0)
    rid = lax.broadcasted_iota(jnp.int32, (ts, 1), 0)
    xm = jnp.where(rid == 0, prev_row, pltpu.roll(x, 1, 0))
    xq = jnp.where(rid == ts - 1, next_row, pltpu.roll(x, ts - 1, 0))
    z = b_ref[...] + xm * w_ref[0:1, :] + x * w_ref[1:2, :] + xq * w_ref[2:3, :]
    u_ref[0] = z[:, 2 * C:3 * C] * z[:, C:2 * C]
    e_ref[0] = z[:, 0:C] * _silu(g_ref[0])


def _hy_pre(hy, hyg, conv_w, conv_b, ts):
    Bx, S, C3 = hy.shape
    C = C3 // 3
    nb8 = S // 8
    r = ts // 8
    w8 = jnp.pad(conv_w, ((0, 8 - SHORT_K), (0, 0)))
    kern = functools.partial(_hy_pre_kernel, ts=ts, C=C)
    return pl.pallas_call(
        kern, name="hy_pre", grid=(Bx, S // ts),
        in_specs=[pl.BlockSpec((1, ts, C3), lambda b, i: (b, i, 0)),
                  pl.BlockSpec((1, 8, C3), lambda b, i: (b, jnp.maximum(i * r - 1, 0), 0)),
                  pl.BlockSpec((1, 8, C3), lambda b, i: (b, jnp.minimum((i + 1) * r, nb8 - 1), 0)),
                  pl.BlockSpec((1, ts, C), lambda b, i: (b, i, 0)),
                  _const_spec((8, C3)), _const_spec((1, C3))],
        out_specs=[pl.BlockSpec((1, ts, C), lambda b, i: (b, i, 0))] * 2,
        out_shape=[jax.ShapeDtypeStruct((Bx, S, C), F32)] * 2,
        compiler_params=_params(("parallel", "arbitrary")),
    )(hy, hy, hy, hyg, w8, conv_b.reshape(1, C3))


def _dft_tables(L):
    n = 2 * L
    N2 = DFT_N2
    N1 = n // N2
    H1 = N1 // 2
    k1 = jnp.arange(N1, dtype=jnp.int32)
    t1 = jnp.arange(H1, dtype=jnp.int32)
    ang = (2.0 * math.pi / N1) * ((k1[:, None] * t1[None, :]) % N1).astype(F32)
    fa = jnp.stack([jnp.cos(ang), -jnp.sin(ang)], 1).reshape(2 * N1, H1)
    fai = jnp.concatenate([jnp.cos(ang), -jnp.sin(ang)], 0).T * (1.0 / n)
    k2 = jnp.arange(N2, dtype=jnp.int32)
    t2 = jnp.arange(N2, dtype=jnp.int32)
    a2 = (2.0 * math.pi / N2) * ((k2[:, None] * t2[None, :]) % N2).astype(F32)
    aw = (2.0 * math.pi / n) * ((k1[:, None] * t2[None, :]) % n).astype(F32)
    fr, fi = jnp.cos(a2)[None], -jnp.sin(a2)[None]
    wr, wi = jnp.cos(aw)[:, None, :], -jnp.sin(aw)[:, None, :]
    cr, ci = fr * wr - fi * wi, fr * wi + fi * wr
    mf = jnp.concatenate([jnp.concatenate([cr, -ci], 2), jnp.concatenate([ci, cr], 2)], 1)
    return fa.astype(BF16), fai.astype(BF16), mf.astype(BF16)


DFT_TT = 16
DFT_KB = 8
SUBLANES = 8
DFT_PITCH_B = 24
DFT_PITCH_A = 40


def _halves(shape_fn, idx_fn):
    return [pl.BlockSpec(shape_fn(SUBLANES), functools.partial(idx_fn, half=h)) for h in range(2)]


def _dft_a_fwd_kernel(xlo_ref, xhi_ref, f_ref, o_ref):
    f = f_ref[...]
    for j in range(DFT_TT):
        src = xlo_ref if j < SUBLANES else xhi_ref
        o_ref[0, j] = _dot(f, src[0, :, j % SUBLANES, :].astype(BF16)).astype(o_ref.dtype)


def _dft_a_fwd(x4, fa):
    Bx, H1, N2, C = x4.shape
    R = fa.shape[0]
    tok = _halves(lambda r: (1, H1, r, C), lambda b, j, half: (b, 0, 2 * j + half, 0))
    return pl.pallas_call(
        _dft_a_fwd_kernel, name="dft_a_fwd", grid=(Bx, N2 // DFT_TT),
        in_specs=tok + [_const_spec((R, H1))],
        out_specs=pl.BlockSpec((1, DFT_TT, R, C), lambda b, j: (b, j, 0, 0)),
        out_shape=jax.ShapeDtypeStruct((Bx, N2, R, C), BF16),
        compiler_params=_params(("parallel", "arbitrary")),
    )(x4, x4, fa)


def _stage_b_operand(scr, j):
    return jnp.concatenate([scr[:, 2 * j, :], scr[:, 2 * j + 1, :]], 0).astype(BF16)


def _dft_b_filter_kernel(a_ref, m_ref, k_ref, scr, *, kb, C, N2):
    scr[:, 0:2 * kb, :] = a_ref[0].astype(F32)
    for j in range(kb):
        h = _dot(m_ref[j], _stage_b_operand(scr, j))
        k_ref[j, 0:N2, :] = h[0:N2, 0:C] + h[0:N2, C:2 * C]
        k_ref[j, N2:2 * N2, :] = h[N2:2 * N2, 0:C] - h[N2:2 * N2, C:2 * C]


def _dft_b_filter(ah, mf):
    _, N2, R, C2 = ah.shape
    N1, C, kb = R // 2, C2 // 2, DFT_KB
    kern = functools.partial(_dft_b_filter_kernel, kb=kb, C=C, N2=N2)
    return pl.pallas_call(
        kern, name="dft_b_filter", grid=(N1 // kb,),
        in_specs=[pl.BlockSpec((1, N2, 2 * kb, C2), lambda i: (0, 0, i, 0)),
                  pl.BlockSpec((kb, 2 * N2, 2 * N2), lambda i: (i, 0, 0))],
        out_specs=pl.BlockSpec((kb, 2 * N2, C), lambda i: (i, 0, 0)),
        out_shape=jax.ShapeDtypeStruct((N1, 2 * N2, C), F32),
        scratch_shapes=[pltpu.VMEM((N2, DFT_PITCH_B, C2), F32)],
        compiler_params=_params(("arbitrary",)),
    )(ah, mf)


def _dft_mid_kernel(a_ref, mf_ref, k_ref, g_ref, scr, *, kb, nb, C, N2):
    for b in range(nb):
        scr[:, 0:2 * kb, :] = a_ref[b].astype(F32)
        for j in range(kb):
            kre = k_ref[j, 0:N2, :]
            kim = k_ref[j, N2:2 * N2, :]
            y = _dot(mf_ref[j], _stage_b_operand(scr, j))
            yre, yim = y[0:N2], y[N2:2 * N2]
            z = jnp.concatenate([yre * kre - yim * kim, yre * kim + yim * kre], 0).astype(BF16)
            g = lax.dot_general(mf_ref[j], z, (((0,), (0,)), ((), ())), preferred_element_type=F32)
            g_ref[b, j] = g.astype(g_ref.dtype)


def _dft_mid(a, mf, kf):
    Bx, N2, R, C = a.shape
    N1, kb = R // 2, DFT_KB
    kern = functools.partial(_dft_mid_kernel, kb=kb, nb=Bx, C=C, N2=N2)
    return pl.pallas_call(
        kern, name="dft_mid", grid=(N1 // kb,),
        in_specs=[pl.BlockSpec((Bx, N2, 2 * kb, C), lambda i: (0, 0, i, 0)),
                  pl.BlockSpec((kb, 2 * N2, 2 * N2), lambda i: (i, 0, 0)),
                  pl.BlockSpec((kb, 2 * N2, C), lambda i: (i, 0, 0))],
        out_specs=pl.BlockSpec((Bx, kb, 2 * N2, C), lambda i: (0, i, 0, 0)),
        out_shape=jax.ShapeDtypeStruct((Bx, N1, 2 * N2, C), BF16),
        scratch_shapes=[pltpu.VMEM((N2, DFT_PITCH_B, C), F32)],
        compiler_params=_params(("arbitrary",)),
    )(a, mf, kf)


def _dft_a_inv_kernel(g_ref, f_ref, ulo_ref, uhi_ref, elo_ref, ehi_ref, s_ref, o_ref, scr):
    f = f_ref[...]
    for ri in range(2):
        scr[:, ri * DFT_TT:(ri + 1) * DFT_TT, :] = g_ref[0, :, ri].astype(F32)
    for j in range(DFT_TT):
        g = jnp.concatenate([scr[:, j, :], scr[:, DFT_TT + j, :]], 0).astype(BF16)
        y = _dot(f, g)
        u_ref, e_ref = (ulo_ref, elo_ref) if j < SUBLANES else (uhi_ref, ehi_ref)
        jj = j % SUBLANES
        o_ref[0, :, j, :] = (y + u_ref[0, :, jj, :] * s_ref[...]) * e_ref[0, :, jj, :]


def _dft_a_inv(g5, fai, u4, e4, skip):
    Bx, N1, _, N2, C = g5.shape
    H1 = fai.shape[0]
    tok = _halves(lambda r: (1, H1, r, C), lambda b, j, half: (b, 0, 2 * j + half, 0))
    return pl.pallas_call(
        _dft_a_inv_kernel, name="dft_a_inv", grid=(Bx, N2 // DFT_TT),
        in_specs=[pl.BlockSpec((1, N1, 2, DFT_TT, C), lambda b, j: (b, 0, 0, j, 0)), _const_spec((H1, 2 * N1))]
        + tok + tok + [_const_spec((1, C))],
        out_specs=pl.BlockSpec((1, H1, DFT_TT, C), lambda b, j: (b, 0, j, 0)),
        out_shape=jax.ShapeDtypeStruct((Bx, H1, N2, C), F32),
        scratch_shapes=[pltpu.VMEM((N1, DFT_PITCH_A, C), F32)],
        compiler_params=_params(("parallel", "arbitrary")),
    )(g5, fai, u4, u4, e4, e4, skip)


def _hy_long_conv(u, e, h2, skip, tabs):
    B, L, C = u.shape
    fa, fai, mf = tabs
    N2 = DFT_N2
    N1 = 2 * L // N2
    H1 = N1 // 2
    kf = _dft_b_filter(_dft_a_fwd(h2.reshape(1, H1, N2, 2 * C), fa), mf)
    u4 = u.reshape(B, H1, N2, C)
    g = _dft_mid(_dft_a_fwd(u4, fa), mf, kf)
    y = _dft_a_inv(g.reshape(B, N1, 2, N2, C), fai, u4, e.reshape(B, H1, N2, C), skip.reshape(1, C))
    return y.reshape(B, L, C)


def _ctx_dft_tables(Lc):
    n = 2 * Lc
    k = jnp.arange(n, dtype=jnp.int32)
    t = jnp.arange(Lc, dtype=jnp.int32)
    ang = (2.0 * math.pi / n) * ((k[:, None] * t[None, :]) % n).astype(F32)
    fc = jnp.concatenate([jnp.cos(ang), -jnp.sin(ang)], 0)
    fi = fc.T * (1.0 / n)
    return fc.astype(BF16), fi.astype(BF16)


def _hy_ctx_conv_kernel(u_ref, e_ref, h_ref, fc_ref, fi_ref, s_ref, o_ref, *, n, C):
    u = u_ref[0]
    fc = fc_ref[...]
    uf = _dot(fc, u.astype(BF16))
    hf = _dot(fc, h_ref[...].astype(BF16))
    kre = hf[0:n, 0:C] + hf[0:n, C:2 * C]
    kim = hf[n:2 * n, 0:C] - hf[n:2 * n, C:2 * C]
    ure, uim = uf[0:n], uf[n:2 * n]
    z = jnp.concatenate([ure * kre - uim * kim, ure * kim + uim * kre], 0).astype(BF16)
    y = _dot(fi_ref[...], z)
    o_ref[0] = (y + u * s_ref[...]) * e_ref[0]


def _hy_ctx_conv(u, e, h2, skip, tabs):
    B, Lc, C = u.shape
    fc, fi = tabs
    n = 2 * Lc
    kern = functools.partial(_hy_ctx_conv_kernel, n=n, C=C)
    blk = pl.BlockSpec((1, Lc, C), lambda b: (b, 0, 0))
    return pl.pallas_call(
        kern, name="hy_ctx_conv", grid=(B,),
        in_specs=[blk, blk, _const_spec((Lc, 2 * C)), _const_spec((2 * n, Lc)), _const_spec((Lc, 2 * n)),
                  _const_spec((1, C))],
        out_specs=blk,
        out_shape=jax.ShapeDtypeStruct((B, Lc, C), F32),
        compiler_params=_params(("arbitrary",)),
    )(u, e, h2, fc, fi, skip.reshape(1, C))


def _rope_tables(S, lane_dim, lane_on, head_rot, identity=False):
    if identity:
        return (jnp.ones((S, LANES), F32), jnp.zeros((S, LANES), F32), jnp.zeros((S, LANES), F32))
    seg_w = head_rot // 2
    half = seg_w // 2
    seg = lane_dim // seg_w
    w = lane_dim % seg_w
    first = w < half
    inv = ROPE_BASE ** (-(w % half).astype(F32) / half)
    nrow = S // GRID_W
    ang_r = jnp.arange(nrow, dtype=jnp.int32).astype(F32)[:, None] * inv[None, :]
    ang_c = jnp.arange(GRID_W, dtype=jnp.int32).astype(F32)[:, None] * inv[None, :]

    def expand(fn):
        tab = jnp.where((seg == 0)[None, None, :], fn(ang_r)[:, None, :], fn(ang_c)[None, :, :])
        return tab.reshape(S, LANES)

    cos, sin = expand(jnp.cos), expand(jnp.sin)
    on = lane_on[None, :]
    c = jnp.where(on, cos, 1.0)
    sm = jnp.where(on & first[None, :], -sin, 0.0)
    sp = jnp.where(on & (~first)[None, :], sin, 0.0)
    return c, sm, sp


def _rope(x, c, sm, sp, hh):
    return x * c + pltpu.roll(x, LANES - hh, 1) * sm + pltpu.roll(x, hh, 1) * sp


MLA_VROWS = 80

def _mla_q_kernel(cq_ref, g_ref, w_ref, wr_ref, c_ref, sm_ref, sp_ref, o_ref, *, nh, scale):
    x = cq_ref[0]
    xn = (x * lax.rsqrt(jnp.mean(x * x, -1, keepdims=True) + EPS) * g_ref[...]).astype(BF16)
    q = _dot(xn, w_ref[...])
    qr = _dot(xn, wr_ref[...])
    c = c_ref[...]
    s = sm_ref[...] + sp_ref[...]
    for h in range(nh):
        qh = (q[:, h * LANES:(h + 1) * LANES] * c + qr[:, h * LANES:(h + 1) * LANES] * s) * scale
        o_ref[0, h * LANES:(h + 1) * LANES, :] = qh.T.astype(o_ref.dtype)


def _mla_q(cq, qnorm, w_uq_p, w_uq_r, tabs, tm):
    Bx, S, R = cq.shape
    N = w_uq_p.shape[1]
    kern = functools.partial(_mla_q_kernel, nh=MLA_HEADS, scale=MLA_QK ** -0.5 * LOG2E)
    tab = pl.BlockSpec((tm, LANES), lambda b, i: (i, 0))
    return pl.pallas_call(
        kern, name="mla_q", grid=(Bx, S // tm),
        in_specs=[pl.BlockSpec((1, tm, R), lambda b, i: (b, i, 0)), _const_spec((1, R)), _const_spec((R, N)),
                  _const_spec((R, N)), tab, tab, tab],
        out_specs=pl.BlockSpec((1, N, tm), lambda b, i: (b, 0, i)),
        out_shape=jax.ShapeDtypeStruct((Bx, N, S), BF16),
        compiler_params=_params(("parallel", "arbitrary")),
    )(cq, qnorm.reshape(1, R), w_uq_p, w_uq_r, *tabs)


def _mla_kv_kernel(ckv_ref, kr_ref, g_ref, wk_ref, wv_ref, c_ref, sm_ref, sp_ref, k_ref, vt_ref, *, nh, hh, tm):
    x = ckv_ref[0]
    xn = (x * lax.rsqrt(jnp.mean(x * x, -1, keepdims=True) + EPS) * g_ref[...]).astype(BF16)
    kn = _dot(xn, wk_ref[...])
    krr = _rope(kr_ref[0], c_ref[...], sm_ref[...], sp_ref[...], hh)
    for h in range(nh):
        k_ref[0, :, h * LANES:(h + 1) * LANES] = (kn[:, h * LANES:(h + 1) * LANES] + krr).astype(k_ref.dtype)
    vt = _dot(xn, wv_ref[...]).T
    pad = MLA_VROWS - MLA_V
    ones_rows = jnp.where(lax.broadcasted_iota(jnp.int32, (pad, tm), 0) == 0, 1.0, 0.0).astype(vt_ref.dtype)
    for h in range(nh):
        vt_ref[0, 0, h * MLA_VROWS:h * MLA_VROWS + MLA_V, :] = vt[h * MLA_V:(h + 1) * MLA_V].astype(vt_ref.dtype)
        vt_ref[0, 0, h * MLA_VROWS + MLA_V:(h + 1) * MLA_VROWS, :] = ones_rows


def _mla_kv(ckv, kr, kvnorm, wk_p, wv, tabs, tm):
    Bx, S, R = ckv.shape
    NK, NV = wk_p.shape[1], wv.shape[1]
    kern = functools.partial(_mla_kv_kernel, nh=MLA_HEADS, hh=MLA_ROPE // 4, tm=tm)
    tab = pl.BlockSpec((tm, LANES), lambda b, i: (i, 0))
    VR = MLA_HEADS * MLA_VROWS
    return pl.pallas_call(
        kern, name="mla_kv", grid=(Bx, S // tm),
        in_specs=[pl.BlockSpec((1, tm, R), lambda b, i: (b, i, 0)),
                  pl.BlockSpec((1, tm, LANES), lambda b, i: (b, i, 0)),
                  _const_spec((1, R)), _const_spec((R, NK)), _const_spec((R, NV)), tab, tab, tab],
        out_specs=[pl.BlockSpec((1, tm, NK), lambda b, i: (b, i, 0)),
                   pl.BlockSpec((1, 1, VR, tm), lambda b, i: (b, i, 0, 0))],
        out_shape=[jax.ShapeDtypeStruct((Bx, S, NK), BF16), jax.ShapeDtypeStruct((Bx, S // tm, VR, tm), BF16)],
        compiler_params=_params(("parallel", "arbitrary")),
    )(ckv, kr, kvnorm.reshape(1, R), wk_p, wv, *tabs)


def _mla_attn_kernel(*refs, tq, tk, n_chunks):
    n_in = 7 if n_chunks else 5
    if n_chunks:
        qt_ref, kc_ref, vct_ref, k_ref, vt_ref, gate_ref, o_ref = refs[:n_in]
    else:
        qt_ref, kc_ref, vct_ref, gate_ref, o_ref = refs[:n_in]
    acc_scr = refs[n_in:n_in + 2]
    VR = MLA_VROWS
    qs = [qt_ref[0, j * LANES:(j + 1) * LANES, :] for j in range(2)]

    def accumulate(j, m, st, mx, vblk):
        m_new = mx if m is None else jnp.maximum(m, mx)
        pt = jnp.exp2(st - m_new).astype(BF16)
        pv = _dot(vblk, pt)
        acc_scr[j][...] = pv if m is None else jnp.exp2(m - m_new) * acc_scr[j][...] + pv
        return m_new

    Lc = kc_ref.shape[1]

    def ctx_scores(j):
        return _dot(kc_ref[0, :, j * LANES:(j + 1) * LANES], qs[j])

    if not n_chunks:
        for j in range(2):
            st = ctx_scores(j)
            accumulate(j, None, st, jnp.max(st, 0, keepdims=True), vct_ref[0, 0, j * VR:(j + 1) * VR, :])
    else:
        s_scr = refs[n_in + 2:]

        def scores(j, c, slot):
            st = pl.multiple_of(c * tk, tk)
            blk = _dot(k_ref[0, pl.ds(st, tk), j * LANES:(j + 1) * LANES], qs[j])
            s_scr[2 * j + slot][...] = blk
            return jnp.max(blk, 0, keepdims=True)

        def scores_ctx(j, slot):
            blk = ctx_scores(j)
            s_scr[2 * j + slot][0:Lc, :] = blk
            return jnp.max(blk, 0, keepdims=True)

        def consume(j, c, slot, mj, mxj):
            return accumulate(j, mj, s_scr[2 * j + slot][...], mxj, vt_ref[0, c, j * VR:(j + 1) * VR, :])

        m = [jnp.full((1, tq), NEG, F32) for _ in range(2)]
        for j in range(2):
            acc_scr[j][...] = jnp.zeros((VR, tq), F32)
        mx = [scores(j, 0, 0) for j in range(2)]

        unroll = 4 if n_chunks % 4 == 0 else 2

        def steps(c, m, mx, last):
            for s in range(unroll):
                slot = s % 2
                if last and s == unroll - 1:
                    nxt = [scores_ctx(j, 1 - slot) for j in range(2)]
                else:
                    nxt = [scores(j, c + s + 1, 1 - slot) for j in range(2)]
                for j in range(2):
                    m[j] = consume(j, c + s, slot, m[j], mx[j])
                mx = nxt
            return m, mx

        def body(ci, carry):
            m0, x0, m1, x1 = carry
            m, mx = steps(unroll * ci, [m0, m1], [x0, x1], False)
            return m[0], mx[0], m[1], mx[1]

        m0, x0, m1, x1 = lax.fori_loop(0, n_chunks // unroll - 1, body, (m[0], mx[0], m[1], mx[1]))
        m, mx = steps(n_chunks - unroll, [m0, m1], [x0, x1], True)
        cslot = unroll % 2
        for j in range(2):
            accumulate(j, m[j], s_scr[2 * j + cslot][0:Lc, :], mx[j], vct_ref[0, 0, j * VR:(j + 1) * VR, :])
    yt = jnp.concatenate([acc_scr[j][0:MLA_V, :] / acc_scr[j][MLA_V:MLA_V + 1, :] for j in range(2)], 0)
    o_ref[0] = yt.T * _silu(gate_ref[0])


def _mla_attn(qt, kc, vct, k, vt, gate, tq):
    B, _, S = qt.shape
    Lc = kc.shape[1]
    npair = MLA_HEADS // 2
    VR2 = 2 * MLA_VROWS
    n_chunks, tk = (0, 0) if k is None else (vt.shape[1], vt.shape[3])
    kern = functools.partial(_mla_attn_kernel, tq=tq, tk=tk, n_chunks=n_chunks)
    in_specs = [pl.BlockSpec((1, 2 * LANES, tq), lambda b, p, i: (b, p, i)),
                pl.BlockSpec((1, Lc, 2 * LANES), lambda b, p, i: (b, 0, p)),
                pl.BlockSpec((1, 1, VR2, Lc), lambda b, p, i: (b, 0, p, 0))]
    args = [qt, kc, vct]
    if n_chunks:
        Sk = k.shape[1]
        in_specs += [pl.BlockSpec((1, Sk, 2 * LANES), lambda b, p, i: (b, 0, p)),
                     pl.BlockSpec((1, n_chunks, VR2, tk), lambda b, p, i: (b, 0, p, 0))]
        args += [k, vt]
    in_specs.append(pl.BlockSpec((1, tq, LANES), lambda b, p, i: (b, i, p)))
    args.append(gate)
    return pl.pallas_call(
        kern, name="mla_attn", grid=(B, npair, S // tq),
        in_specs=in_specs,
        out_specs=pl.BlockSpec((1, tq, LANES), lambda b, p, i: (b, i, p)),
        out_shape=jax.ShapeDtypeStruct((B, S, npair * LANES), F32),
        scratch_shapes=[pltpu.VMEM((MLA_VROWS, tq), F32)] * 2 + [pltpu.VMEM((tk, tq), F32)] * (4 if n_chunks else 0),
        compiler_params=_params(("parallel", "arbitrary", "arbitrary")),
    )(*args)


SWA_VROWS = 80


def _ones_rows(n, width, dtype):
    return jnp.where(lax.broadcasted_iota(jnp.int32, (n, width), 0) == 0, 1.0, 0.0).astype(dtype)


def _swa_prep_kernel(q_ref, k_ref, v_ref, c_ref, sm_ref, sp_ref, qo_ref, ko_ref, vo_ref, *, scale, hh, nq, tm):
    c, sm, sp = c_ref[...], sm_ref[...], sp_ref[...]
    for r in range(nq):
        qr = _rope(q_ref[0, :, r * LANES:(r + 1) * LANES], c, sm, sp, hh) * scale
        qo_ref[0, r * LANES:(r + 1) * LANES, :] = qr.T.astype(qo_ref.dtype)
    ko_ref[0] = _rope(k_ref[0], c, sm, sp, hh).astype(ko_ref.dtype)
    vt = v_ref[0].T
    hd = LANES // SWA_KV_HEADS
    ones = _ones_rows(SWA_VROWS - hd, tm, vo_ref.dtype)
    for g in range(SWA_KV_HEADS):
        vo_ref[0, g * SWA_VROWS:g * SWA_VROWS + hd, :] = vt[g * hd:(g + 1) * hd].astype(vo_ref.dtype)
        vo_ref[0, g * SWA_VROWS + hd:(g + 1) * SWA_VROWS, :] = ones


def _swa_prep(sq, sk, sv, tabs, tm):
    Bx, S, NQ = sq.shape
    VR = SWA_KV_HEADS * SWA_VROWS
    kern = functools.partial(_swa_prep_kernel, scale=SWA_HD ** -0.5 * LOG2E, hh=SWA_HD // 4, nq=NQ // LANES, tm=tm)
    tab = pl.BlockSpec((tm, LANES), lambda b, i: (i, 0))
    qs = pl.BlockSpec((1, tm, NQ), lambda b, i: (b, i, 0))
    ks = pl.BlockSpec((1, tm, LANES), lambda b, i: (b, i, 0))
    return pl.pallas_call(
        kern, name="swa_prep", grid=(Bx, S // tm),
        in_specs=[qs, ks, ks, tab, tab, tab],
        out_specs=[pl.BlockSpec((1, NQ, tm), lambda b, i: (b, 0, i)), ks,
                   pl.BlockSpec((1, VR, tm), lambda b, i: (b, 0, i))],
        out_shape=[jax.ShapeDtypeStruct((Bx, NQ, S), BF16), jax.ShapeDtypeStruct((Bx, S, LANES), BF16),
                   jax.ShapeDtypeStruct((Bx, VR, S), BF16)],
        compiler_params=_params(("parallel", "arbitrary")),
    )(sq, sk, sv, *tabs)


def _sink_column(sink_ref, g, R, W):
    rid = lax.broadcasted_iota(jnp.int32, (R * W, 1), 0)
    col = jnp.full((R * W, 1), sink_ref[g * R + R - 1], F32)
    for r in range(R - 2, -1, -1):
        col = jnp.where(rid < (r + 1) * W, sink_ref[g * R + r], col)
    return col


def _swa_group(qg, sk2, blocks):
    ss = []
    for kk, _, bias in blocks:
        s = _dot_nt(qg, kk)
        ss.append(s if bias is None else s + bias)
    m = sk2
    for s in ss:
        m = jnp.maximum(m, jnp.max(s, -1, keepdims=True))
    o = None
    for s, (_, vv, _) in zip(ss, blocks):
        pv = _dot(jnp.exp2(s - m).astype(BF16), vv)
        o = pv if o is None else o + pv
    den = pltpu.roll(o, LANES // 2, 1) + jnp.exp2(sk2 - m)
    return o / den


def _group_select(x, g, fill):
    lane = lax.broadcasted_iota(jnp.int32, (1, LANES), 1)
    sel = (lane < LANES // 2) if g == 0 else (lane >= LANES // 2)
    return jnp.where(sel, x, jnp.full_like(x, fill))


def _swa_attn_kernel(sink_ref, qt_ref, kp_ref, km_ref, kn_ref, vp_ref, vm_ref, vn_ref, kc_ref, vct_ref, gate_ref,
                     o_ref, *s_scr, bpt, W, R, G, Lc):
    i = pl.program_id(1)
    nblk = pl.num_programs(1) * bpt
    hd = LANES // G
    VR = SWA_VROWS
    kcat = jnp.concatenate([kp_ref[0], km_ref[0], kn_ref[0]], 0)
    vcat = jnp.concatenate([vp_ref[0], vm_ref[0], vn_ref[0]], 1)
    kc = kc_ref[0]
    nk = Lc + 3 * W
    kj = lax.broadcasted_iota(jnp.int32, (nk, R * W), 0) - Lc
    qi = lax.broadcasted_iota(jnp.int32, (nk, R * W), 1) % W
    band_bias = jnp.where((kj < 0) | (jnp.abs(W + qi - kj) <= W), 0.0, NEG)
    head = lax.broadcasted_iota(jnp.int32, (1, R * W), 1) // W
    row = lax.broadcasted_iota(jnp.int32, (LANES, 1), 0)
    sk2 = []
    for g in range(G):
        sk = jnp.full((1, R * W), sink_ref[g * R + R - 1], F32)
        for r in range(R - 2, -1, -1):
            sk = jnp.where(head <= r, sink_ref[g * R + r], sk)
        sk2.append(sk * LOG2E)
    kj_col = lax.broadcasted_iota(jnp.int32, (nk, 1), 0) - Lc
    items = [(jb, g) for jb in range(bpt) for g in range(G)]

    def scores(n):
        jb, g = items[n]
        gblk = i * bpt + jb
        keys = jnp.concatenate([kc, kcat[jb * W:(jb + 3) * W]], 0)
        pen_prev = jnp.where(gblk > 0, 0.0, NEG)
        pen_next = jnp.where(gblk < nblk - 1, 0.0, NEG)
        pen = jnp.where((kj_col >= 0) & (kj_col < W), pen_prev, jnp.where(kj_col >= 2 * W, pen_next, 0.0))
        sel = (row >= g * hd) & (row < (g + 1) * hd)
        tiles = [qt_ref[0, r * LANES:(r + 1) * LANES, jb * W:(jb + 1) * W] for r in range(R)]
        qg = jnp.concatenate([jnp.where(sel, t, jnp.zeros_like(t)) for t in tiles], 1)
        st = (_dot(keys, qg) + band_bias) + pen
        s_scr[n % 2][...] = st
        return jnp.max(st, 0, keepdims=True)

    def consume(n, mx):
        jb, g = items[n]
        m = jnp.maximum(mx, sk2[g])
        pt = jnp.exp2(s_scr[n % 2][...] - m).astype(BF16)
        o = (_dot(vct_ref[0, g * VR:(g + 1) * VR, :], pt[0:Lc])
             + _dot(vcat[g * VR:(g + 1) * VR, jb * W:(jb + 3) * W], pt[Lc:nk]))
        den = o[hd:hd + 1] + jnp.exp2(sk2[g] - m)
        return o[0:hd] / den

    mx = scores(0)
    outg = []
    for n, (jb, g) in enumerate(items):
        nxt = scores(n + 1) if n + 1 < len(items) else None
        outg.append(consume(n, mx))
        mx = nxt
        if g == G - 1:
            for r in range(R):
                yt = jnp.concatenate([og[:, r * W:(r + 1) * W] for og in outg], 0)
                gt = gate_ref[0, jb * W:(jb + 1) * W, r * LANES:(r + 1) * LANES]
                o_ref[0, jb * W:(jb + 1) * W, r * LANES:(r + 1) * LANES] = yt.T * _silu(gt)
            outg = []


def _swa_attn(qt, k, vt, kc, vct, sink, gate, bpt):
    B, NQ, S = qt.shape
    Lc = kc.shape[1]
    W = WINDOW
    G = SWA_KV_HEADS
    R = SWA_HEADS // G
    VR = G * SWA_VROWS
    nb = S // W
    T = bpt * W
    kern = functools.partial(_swa_attn_kernel, bpt=bpt, W=W, R=R, G=G, Lc=Lc)
    main = pl.BlockSpec((1, T, LANES), lambda b, i: (b, i, 0))
    prev = pl.BlockSpec((1, W, LANES), lambda b, i: (b, jnp.maximum(i * bpt - 1, 0), 0))
    nxt = pl.BlockSpec((1, W, LANES), lambda b, i: (b, jnp.minimum((i + 1) * bpt, nb - 1), 0))
    main_t = pl.BlockSpec((1, VR, T), lambda b, i: (b, 0, i))
    prev_t = pl.BlockSpec((1, VR, W), lambda b, i: (b, 0, jnp.maximum(i * bpt - 1, 0)))
    nxt_t = pl.BlockSpec((1, VR, W), lambda b, i: (b, 0, jnp.minimum((i + 1) * bpt, nb - 1)))
    tok = pl.BlockSpec((1, T, NQ), lambda b, i: (b, i, 0))
    return pl.pallas_call(
        kern, name="swa_attn", grid=(B, S // T),
        in_specs=[pl.BlockSpec(memory_space=pltpu.SMEM), pl.BlockSpec((1, NQ, T), lambda b, i: (b, 0, i)),
                  prev, main, nxt, prev_t, main_t, nxt_t,
                  pl.BlockSpec((1, Lc, LANES), lambda b, i: (b, 0, 0)),
                  pl.BlockSpec((1, VR, Lc), lambda b, i: (b, 0, 0)), tok],
        out_specs=tok,
        out_shape=jax.ShapeDtypeStruct((B, S, NQ), F32),
        scratch_shapes=[pltpu.VMEM((Lc + 3 * W, R * W), F32)] * 2,
        compiler_params=_params(("parallel", "arbitrary")),
    )(sink, qt, k, k, k, vt, vt, vt, kc, vct, gate)


def _swa_ctx_kernel(sink_ref, q_ref, kc_ref, vc_ref, gate_ref, o_ref, *, Lc, R, G, scale):
    kc = kc_ref[0]
    lane = lax.broadcasted_iota(jnp.int32, (Lc, LANES), 1)
    tiles = [(q_ref[0, :, r * LANES:(r + 1) * LANES] * scale).astype(BF16) for r in range(R)]
    outg = []
    for g in range(G):
        qg = jnp.concatenate([_group_select(t, g, 0.0) for t in tiles], 0)
        outg.append(_swa_group(qg, _sink_column(sink_ref, g, R, Lc) * LOG2E,
                               [(kc, _group_select(vc_ref[0], g, 1.0), None)]))
    for r in range(R):
        y = jnp.where(lane < LANES // 2, outg[0][r * Lc:(r + 1) * Lc], outg[1][r * Lc:(r + 1) * Lc])
        o_ref[0, :, r * LANES:(r + 1) * LANES] = y * _silu(gate_ref[0, :, r * LANES:(r + 1) * LANES])


def _swa_ctx(q, kc, vc, sink, gate):
    B, Lc, NQ = q.shape
    G = SWA_KV_HEADS
    R = SWA_HEADS // G
    kern = functools.partial(_swa_ctx_kernel, Lc=Lc, R=R, G=G, scale=SWA_HD ** -0.5 * LOG2E)
    qs = pl.BlockSpec((1, Lc, NQ), lambda b: (b, 0, 0))
    ctx = pl.BlockSpec((1, Lc, LANES), lambda b: (b, 0, 0))
    return pl.pallas_call(
        kern, name="swa_ctx", grid=(B,),
        in_specs=[pl.BlockSpec(memory_space=pltpu.SMEM), qs, ctx, ctx, qs],
        out_specs=qs,
        out_shape=jax.ShapeDtypeStruct((B, Lc, NQ), F32),
        compiler_params=_params(("arbitrary",)),
    )(sink, q, kc, vc, gate)


def _merge_kernel(yh_ref, ym_ref, ys_ref, mg_ref, x_ref, mod_ref, wh, wm, ws, wo, lg, lb, o_ref, *, D, alpha):
    m = (jax.nn.sigmoid(mg_ref[0, :, 0:D]) * _dot(yh_ref[0].astype(BF16), wh[...])
         + jax.nn.sigmoid(mg_ref[0, :, D:2 * D]) * _dot(ym_ref[0].astype(BF16), wm[...])
         + jax.nn.sigmoid(mg_ref[0, :, 2 * D:3 * D]) * _dot(ys_ref[0].astype(BF16), ws[...]))
    out = _dot(m.astype(BF16), wo[...])
    r = alpha * x_ref[0] + mod_ref[0, :, 2 * D:3 * D] * out
    mu = jnp.mean(r, -1, keepdims=True)
    rc = r - mu
    var = jnp.mean(rc * rc, -1, keepdims=True)
    o_ref[0] = rc * lax.rsqrt(var + EPS) * lg[...] + lb[...]


def _merge(yh, ym, ys, mg, x, mod, wh, wm, ws, wo, lg, lb, alpha, tm):
    Bx, S, D = x.shape
    kern = functools.partial(_merge_kernel, D=D, alpha=alpha)
    tok = lambda w: pl.BlockSpec((1, tm, w), lambda b, i: (b, i, 0))
    cs = _const_spec
    return pl.pallas_call(
        kern, name="merge", grid=(Bx, S // tm),
        in_specs=[tok(yh.shape[2]), tok(ym.shape[2]), tok(ys.shape[2]), tok(3 * D), tok(D),
                  pl.BlockSpec((1, 1, 3 * D), lambda b, i: (b, 0, 0)),
                  cs(wh.shape), cs(wm.shape), cs(ws.shape), cs(wo.shape), cs((1, D)), cs((1, D))],
        out_specs=tok(D),
        out_shape=jax.ShapeDtypeStruct((Bx, S, D), F32),
        compiler_params=_params(("parallel", "arbitrary")),
    )(yh, ym, ys, mg, x, mod, wh, wm, ws, wo, lg.reshape(1, D), lb.reshape(1, D))


_PAIR_ORDER = (0, 4, 1, 5, 2, 6, 3, 7)

_IN_COLS = (3 * HY_W, HY_W, MLA_Q_RANK, MLA_KV_RANK, MLA_ROPE, MLA_HEADS * MLA_V,
            SWA_HEADS * SWA_HD, SWA_KV_HEADS * SWA_HD, SWA_KV_HEADS * SWA_HD, SWA_HEADS * SWA_HD)
_SLAB_WIDTHS = (None, 3 * HY_W, HY_W, MLA_Q_RANK, MLA_KV_RANK, LANES, MLA_HEADS * MLA_V,
                SWA_HEADS * SWA_HD, LANES, LANES, SWA_HEADS * SWA_HD)


def _pair_cols(w):
    D = w.shape[0]
    return w.reshape(D, SWA_HEADS, SWA_HD)[:, _PAIR_ORDER, :].reshape(D, SWA_HEADS * SWA_HD)


def _layout_w_in(w_in, D):
    offs = [0]
    for cw in _IN_COLS:
        offs.append(offs[-1] + cw)
    sl = [w_in[:, offs[i]:offs[i + 1]] for i in range(len(_IN_COLS))]
    hy, hyg, cq, ckv, kr, mlag, sq, sk, sv, swag = sl
    mg = w_in[:, offs[-1]:]
    kr_p = jnp.pad(kr, ((0, 0), (MLA_NOPE, LANES - MLA_NOPE - MLA_ROPE)))
    return jnp.concatenate([mg, hy, hyg, cq, ckv, kr_p, mlag, _pair_cols(sq), sk, sv, _pair_cols(swag)],
                           1).astype(BF16)


def _layer_weights(p, D):
    w = {}
    w['w_in'] = _layout_w_in(p['w_in'], D)
    uq = p['mla_w_uq'].reshape(MLA_Q_RANK, MLA_HEADS, MLA_QK)
    w['w_uq'] = jnp.pad(uq, ((0, 0), (0, 0), (0, LANES - MLA_QK))).reshape(MLA_Q_RANK, MLA_HEADS * LANES).astype(BF16)
    half = MLA_ROPE // 4
    rot = uq[:, :, MLA_NOPE:].reshape(MLA_Q_RANK, MLA_HEADS, 2, 2, half)[:, :, :, ::-1, :]
    rot = rot.reshape(MLA_Q_RANK, MLA_HEADS, MLA_ROPE)
    w['w_uq_r'] = jnp.pad(rot, ((0, 0), (0, 0), (MLA_NOPE, LANES - MLA_QK))).reshape(
        MLA_Q_RANK, MLA_HEADS * LANES).astype(BF16)
    ukv = p['mla_w_ukv'].reshape(MLA_KV_RANK, MLA_HEADS, MLA_NOPE + MLA_V)
    w['w_uk'] = jnp.pad(ukv[:, :, :MLA_NOPE], ((0, 0), (0, 0), (0, LANES - MLA_NOPE))).reshape(
        MLA_KV_RANK, MLA_HEADS * LANES).astype(BF16)
    w['w_uv'] = ukv[:, :, MLA_NOPE:].reshape(MLA_KV_RANK, MLA_HEADS * MLA_V).astype(BF16)
    w['w_proj_hy'] = p['w_proj_hy'].astype(BF16)
    w['w_proj_mla'] = p['w_proj_mla'].astype(BF16)
    w['w_proj_swa'] = p['w_proj_swa'].reshape(SWA_HEADS, SWA_HD, D)[_PAIR_ORDER, :, :].reshape(
        SWA_HEADS * SWA_HD, D).astype(BF16)
    w['w_out'] = p['w_out'].astype(BF16)
    return w


def _trunk_layer(xl, xc, p, tabs, alpha, ctx_out):
    B, S, D = xl.shape
    Lc = xc.shape[1]
    w = _layer_weights(p, D)
    widths = (N_BRANCH * D,) + _SLAB_WIDTHS[1:]

    mod = p['mod']
    mod_l = mod[:B].reshape(B, 1, 3 * D)
    mod_c = jnp.broadcast_to(mod[B].reshape(1, 1, 3 * D), (B, 1, 3 * D))

    tm_c = min(Lc, 256)
    (mg_l, hy_l, hyg_l, cq_l, ckv_l, kr_l, mlag_l, sq_l, sk_l, sv_l, swag_l) = _in_proj(xl, mod_l, w['w_in'], widths, 512)
    (mg_c, hy_c, hyg_c, cq_c, ckv_c, kr_c, mlag_c, sq_c, sk_c, sv_c, swag_c) = _in_proj(xc, mod_c, w['w_in'], widths, tm_c)

    k_mc, v_mc = _mla_kv(ckv_c, kr_c, p['mla_kv_norm'], w['w_uk'], w['w_uv'], tabs['id_c'], tm_c)
    k_sc = sk_c.astype(BF16)
    v_sc = sv_c.astype(BF16)

    u_l, e_l = _hy_pre(hy_l, hyg_l, p['hy_conv_w'], p['hy_conv_b'], 512)
    h2_l = _hy_filters(S, p)
    y_hy = _hy_long_conv(u_l, e_l, h2_l, p['hy_skip'], tabs['dft'])

    q_ml = _mla_q(cq_l, p['mla_q_norm'], w['w_uq'], w['w_uq_r'], tabs['mla'], 512)
    k_ml, v_ml = _mla_kv(ckv_l, kr_l, p['mla_kv_norm'], w['w_uk'], w['w_uv'], tabs['mla'], 512)
    y_mla = _mla_attn(q_ml, k_mc, v_mc, k_ml, v_ml, mlag_l, 512)

    q_sl, k_sl, v_sl = _swa_prep(sq_l, sk_l, sv_l, tabs['swa'], 512)
    hd = LANES // SWA_KV_HEADS
    ones = jnp.broadcast_to(_ones_rows(SWA_VROWS - hd, Lc, BF16)[None], (B, SWA_VROWS - hd, Lc))
    v_sct = jnp.swapaxes(v_sc, 1, 2)
    v_sct = jnp.concatenate([v_sct[:, :hd], ones, v_sct[:, hd:], ones], 1)
    y_swa = _swa_attn(q_sl, k_sl, v_sl, k_sc, v_sct, p['swa_sink'], swag_l, 8)

    xl_new = _merge(y_hy, y_mla, y_swa, mg_l, xl, mod_l, w['w_proj_hy'], w['w_proj_mla'], w['w_proj_swa'],
                    w['w_out'], p['ln_g'], p['ln_b'], alpha, 512)
    if not ctx_out:
        return xl_new, xc

    u_c, e_c = _hy_pre(hy_c, hyg_c, p['hy_conv_w'], p['hy_conv_b'], tm_c)
    h2_c = _hy_filters(Lc, p)
    yc_hy = _hy_ctx_conv(u_c, e_c, h2_c, p['hy_skip'], tabs['dft_c'])
    q_mc = _mla_q(cq_c, p['mla_q_norm'], w['w_uq'], w['w_uq_r'], tabs['id_c'], tm_c)
    yc_mla = _mla_attn(q_mc, k_mc, v_mc, None, None, mlag_c, tm_c)
    yc_swa = _swa_ctx(sq_c, k_sc, v_sc, p['swa_sink'], swag_c)
    xc_new = _merge(yc_hy, yc_mla, yc_swa, mg_c, xc, mod_c, w['w_proj_hy'], w['w_proj_mla'], w['w_proj_swa'],
                    w['w_out'], p['ln_g'], p['ln_b'], alpha, tm_c)
    return xl_new, xc_new


def kernel(x, c, ctx, c_ctx, w_ada, b_ada, w_in, hy_conv_w, hy_conv_b, filt_w1, filt_b1, filt_w2, filt_b2,
           filt_w3, filt_b3, filt_freq, filt_w_out, hy_skip, mla_q_norm, mla_w_uq, mla_kv_norm, mla_w_ukv,
           swa_sink, w_proj_hy, w_proj_mla, w_proj_swa, w_out, ln_g, ln_b):
    B, S, D = x.shape
    Lc = ctx.shape[1]
    depth = w_in.shape[0]
    alpha = (2 * depth) ** 0.25
    stacked = dict(w_ada=w_ada, b_ada=b_ada, w_in=w_in, hy_conv_w=hy_conv_w, hy_conv_b=hy_conv_b,
                   filt_w1=filt_w1, filt_b1=filt_b1, filt_w2=filt_w2, filt_b2=filt_b2, filt_w3=filt_w3,
                   filt_b3=filt_b3, filt_freq=filt_freq, filt_w_out=filt_w_out, hy_skip=hy_skip,
                   mla_q_norm=mla_q_norm, mla_w_uq=mla_w_uq, mla_kv_norm=mla_kv_norm, mla_w_ukv=mla_w_ukv,
                   swa_sink=swa_sink, w_proj_hy=w_proj_hy, w_proj_mla=w_proj_mla, w_proj_swa=w_proj_swa,
                   w_out=w_out, ln_g=ln_g, ln_b=ln_b)

    lane = jnp.arange(LANES, dtype=jnp.int32)
    mla_on = (lane >= MLA_NOPE) & (lane < MLA_QK)
    tabs = {
        'mla': _rope_tables(S, jnp.clip(lane - MLA_NOPE, 0, MLA_ROPE - 1), mla_on, MLA_ROPE),
        'swa': _rope_tables(S, lane % SWA_HD, jnp.ones((LANES,), bool), SWA_HD),
        'id_c': _rope_tables(Lc, None, None, None, identity=True),
        'dft': _dft_tables(S),
        'dft_c': _ctx_dft_tables(Lc),
    }
    cvec = jnp.concatenate([c, c_ctx[None, :], jnp.zeros((8 - B - 1, D), F32)], 0)

    xl, xc = x, ctx
    for l in range(depth):
        p = {k: v[l] for k, v in stacked.items() if k != 'w_ada'}
        p['mod'] = _ada_mod(cvec, w_ada, p['b_ada'], l)
        xl, xc = _trunk_layer(xl, xc, p, tabs, alpha, l < depth - 1)
    return xl
```
